```python
import math
import jax
import jax.numpy as jnp
from jax import lax
import numpy as np

D_MODEL = 1024
BATCH = 8
SEQ = 2048
DEPTH = 2
DEC_BATCH = 32
DEC_SEQ = 8
PAST_LEN = 8192
PAGE_SIZE = 128

GLA_HEADS = 4
GLA_VW = D_MODEL // 4
GLA_DV = GLA_VW // GLA_HEADS
GLA_DK = GLA_DV // 2
GLA_QK = GLA_HEADS * GLA_DK
GLA_GATE_RANK = 16
GLA_TAU = 16.0
GLA_CHUNK = 64
CONV_CH = D_MODEL // 4
CONV_WIDTH = 31
DIFF_HEADS = 4
DIFF_VW = D_MODEL // 2
DIFF_DH = DIFF_VW // (2 * DIFF_HEADS)
Q_BLOCK = 128
REL_BUCKETS = 32
REL_MAX_DIST = 128
MIX_WIDTH = GLA_VW + CONV_CH + DIFF_VW
D_FF = -(-(8 * D_MODEL) // (3 * 256)) * 256
PLE_DIM = 256
ALPHA = (2 * DEPTH) ** 0.25
BETA = (8 * DEPTH) ** -0.25
EPS = 1e-5
IN_SIZES = (GLA_QK, GLA_QK, GLA_VW, GLA_VW, GLA_GATE_RANK, 2 * CONV_CH, DIFF_VW, DIFF_VW, DIFF_VW)
N_IN = sum(IN_SIZES)

kernel_name = 'hybrid_gla_conformer_diffattn_decoder_step'

F32 = jnp.float32


def layer_norm(x, g, b):
    xf = x.astype(F32)
    mu = jnp.mean(xf, axis=-1, keepdims=True)
    var = jnp.mean(jnp.square(xf - mu), axis=-1, keepdims=True)
    y = (xf - mu) * lax.rsqrt(var + EPS) * g.astype(F32) + b.astype(F32)
    return y.astype(x.dtype)


def rms_norm(x, g):
    xf = x.astype(F32)
    y = xf * lax.rsqrt(jnp.mean(jnp.square(xf), axis=-1, keepdims=True) + EPS) * g.astype(F32)
    return y.astype(x.dtype)


def split_in(u):
    idx, s = [], 0
    for n in IN_SIZES[:-1]:
        s += n
        idx.append(s)
    return jnp.split(u, idx, axis=-1)


def rel_bucket(dist):
    n = jnp.maximum(dist, 0)
    max_exact = REL_BUCKETS // 2
    nf = jnp.maximum(n, 1).astype(F32)
    large = max_exact + (jnp.log(nf / max_exact) / math.log(REL_MAX_DIST / max_exact)
                         * (REL_BUCKETS - max_exact)).astype(jnp.int32)
    large = jnp.minimum(large, REL_BUCKETS - 1)
    return jnp.where(n < max_exact, n, large)


def gla_recurrence(q, k, v, lg, s0, chunk):
    B, T, H, _ = q.shape
    DV = v.shape[-1]
    nc = T // chunk

    def to_chunks(t):
        return t.astype(F32).reshape(B, nc, chunk, H, -1).transpose(1, 0, 3, 2, 4)

    causal = jnp.tril(jnp.ones((chunk, chunk), bool))

    def step(s, inp):
        qc, kc, vc, gc = inp
        b = jnp.cumsum(gc, axis=2)
        o = jnp.einsum('bhcd,bhde->bhce', qc * jnp.exp(b), s)
        decay = jnp.exp(jnp.where(causal[:, :, None], b[:, :, :, None, :] - b[:, :, None, :, :], -jnp.inf))
        a = jnp.einsum('bhid,bhjd,bhijd->bhij', qc, kc, decay)
        o = o + jnp.einsum('bhij,bhje->bhie', a, vc)
        b_end = b[:, :, -1:, :]
        s = jnp.exp(b_end[:, :, 0, :, None]) * s + jnp.einsum('bhjd,bhje->bhde', kc * jnp.exp(b_end - b), vc)
        return s, o

    s, o = lax.scan(step, s0.astype(F32), (to_chunks(q), to_chunks(k), to_chunks(v), to_chunks(lg)))
    return o.transpose(1, 0, 3, 2, 4).reshape(B, T, H, DV), s


def conformer_conv(glu_in, buf, w, b, g, beta):
    a, gate = jnp.split(glu_in, 2, axis=-1)
    u = a * jax.nn.sigmoid(gate)
    seq = jnp.concatenate([buf.astype(u.dtype), u], axis=1)
    y = lax.conv_general_dilated(seq, w.astype(u.dtype)[:, None, :], window_strides=(1,), padding='VALID',
                                 dimension_numbers=('NWC', 'WIO', 'NWC'),
                                 feature_group_count=u.shape[-1]) + b
    y = jax.nn.silu(layer_norm(y, g, beta))
    return y, seq[:, -(CONV_WIDTH - 1):]


def diff_weights(logits, dist, rel_bias, lam):
    bias = jnp.moveaxis(rel_bias[rel_bucket(dist)], -1, 0).astype(F32)
    logits = jnp.where(dist >= 0, logits + bias, -jnp.inf)
    p = jax.nn.softmax(logits, axis=-1)
    b_, h2, nq, nk = p.shape
    p = p.reshape(b_, h2 // 2, 2, nq, nk)
    return p[:, :, 0] - lam * p[:, :, 1]


def diff_attn_prompt(q, k, v, lam, rel_bias):
    B, S = q.shape[:2]
    nb = S // Q_BLOCK
    qb = jnp.moveaxis(q.reshape(B, nb, Q_BLOCK, *q.shape[2:]), 1, 0)
    starts = jnp.arange(nb, dtype=jnp.int32) * Q_BLOCK
    kpos = jnp.arange(S, dtype=jnp.int32)

    def block(args):
        qblk, start = args
        logits = jnp.einsum('bqhd,bkhd->bhqk', qblk, k, preferred_element_type=F32)
        dist = (start + jnp.arange(Q_BLOCK, dtype=jnp.int32))[:, None] - kpos[None, :]
        a = diff_weights(logits, dist, rel_bias, lam)
        return jnp.einsum('bhqk,bkhe->bqhe', a.astype(v.dtype), v)

    o = lax.map(block, (qb, starts))
    return jnp.moveaxis(o, 0, 1).reshape(B, S, *o.shape[3:])


def diff_attn_sample(q, k_new, v_new, lam, rel_bias, k_past, v_past):
    T = q.shape[1]
    P = k_past.shape[1]
    lp = jnp.einsum('bqhd,bkhd->bhqk', q, k_past, preferred_element_type=F32)
    ln = jnp.einsum('bqhd,bkhd->bhqk', q, k_new, preferred_element_type=F32)
    tpos = jnp.arange(T, dtype=jnp.int32)
    dist = jnp.concatenate([(P + tpos)[:, None] - jnp.arange(P, dtype=jnp.int32)[None, :],
                            tpos[:, None] - tpos[None, :]], axis=-1)
    a = diff_weights(jnp.concatenate([lp, ln], axis=-1), dist, rel_bias, lam).astype(v_new.dtype)
    return (jnp.einsum('bhqk,bkhe->bqhe', a[..., :P], v_past)
            + jnp.einsum('bhqk,bkhe->bqhe', a[..., P:], v_new))


def trunk_layer(x, pe, i, w, diff_fn, gla_s0, conv_buf):
    B, T, _ = x.shape

    def heads(t, n):
        return t.reshape(B, T, n, -1)

    u = jnp.einsum('btd,dn->btn', x, w['w_in'][i])
    gq, gk, gv, gg, glr, cglu, dq, dk, dv = split_in(u)
    lg = jax.nn.log_sigmoid((glr @ w['gla_w_gate_up'][i] + w['gla_b_gate'][i]).astype(F32)) / GLA_TAU
    chunk = GLA_CHUNK if T % GLA_CHUNK == 0 else T
    o_a, s_new = gla_recurrence(heads(gq, GLA_HEADS) * GLA_DK ** -0.5, heads(gk, GLA_HEADS),
                                heads(gv, GLA_HEADS), heads(lg, GLA_HEADS), gla_s0, chunk)
    o_a = (rms_norm(o_a.astype(x.dtype), w['gla_norm_w'][i]) * jax.nn.silu(heads(gg, GLA_HEADS))).reshape(B, T, GLA_VW)
    o_b, buf_new = conformer_conv(cglu, conv_buf, w['conv_w'][i], w['conv_b'][i],
                                  w['conv_ln_g'][i], w['conv_ln_b'][i])
    lam_init = 0.8 - 0.6 * math.exp(-0.3 * i)
    lam = (jnp.exp(jnp.sum(w['diff_lq1'][i].astype(F32) * w['diff_lk1'][i].astype(F32)))
           - jnp.exp(jnp.sum(w['diff_lq2'][i].astype(F32) * w['diff_lk2'][i].astype(F32))) + lam_init)
    kd = heads(dk, 2 * DIFF_HEADS)
    vd = heads(dv, DIFF_HEADS)
    o_c = diff_fn(i, heads(dq, 2 * DIFF_HEADS) * DIFF_DH ** -0.5, kd, vd, lam)
    o_c = (rms_norm(o_c, w['diff_norm_w'][i]) * (1.0 - lam_init)).reshape(B, T, DIFF_VW)
    mix = jnp.concatenate([o_a, o_b, o_c], axis=-1) @ w['w_out'][i]
    x = layer_norm(ALPHA * x + mix, w['ln1_g'][i], w['ln1_b'][i])
    ffn = (jax.nn.silu(x @ w['w_ffn_gate'][i]) * (x @ w['w_ffn_up'][i])) @ w['w_ffn_down'][i]
    ple = jax.nn.sigmoid(x @ w['w_ple_gate'][i]) * (pe @ w['w_ple_proj'][i])
    x = layer_norm(ALPHA * x + ffn + ple, w['ln2_g'][i], w['ln2_b'][i])
    return x, kd, vd, s_new, buf_new


def setup_inputs(seed: int = 0) -> dict:
    key = jax.random.key(seed)
    ks = iter(jax.random.split(key, 40))

    def nrm(shape, scale=1.0):
        return jax.random.normal(next(ks), shape, F32) * scale

    def gain(shape):
        return 1.0 + nrm(shape, 0.02)

    n_pages = PAST_LEN // PAGE_SIZE
    n_used = DEC_BATCH * n_pages
    n_pool = n_used + n_used // 4
    return {
        'x_prompt': nrm((BATCH, SEQ, D_MODEL)),
        'x_sample': nrm((DEC_BATCH, DEC_SEQ, D_MODEL)),
        'cache_k': nrm((DEPTH, n_pool, PAGE_SIZE, 2 * DIFF_HEADS, DIFF_DH)),
        'cache_v': nrm((DEPTH, n_pool, PAGE_SIZE, DIFF_HEADS, 2 * DIFF_DH)),
        'state_gla': nrm((DEPTH, DEC_BATCH, GLA_HEADS, GLA_DK, GLA_DV), 0.5),
        'state_conv': nrm((DEPTH, DEC_BATCH, CONV_WIDTH - 1, CONV_CH), 0.5),
        'page_table': jax.random.permutation(next(ks), n_pool)[:n_used].reshape(DEC_BATCH, n_pages).astype(jnp.int32),
        'p_prompt': nrm((DEPTH, BATCH, SEQ, PLE_DIM)),
        'p_sample': nrm((DEPTH, DEC_BATCH, DEC_SEQ, PLE_DIM)),
        'w_in': nrm((DEPTH, D_MODEL, N_IN), D_MODEL ** -0.5),
        'gla_w_gate_up': nrm((DEPTH, GLA_GATE_RANK, GLA_QK), GLA_GATE_RANK ** -0.5),
        'gla_b_gate': nrm((DEPTH, GLA_QK), 0.1),
        'gla_norm_w': gain((DEPTH, GLA_DV)),
        'conv_w': nrm((DEPTH, CONV_WIDTH, CONV_CH), CONV_WIDTH ** -0.5),
        'conv_b': nrm((DEPTH, CONV_CH), 0.01),
        'conv_ln_g': gain((DEPTH, CONV_CH)),
        'conv_ln_b': nrm((DEPTH, CONV_CH), 0.01),
        'diff_lq1': nrm((DEPTH, DIFF_DH), 0.1),
        'diff_lk1': nrm((DEPTH, DIFF_DH), 0.1),
        'diff_lq2': nrm((DEPTH, DIFF_DH), 0.1),
        'diff_lk2': nrm((DEPTH, DIFF_DH), 0.1),
        'diff_norm_w': gain((DEPTH, 2 * DIFF_DH)),
        'rel_bias': nrm((REL_BUCKETS, 2 * DIFF_HEADS), 0.2),
        'w_out': nrm((DEPTH, MIX_WIDTH, D_MODEL), MIX_WIDTH ** -0.5 * BETA),
        'ln1_g': gain((DEPTH, D_MODEL)),
        'ln1_b': nrm((DEPTH, D_MODEL), 0.01),
        'w_ffn_gate': nrm((DEPTH, D_MODEL, D_FF), D_MODEL ** -0.5),
        'w_ffn_up': nrm((DEPTH, D_MODEL, D_FF), D_MODEL ** -0.5),
        'w_ffn_down': nrm((DEPTH, D_FF, D_MODEL), D_FF ** -0.5 * BETA),
        'w_ple_gate': nrm((DEPTH, D_MODEL, D_MODEL), D_MODEL ** -0.5),
        'w_ple_proj': nrm((DEPTH, PLE_DIM, D_MODEL), PLE_DIM ** -0.5 * BETA),
        'ln2_g': gain((DEPTH, D_MODEL)),
        'ln2_b': nrm((DEPTH, D_MODEL), 0.01),
    }


def reference(x_prompt, x_sample, cache_k, cache_v, state_gla, state_conv, page_table, p_prompt, p_sample,
              w_in, gla_w_gate_up, gla_b_gate, gla_norm_w, conv_w, conv_b, conv_ln_g, conv_ln_b,
              diff_lq1, diff_lk1, diff_lq2, diff_lk2, diff_norm_w, rel_bias, w_out, ln1_g, ln1_b,
              w_ffn_gate, w_ffn_up, w_ffn_down, w_ple_gate, w_ple_proj, ln2_g, ln2_b):
    w = dict(w_in=w_in, gla_w_gate_up=gla_w_gate_up, gla_b_gate=gla_b_gate, gla_norm_w=gla_norm_w,
             conv_w=conv_w, conv_b=conv_b, conv_ln_g=conv_ln_g, conv_ln_b=conv_ln_b,
             diff_lq1=diff_lq1, diff_lk1=diff_lk1, diff_lq2=diff_lq2, diff_lk2=diff_lk2,
             diff_norm_w=diff_norm_w, w_out=w_out, ln1_g=ln1_g, ln1_b=ln1_b,
             w_ffn_gate=w_ffn_gate, w_ffn_up=w_ffn_up, w_ffn_down=w_ffn_down,
             w_ple_gate=w_ple_gate, w_ple_proj=w_ple_proj, ln2_g=ln2_g, ln2_b=ln2_b)
    n_prompt = x_prompt.shape[0]
    n_sample = x_sample.shape[0]

    def diff_prompt(i, q, k, v, lam):
        return diff_attn_prompt(q, k, v, lam, rel_bias)

    def diff_sample(i, q, k, v, lam):
        k_past = cache_k[i, page_table].reshape(n_sample, -1, *cache_k.shape[3:])
        v_past = cache_v[i, page_table].reshape(n_sample, -1, *cache_v.shape[3:])
        return diff_attn_sample(q, k, v, lam, rel_bias, k_past, v_past)

    yp, ys = x_prompt, x_sample
    kp, vp, sp, cp, kq, vq, sq, cq = [], [], [], [], [], [], [], []
    for i in range(DEPTH):
        yp, k_i, v_i, s_i, c_i = trunk_layer(
            yp, p_prompt[i], i, w, diff_prompt,
            jnp.zeros((n_prompt, GLA_HEADS, GLA_DK, GLA_DV), F32),
            jnp.zeros((n_prompt, CONV_WIDTH - 1, CONV_CH), x_prompt.dtype))
        kp.append(k_i); vp.append(v_i); sp.append(s_i); cp.append(c_i)
        ys, k_i, v_i, s_i, c_i = trunk_layer(
            ys, p_sample[i], i, w, diff_sample, state_gla[i], state_conv[i])
        kq.append(k_i); vq.append(v_i); sq.append(s_i); cq.append(c_i)
    return (yp, ys, jnp.stack(kp), jnp.stack(vp), jnp.stack(sp), jnp.stack(cp),
            jnp.stack(kq), jnp.stack(vq), jnp.stack(sq), jnp.stack(cq))
```

```python
import functools
import math

import jax
import jax.numpy as jnp
from jax import lax
from jax.experimental import pallas as pl
from jax.experimental.pallas import tpu as pltpu

F32 = jnp.float32
BF16 = jnp.bfloat16

GLA_HEADS = 4
GLA_DK = 32
GLA_DV = 64
GLA_QK = GLA_HEADS * GLA_DK
GLA_VW = GLA_HEADS * GLA_DV
GLA_GATE_RANK = 16
GLA_TAU = 16.0
CONV_CH = 256
CONV_WIDTH = 31
DIFF_HEADS = 4
DIFF_DH = 64
DIFF_VW = 2 * DIFF_HEADS * DIFF_DH
REL_BUCKETS = 32
REL_MAX_DIST = 128
PAGE_SIZE = 128
DEPTH = 2
ALPHA = (2 * DEPTH) ** 0.25
EPS = 1e-5

LANES = 128
GLA_BLOCK = 16
ATTN_TILE = 256
CONV_HALO = 32
VMEM_LIMIT = 56 * 1024 * 1024

_IN_OFF = {}
_o = 0
for _name, _n in (("gq", 128), ("gk", 128), ("gv", 256), ("gg", 256), ("glr", 128), ("ca", 256), ("cg", 256),
                  ("dq", 512), ("dk", 512), ("dv", 512)):
    _IN_OFF[_name] = (_o, _n)
    _o += _n
N_IN_PAD = _o


def _params(*sem):
    return pltpu.CompilerParams(dimension_semantics=sem, vmem_limit_bytes=VMEM_LIMIT)


def _const_spec(shape):
    nd = len(shape)
    return pl.BlockSpec(shape, lambda *_: (0,) * nd, pipeline_mode=pl.Buffered(1))


def _nt(a, b):
    return lax.dot_general(a, b, (((1,), (1,)), ((), ())), preferred_element_type=F32)


def _mm(a, b):
    return jnp.dot(a, b, preferred_element_type=F32)


def _mm_split(a, b):
    hi = a.astype(BF16)
    lo = (a - hi.astype(F32)).astype(BF16)
    return _mm(hi, b) + _mm(lo, b)


def _sigmoid(x):
    return 1.0 / (1.0 + jnp.exp(-x))


def _layer_norm(x, g, b):
    mu = jnp.mean(x, axis=-1, keepdims=True)
    xc = x - mu
    var = jnp.mean(xc * xc, axis=-1, keepdims=True)
    return xc * lax.rsqrt(var + EPS) * g + b


def _inproj_body(x_ref, w_ref, wg_ref, bg_ref, gq_ref, gk_ref, gv_ref, gs_ref, lg_ref, u_ref, dq_ref, dk_ref,
                 dv_ref):
    x = x_ref[...].astype(BF16)

    def proj(name):
        lo, n = _IN_OFF[name]
        return _mm(x, w_ref[:, lo:lo + n])

    gq_ref[...] = proj("gq") * GLA_DK ** -0.5
    gk_ref[...] = proj("gk")
    gv_ref[...] = proj("gv")
    gg = proj("gg")
    gs_ref[...] = gg * _sigmoid(gg)
    z = _mm(proj("glr").astype(BF16), wg_ref[...]) + bg_ref[...]
    lg_ref[...] = (jnp.minimum(z, 0.0) - jnp.log(1.0 + jnp.exp(-jnp.abs(z)))) * (1.0 / GLA_TAU)
    u_ref[...] = proj("ca") * _sigmoid(proj("cg"))
    dq_ref[...] = proj("dq") * DIFF_DH ** -0.5
    dk_ref[...] = proj("dk")
    dv_ref[...] = proj("dv")


def _in_proj(x2, w_in_p, wg_p, bg):
    m, d = x2.shape
    tm = min(512, m)
    widths = (GLA_QK, GLA_QK, GLA_VW, GLA_VW, GLA_QK, CONV_CH, DIFF_VW, DIFF_VW, DIFF_VW)
    return pl.pallas_call(
        _inproj_body,
        grid=(m // tm,),
        in_specs=[pl.BlockSpec((tm, d), lambda i: (i, 0)),
                  _const_spec(w_in_p.shape), _const_spec(wg_p.shape), _const_spec(bg.shape)],
        out_specs=[pl.BlockSpec((tm, n), lambda i: (i, 0)) for n in widths],
        out_shape=[jax.ShapeDtypeStruct((m, n), F32) for n in widths],
        compiler_params=_params("parallel"),
        name="in_proj",
    )(x2, w_in_p, wg_p, bg)


def _gla_body(q_ref, k_ref, v_ref, lg_ref, gs_ref, nw_ref, s0_ref, o_ref, sout_ref, s_scr, *, cb):
    j = pl.program_id(1)

    @pl.when(j == 0)
    def _():
        s_scr[...] = s0_ref[0]

    q = q_ref[...]
    k = k_ref[...]
    v = v_ref[...]
    r = q.shape[0]
    row = lax.broadcasted_iota(jnp.int32, (r, 1), 0) % cb

    b = lg_ref[...]
    s = 1
    while s < cb:
        b = b + jnp.where(row >= s, pltpu.roll(b, s, 0), 0.0)
        s *= 2

    same_head = (lax.broadcasted_iota(jnp.int32, (GLA_QK, GLA_VW), 0) // GLA_DK
                 == lax.broadcasted_iota(jnp.int32, (GLA_QK, GLA_VW), 1) // GLA_DV)
    expand = same_head.astype(BF16)
    mask_t = (lax.broadcasted_iota(jnp.int32, (GLA_VW, GLA_QK), 0) // GLA_DV
              == lax.broadcasted_iota(jnp.int32, (GLA_VW, GLA_QK), 1) // GLA_DK).astype(F32)

    o = _mm((q * k).astype(BF16), expand) * v
    for delta in range(1, cb):
        ks = pltpu.roll(k, delta, 0)
        bs = pltpu.roll(b, delta, 0)
        vs = pltpu.roll(v, delta, 0)
        p = jnp.where(row >= delta, q * ks * jnp.exp(b - bs), 0.0)
        o = o + _mm(p.astype(BF16), expand) * vs

    st = s_scr[...]
    inter = []
    for t in range(r // cb):
        sl = slice(t * cb, (t + 1) * cb)
        bt = b[sl]
        bend = bt[cb - 1:cb]
        qe = (q[sl] * jnp.exp(bt)).astype(BF16)
        inter.append(_nt(qe, st.astype(BF16)))
        ke = k[sl] * jnp.exp(bend - bt)
        vt = v[sl]
        if cb < 16:
            ke = jnp.concatenate([ke, jnp.zeros((16 - cb, GLA_QK), F32)], axis=0)
            vt = jnp.concatenate([vt, jnp.zeros((16 - cb, GLA_VW), F32)], axis=0)
        kv = lax.dot_general(vt.astype(BF16), ke.astype(BF16), (((0,), (0,)), ((), ())),
                             preferred_element_type=F32)
        st = st * jnp.exp(bend) + kv * mask_t
    s_scr[...] = st
    o = o + (inter[0] if len(inter) == 1 else jnp.concatenate(inter, axis=0))

    grp = (lax.broadcasted_iota(jnp.int32, (GLA_VW, GLA_VW), 0) // GLA_DV
           == lax.broadcasted_iota(jnp.int32, (GLA_VW, GLA_VW), 1) // GLA_DV)
    mean_sq = _mm_split(o * o, jnp.where(grp, 1.0 / GLA_DV, 0.0).astype(BF16))
    o_ref[...] = o * lax.rsqrt(mean_sq + EPS) * nw_ref[...] * gs_ref[...]

    @pl.when(j == pl.num_programs(1) - 1)
    def _():
        sout_ref[0] = st


def _gla(gq, gk, gv, lg, gs, nw, s0_t, nb, t):
    cb = GLA_BLOCK if t % GLA_BLOCK == 0 else t
    r = min(t, 256)
    nj = t // r
    m = nb * t

    def rows(n):
        return pl.BlockSpec((r, n), lambda b, j: (b * nj + j, 0))

    return pl.pallas_call(
        functools.partial(_gla_body, cb=cb),
        grid=(nb, nj),
        in_specs=[rows(GLA_QK), rows(GLA_QK), rows(GLA_VW), rows(GLA_QK), rows(GLA_VW),
                  _const_spec(nw.shape),
                  pl.BlockSpec((1, GLA_VW, GLA_QK), lambda b, j: (b, 0, 0))],
        out_specs=[rows(GLA_VW), pl.BlockSpec((1, GLA_VW, GLA_QK), lambda b, j: (b, 0, 0))],
        out_shape=[jax.ShapeDtypeStruct((m, GLA_VW), F32), jax.ShapeDtypeStruct((nb, GLA_VW, GLA_QK), F32)],
        scratch_shapes=[pltpu.VMEM((GLA_VW, GLA_QK), F32)],
        compiler_params=_params("parallel", "arbitrary"),
        name="gla",
    )(gq, gk, gv, lg, gs, nw, s0_t)


def _conv_body(u_ref, buf_ref, cw_ref, cb_ref, g_ref, beta_ref, y_ref, nbuf_ref, seq_scr):
    j = pl.program_id(1)
    r = u_ref.shape[0]
    hist = CONV_WIDTH - 1

    @pl.when(j == 0)
    def _():
        seq_scr[0:CONV_HALO - hist, :] = jnp.zeros((CONV_HALO - hist, CONV_CH), F32)
        seq_scr[CONV_HALO - hist:CONV_HALO, :] = buf_ref[0]

    @pl.when(j > 0)
    def _():
        seq_scr[0:CONV_HALO, :] = seq_scr[r:r + CONV_HALO, :]

    seq_scr[CONV_HALO:CONV_HALO + r, :] = u_ref[...]

    acc = jnp.zeros((r, CONV_CH), F32) + cb_ref[...]
    for w in range(CONV_WIDTH):
        lo = CONV_HALO - hist + w
        acc = acc + seq_scr[lo:lo + r, :] * cw_ref[w:w + 1, :]
    y = _layer_norm(acc, g_ref[...], beta_ref[...])
    y_ref[...] = y * _sigmoid(y)

    @pl.when(j == pl.num_programs(1) - 1)
    def _():
        nbuf_ref[0] = seq_scr[r + CONV_HALO - hist:r + CONV_HALO, :]


def _conv(u, buf, cw, cb, g, beta, nb, t):
    r = min(t, 512)
    nj = t // r
    assert nj == 1 or r >= CONV_HALO
    hist = CONV_WIDTH - 1
    return pl.pallas_call(
        _conv_body,
        grid=(nb, nj),
        in_specs=[pl.BlockSpec((r, CONV_CH), lambda b, j: (b * nj + j, 0)),
                  pl.BlockSpec((1, hist, CONV_CH), lambda b, j: (b, 0, 0)),
                  _const_spec(cw.shape), _const_spec(cb.shape), _const_spec(g.shape), _const_spec(beta.shape)],
        out_specs=[pl.BlockSpec((r, CONV_CH), lambda b, j: (b * nj + j, 0)),
                   pl.BlockSpec((1, hist, CONV_CH), lambda b, j: (b, 0, 0))],
        out_shape=[jax.ShapeDtypeStruct((nb * t, CONV_CH), F32), jax.ShapeDtypeStruct((nb, hist, CONV_CH), F32)],
        scratch_shapes=[pltpu.VMEM((r + CONV_HALO, CONV_CH), F32)],
        compiler_params=_params("parallel", "arbitrary"),
        name="conv",
    )(u, buf, cw, cb, g, beta)


def _rel_bucket(dist):
    n = jnp.maximum(dist, 0)
    max_exact = REL_BUCKETS // 2
    nf = jnp.maximum(n, 1).astype(F32)
    large = max_exact + (jnp.log(nf / max_exact) / math.log(REL_MAX_DIST / max_exact)
                         * (REL_BUCKETS - max_exact)).astype(jnp.int32)
    large = jnp.minimum(large, REL_BUCKETS - 1)
    return jnp.where(n < max_exact, n, large)


def _bias_body(rb_ref, idx_ref, o_ref):
    h = pl.program_id(1)
    idx = idx_ref[0]
    acc = jnp.full(idx.shape, -jnp.inf, F32)
    for bucket in range(REL_BUCKETS):
        acc = jnp.where(idx == bucket, rb_ref[bucket, h], acc)
    o_ref[0, 0] = acc


def _bias_tiles(rel_bias, t):
    assert t >= REL_MAX_DIST
    ii = jnp.arange(t, dtype=jnp.int32)[:, None]
    jj = jnp.arange(t, dtype=jnp.int32)[None, :]
    idx = jnp.stack([jnp.where(ii >= jj, _rel_bucket(ii - jj), -1), _rel_bucket(t + ii - jj)])
    nh = rel_bias.shape[1]
    return pl.pallas_call(
        _bias_body,
        grid=(2, nh),
        in_specs=[pl.BlockSpec(memory_space=pltpu.SMEM),
                  pl.BlockSpec((1, t, t), lambda r, h: (r, 0, 0))],
        out_specs=pl.BlockSpec((1, 1, t, t), lambda r, h: (r, h, 0, 0)),
        out_shape=jax.ShapeDtypeStruct((2, nh, t, t), F32),
        compiler_params=_params("arbitrary", "arbitrary"),
        name="rel_bias_tiles",
    )(rel_bias, idx)


def _attn_body(lam_ref, q_ref, k_ref, v_ref, bias_ref, dnw_ref, o_ref, m_scr, l_scr, acc_scr, *, out_scale):
    h = pl.program_id(1)
    i = pl.program_id(2)
    t = q_ref.shape[0]
    lane = lax.broadcasted_iota(jnp.int32, (t, LANES), 1)
    q = q_ref[...]
    qs = (jnp.where(lane < DIFF_DH, q, 0.0).astype(BF16), jnp.where(lane >= DIFF_DH, q, 0.0).astype(BF16))
    m_scr[...] = jnp.full(m_scr.shape, -jnp.inf, F32)
    l_scr[...] = jnp.zeros(l_scr.shape, F32)
    acc_scr[...] = jnp.zeros(acc_scr.shape, F32)

    def update(kstart, bias_of):
        kb = k_ref[pl.ds(kstart, t), :].astype(BF16)
        vb = v_ref[pl.ds(kstart, t), :].astype(BF16)
        for a in (0, 1):
            s = _nt(qs[a], kb) + bias_of(a)
            m_old = m_scr[a]
            m_new = jnp.maximum(m_old, jnp.max(s, axis=-1, keepdims=True))
            alpha = jnp.exp(m_old - m_new)
            p = jnp.exp(s - m_new)
            l_scr[a] = alpha * l_scr[a] + jnp.sum(p, axis=-1, keepdims=True)
            acc_scr[a] = alpha * acc_scr[a] + _mm(p.astype(BF16), vb)
            m_scr[a] = m_new

    def far_bias(a):
        return bias_ref[1, 2 * h + a, t - 1:t, 0:1]

    def far_step(jb, carry):
        update(pl.multiple_of(jb * t, t), far_bias)
        return carry

    lax.fori_loop(0, jnp.maximum(i - 1, 0), far_step, 0)

    @pl.when(i >= 1)
    def _():
        update(pl.multiple_of((i - 1) * t, t), lambda a: bias_ref[1, 2 * h + a])

    update(pl.multiple_of(i * t, t), lambda a: bias_ref[0, 2 * h + a])

    o = acc_scr[0] / l_scr[0] - lam_ref[0] * (acc_scr[1] / l_scr[1])
    ms = jnp.mean(o * o, axis=-1, keepdims=True)
    o_ref[...] = o * lax.rsqrt(ms + EPS) * dnw_ref[...] * out_scale


def _attn_prompt(dq, dk, dv, bias, lam, dnw, nb, s, out_scale):
    t = bias.shape[-1]
    nq = s // t
    return pl.pallas_call(
        functools.partial(_attn_body, out_scale=out_scale),
        grid=(nb, DIFF_HEADS, nq),
        in_specs=[pl.BlockSpec(memory_space=pltpu.SMEM),
                  pl.BlockSpec((t, LANES), lambda b, h, i: (b * nq + i, h)),
                  pl.BlockSpec((s, LANES), lambda b, h, i: (b, h)),
                  pl.BlockSpec((s, LANES), lambda b, h, i: (b, h)),
                  _const_spec(bias.shape), _const_spec(dnw.shape)],
        out_specs=pl.BlockSpec((t, LANES), lambda b, h, i: (b * nq + i, h)),
        out_shape=jax.ShapeDtypeStruct((nb * s, DIFF_VW), F32),
        scratch_shapes=[pltpu.VMEM((2, t, 1), F32), pltpu.VMEM((2, t, 1), F32), pltpu.VMEM((2, t, LANES), F32)],
        compiler_params=_params("parallel", "parallel", "arbitrary"),
        name="diff_attn_prompt",
    )(lam, dq, dk, dv, bias, dnw)


def _sattn_body(pt_ref, lam_ref, q_ref, kn_ref, vn_ref, cf_ref, d1_ref, d0_ref, dnw_ref, *rest, npg, out_scale):
    del pt_ref
    kp = rest[:npg]
    vp = rest[npg:2 * npg]
    o_ref = rest[2 * npg]
    m_scr, l_scr, acc_scr = rest[2 * npg + 1:]
    p_id = pl.program_id(1)
    last = pl.num_programs(1) - 1
    nh = 2 * DIFF_HEADS
    tq = q_ref.shape[0]

    q = q_ref[...]
    lane_head = lax.broadcasted_iota(jnp.int32, q.shape, 1) // DIFF_DH
    qs = jnp.concatenate([jnp.where(lane_head == hh, q, 0.0) for hh in range(nh)], axis=0).astype(BF16)

    @pl.when(p_id == 0)
    def _():
        m_scr[...] = jnp.full(m_scr.shape, -jnp.inf, F32)
        l_scr[...] = jnp.zeros(l_scr.shape, F32)
        acc_scr[...] = jnp.zeros(acc_scr.shape, F32)

    def update(s, vals):
        m_old = m_scr[...]
        m_new = jnp.maximum(m_old, jnp.max(s, axis=-1, keepdims=True))
        alpha = jnp.exp(m_old - m_new)
        p = jnp.exp(s - m_new)
        l_scr[...] = alpha * l_scr[...] + jnp.sum(p, axis=-1, keepdims=True)
        pv = _mm(p[:, 0:PAGE_SIZE].astype(BF16), vals[0])
        for g in range(1, len(vals)):
            pv = pv + _mm(p[:, g * PAGE_SIZE:(g + 1) * PAGE_SIZE].astype(BF16), vals[g])
        acc_scr[...] = alpha * acc_scr[...] + pv
        m_scr[...] = m_new

    cf = cf_ref[...]
    ss = []
    for g in range(npg):
        bias = jnp.where(p_id == last, d1_ref[...], cf) if g == npg - 1 else cf
        ss.append(_nt(qs, kp[g][...].astype(BF16)) + bias)
    update(jnp.concatenate(ss, axis=1), [vp[g][...].astype(BF16) for g in range(npg)])

    @pl.when(p_id == last)
    def _():
        pad = jnp.zeros((PAGE_SIZE - tq, DIFF_VW), F32)
        kn = jnp.concatenate([kn_ref[...], pad], axis=0).astype(BF16)
        vn = jnp.concatenate([vn_ref[...], pad], axis=0).astype(BF16)
        update(_nt(qs, kn) + d0_ref[...], [vn])
        o = acc_scr[...] / l_scr[...]
        lam = lam_ref[0]
        outs = []
        for h in range(DIFF_HEADS):
            cols = slice(h * LANES, (h + 1) * LANES)
            oh = o[2 * h * tq:(2 * h + 1) * tq, cols] - lam * o[(2 * h + 1) * tq:(2 * h + 2) * tq, cols]
            ms = jnp.mean(oh * oh, axis=-1, keepdims=True)
            outs.append(oh * lax.rsqrt(ms + EPS) * dnw_ref[...] * out_scale)
        o_ref[...] = jnp.concatenate(outs, axis=1)


def _attn_sample(dq, dk, dv, cache_k4, cache_v4, page_table, layer, bias, lam, dnw, nb, tq, out_scale):
    n_pages = page_table.shape[1]
    npg = 8
    assert n_pages % npg == 0 and cache_k4.shape[2] == PAGE_SIZE and tq == 8
    nh = 2 * DIFF_HEADS
    t = bias.shape[-1]
    cf = jnp.broadcast_to(bias[1, :, t - 1, 0][:, None, None], (nh, tq, PAGE_SIZE)).reshape(nh * tq, PAGE_SIZE)
    d1 = bias[1, :, 0:tq, t - PAGE_SIZE:t].reshape(nh * tq, PAGE_SIZE)
    d0 = jnp.concatenate([bias[0, :, 0:tq, 0:tq], jnp.full((nh, tq, PAGE_SIZE - tq), -jnp.inf, F32)],
                         axis=-1).reshape(nh * tq, PAGE_SIZE)

    def page_spec(g):
        return pl.BlockSpec((None, None, PAGE_SIZE, DIFF_VW),
                            lambda b, p, pt: (layer, pt[b * n_pages + p * npg + g], 0, 0))

    def rows_spec():
        return pl.BlockSpec((tq, DIFF_VW), lambda b, p, pt: (b, 0))

    def full_spec(a):
        nd = a.ndim
        return pl.BlockSpec(a.shape, lambda b, p, pt: (0,) * nd)

    grid_spec = pltpu.PrefetchScalarGridSpec(
        num_scalar_prefetch=1,
        grid=(nb, n_pages // npg),
        in_specs=[pl.BlockSpec(memory_space=pltpu.SMEM), rows_spec(), rows_spec(), rows_spec(),
                  full_spec(cf), full_spec(d1), full_spec(d0), full_spec(dnw)]
                 + [page_spec(g) for g in range(npg)] + [page_spec(g) for g in range(npg)],
        out_specs=rows_spec(),
        scratch_shapes=[pltpu.VMEM((nh * tq, 1), F32), pltpu.VMEM((nh * tq, 1), F32),
                        pltpu.VMEM((nh * tq, DIFF_VW), F32)],
    )
    return pl.pallas_call(
        functools.partial(_sattn_body, npg=npg, out_scale=out_scale),
        grid_spec=grid_spec,
        out_shape=jax.ShapeDtypeStruct((nb * tq, DIFF_VW), F32),
        compiler_params=_params("parallel", "arbitrary"),
        name="diff_attn_sample",
    )(page_table.reshape(-1), lam, dq, dk, dv, cf, d1, d0, dnw, *([cache_k4] * npg), *([cache_v4] * npg))


def _outproj_body(oa_ref, ob_ref, oc_ref, x_ref, w_ref, g_ref, b_ref, y_ref):
    mix = _mm(oa_ref[...].astype(BF16), w_ref[0:GLA_VW, :])
    mix = mix + _mm(ob_ref[...].astype(BF16), w_ref[GLA_VW:GLA_VW + CONV_CH, :])
    mix = mix + _mm(oc_ref[...].astype(BF16), w_ref[GLA_VW + CONV_CH:, :])
    y_ref[...] = _layer_norm(ALPHA * x_ref[...] + mix, g_ref[...], b_ref[...])


def _out_proj(oa, ob, oc, x2, w, g, b):
    m, d = x2.shape
    tm = min(512, m)

    def rows(n):
        return pl.BlockSpec((tm, n), lambda i: (i, 0))

    return pl.pallas_call(
        _outproj_body,
        grid=(m // tm,),
        in_specs=[rows(GLA_VW), rows(CONV_CH), rows(DIFF_VW), rows(d),
                  _const_spec(w.shape), _const_spec(g.shape), _const_spec(b.shape)],
        out_specs=rows(d),
        out_shape=jax.ShapeDtypeStruct((m, d), F32),
        compiler_params=_params("parallel"),
        name="out_proj_ln",
    )(oa, ob, oc, x2, w, g, b)


def _ffn_body(x_ref, pe_ref, wg_ref, wu_ref, wd_ref, wpg_ref, wpp_ref, g_ref, b_ref, y_ref, *, chunk):
    x = x_ref[...]
    xb = x.astype(BF16)
    ple = _sigmoid(_mm(xb, wpg_ref[...])) * _mm(pe_ref[...].astype(BF16), wpp_ref[...])
    acc = ALPHA * x + ple
    for c in range(wg_ref.shape[1] // chunk):
        cols = slice(c * chunk, (c + 1) * chunk)
        gate = _mm(xb, wg_ref[:, cols])
        hid = gate * _sigmoid(gate) * _mm(xb, wu_ref[:, cols])
        acc = acc + _mm(hid.astype(BF16), wd_ref[cols, :])
    y_ref[...] = _layer_norm(acc, g_ref[...], b_ref[...])


def _ffn(x2, pe2, wg, wu, wd, wpg, wpp, g, b):
    m, d = x2.shape
    tm = min(512, m)

    def rows(n):
        return pl.BlockSpec((tm, n), lambda i: (i, 0))

    return pl.pallas_call(
        functools.partial(_ffn_body, chunk=256),
        grid=(m // tm,),
        in_specs=[rows(d), rows(pe2.shape[1])] + [_const_spec(a.shape) for a in (wg, wu, wd, wpg, wpp, g, b)],
        out_specs=rows(d),
        out_shape=jax.ShapeDtypeStruct((m, d), F32),
        compiler_params=_params("parallel"),
        name="ffn_ple_ln",
    )(x2, pe2, wg, wu, wd, wpg, wpp, g, b)


def _row(a):
    return a.reshape(1, -1)


def _state_to_t(s):
    nb = s.shape[0]
    eye = jnp.eye(GLA_HEADS, dtype=s.dtype)
    return jnp.einsum("bhde,hg->bhegd", s, eye).reshape(nb, GLA_VW, GLA_QK)


def _state_from_t(s_t):
    nb = s_t.shape[0]
    blocks = s_t.reshape(nb, GLA_HEADS, GLA_DV, GLA_HEADS, GLA_DK)
    diag = jnp.stack([blocks[:, h, :, h, :] for h in range(GLA_HEADS)], axis=1)
    return diag.transpose(0, 1, 3, 2)


def _layer(x2, pe2, nb, t, lw, layer, s0, buf, attend):
    gq, gk, gv, gs, lg, u, dq, dk, dv = _in_proj(x2, lw["w_in"], lw["wg"], lw["bg"])
    o_a, s_t = _gla(gq, gk, gv, lg, gs, lw["gla_nw"], _state_to_t(s0), nb, t)
    o_b, nbuf = _conv(u, buf, lw["conv_w"], lw["conv_b"], lw["conv_g"], lw["conv_beta"], nb, t)
    lam_init = 0.8 - 0.6 * math.exp(-0.3 * layer)
    o_c = attend(dq, dk, dv, lw["lam"], lw["dnw"], 1.0 - lam_init)
    x1 = _out_proj(o_a, o_b, o_c, x2, lw["w_out"], lw["ln1_g"], lw["ln1_b"])
    y = _ffn(x1, pe2, lw["w_ffn_gate"], lw["w_ffn_up"], lw["w_ffn_down"], lw["w_ple_gate"], lw["w_ple_proj"],
             lw["ln2_g"], lw["ln2_b"])
    return y, dk, dv, _state_from_t(s_t), nbuf


def _prep_w_in(w):
    sizes = (GLA_QK, GLA_QK, GLA_VW, GLA_VW, GLA_GATE_RANK, CONV_CH, CONV_CH, DIFF_VW, DIFF_VW, DIFF_VW)
    parts, s = [], 0
    for n in sizes:
        parts.append(w[:, s:s + n])
        s += n
    parts[4] = jnp.pad(parts[4], ((0, 0), (0, LANES - GLA_GATE_RANK)))
    return jnp.concatenate(parts, axis=1).astype(BF16)


def kernel(x_prompt, x_sample, cache_k, cache_v, state_gla, state_conv, page_table, p_prompt, p_sample, w_in, gla_w_gate_up, gla_b_gate, gla_norm_w, conv_w, conv_b, conv_ln_g, conv_ln_b, diff_lq1, diff_lk1, diff_lq2, diff_lk2, diff_norm_w, rel_bias, w_out, ln1_g, ln1_b, w_ffn_gate, w_ffn_up, w_ffn_down, w_ple_gate, w_ple_proj, ln2_g, ln2_b):
    nbp, s, d = x_prompt.shape
    nbs, ts, _ = x_sample.shape
    depth = w_in.shape[0]
    assert depth == DEPTH and cache_k.shape[2] == PAGE_SIZE
    cache_k4 = cache_k.reshape(*cache_k.shape[:3], DIFF_VW)
    cache_v4 = cache_v.reshape(*cache_v.shape[:3], DIFF_VW)
    bias = _bias_tiles(rel_bias, ATTN_TILE)

    yp = x_prompt.reshape(nbp * s, d)
    ys = x_sample.reshape(nbs * ts, d)
    zero_state = jnp.zeros((nbp, GLA_HEADS, GLA_DK, GLA_DV), F32)
    zero_buf = jnp.zeros((nbp, CONV_WIDTH - 1, CONV_CH), F32)
    outs = [[] for _ in range(8)]
    for i in range(depth):
        lam_init = 0.8 - 0.6 * math.exp(-0.3 * i)
        lam = (jnp.exp(jnp.sum(diff_lq1[i] * diff_lk1[i])) - jnp.exp(jnp.sum(diff_lq2[i] * diff_lk2[i]))
               + lam_init).reshape(1).astype(F32)
        lw = dict(
            w_in=_prep_w_in(w_in[i]),
            wg=jnp.pad(gla_w_gate_up[i], ((0, LANES - GLA_GATE_RANK), (0, 0))).astype(BF16),
            bg=_row(gla_b_gate[i]),
            gla_nw=_row(jnp.tile(gla_norm_w[i], GLA_HEADS)),
            conv_w=conv_w[i], conv_b=_row(conv_b[i]), conv_g=_row(conv_ln_g[i]), conv_beta=_row(conv_ln_b[i]),
            lam=lam, dnw=_row(diff_norm_w[i]),
            w_out=w_out[i].astype(BF16), ln1_g=_row(ln1_g[i]), ln1_b=_row(ln1_b[i]),
            w_ffn_gate=w_ffn_gate[i].astype(BF16), w_ffn_up=w_ffn_up[i].astype(BF16),
            w_ffn_down=w_ffn_down[i].astype(BF16), w_ple_gate=w_ple_gate[i].astype(BF16),
            w_ple_proj=w_ple_proj[i].astype(BF16), ln2_g=_row(ln2_g[i]), ln2_b=_row(ln2_b[i]),
        )

        def attend_prompt(dq, dk, dv, lam_, dnw, scale):
            return _attn_prompt(dq, dk, dv, bias, lam_, dnw, nbp, s, scale)

        def attend_sample(dq, dk, dv, lam_, dnw, scale, layer=i):
            return _attn_sample(dq, dk, dv, cache_k4, cache_v4, page_table, layer, bias, lam_, dnw, nbs, ts, scale)

        yp, k_i, v_i, s_i, c_i = _layer(yp, p_prompt[i].reshape(nbp * s, -1), nbp, s, lw, i, zero_state, zero_buf,
                                        attend_prompt)
        outs[0].append(k_i.reshape(nbp, s, 2 * DIFF_HEADS, DIFF_DH))
        outs[1].append(v_i.reshape(nbp, s, DIFF_HEADS, 2 * DIFF_DH))
        outs[2].append(s_i)
        outs[3].append(c_i)
        ys, k_i, v_i, s_i, c_i = _layer(ys, p_sample[i].reshape(nbs * ts, -1), nbs, ts, lw, i, state_gla[i],
                                        state_conv[i], attend_sample)
        outs[4].append(k_i.reshape(nbs, ts, 2 * DIFF_HEADS, DIFF_DH))
        outs[5].append(v_i.reshape(nbs, ts, DIFF_HEADS, 2 * DIFF_DH))
        outs[6].append(s_i)
        outs[7].append(c_i)
    return (yp.reshape(nbp, s, d), ys.reshape(nbs, ts, d)) + tuple(jnp.stack(o) for o in outs)
```

```python
import functools
import math

import jax
import jax.numpy as jnp
from jax import lax
from jax.experimental import pallas as pl
from jax.experimental.pallas import tpu as pltpu

F32 = jnp.float32
BF16 = jnp.bfloat16

GLA_HEADS = 4
GLA_DK = 32
GLA_DV = 64
GLA_QK = GLA_HEADS * GLA_DK
GLA_VW = GLA_HEADS * GLA_DV
GLA_GATE_RANK = 16
GLA_TAU = 16.0
CONV_CH = 256
CONV_WIDTH = 31
DIFF_HEADS = 4
DIFF_DH = 64
DIFF_VW = 2 * DIFF_HEADS * DIFF_DH
REL_BUCKETS = 32
REL_MAX_DIST = 128
PAGE_SIZE = 128
DEPTH = 2
ALPHA = (2 * DEPTH) ** 0.25
EPS = 1e-5

LANES = 128
GLA_BLOCK = 16
ATTN_TILE = 256
PAGES_PER_STEP = 8
CONV_HALO = 32
VMEM_LIMIT = 56 * 1024 * 1024

_IN_OFF = {}
_o = 0
for _name, _n in (("gq", 128), ("gk", 128), ("gv", 256), ("gg", 256), ("glr", 128), ("ca", 256), ("cg", 256),
                  ("dq", 512), ("dk", 512), ("dv", 512)):
    _IN_OFF[_name] = (_o, _n)
    _o += _n
N_IN_PAD = _o


def _params(*sem):
    return pltpu.CompilerParams(dimension_semantics=sem, vmem_limit_bytes=VMEM_LIMIT)


def _const_spec(shape):
    nd = len(shape)
    return pl.BlockSpec(shape, lambda *_: (0,) * nd, pipeline_mode=pl.Buffered(1))


def _nt(a, b):
    return lax.dot_general(a, b, (((1,), (1,)), ((), ())), preferred_element_type=F32)


def _mm(a, b):
    return jnp.dot(a, b, preferred_element_type=F32)


def _mm_split(a, b):
    hi = a.astype(BF16)
    lo = (a - hi.astype(F32)).astype(BF16)
    return _mm(hi, b) + _mm(lo, b)


def _sigmoid(x):
    return 1.0 / (1.0 + jnp.exp(-x))


def _layer_norm(x, g, b):
    mu = jnp.mean(x, axis=-1, keepdims=True)
    xc = x - mu
    var = jnp.mean(xc * xc, axis=-1, keepdims=True)
    return xc * lax.rsqrt(var + EPS) * g + b


def _fold_lanes(x, op):
    acc = x[:, 0:LANES]
    for c in range(1, x.shape[1] // LANES):
        acc = op(acc, x[:, c * LANES:(c + 1) * LANES])
    return acc


def _inproj_body(x_ref, w_ref, wkt_ref, wg_ref, bg_ref, gq_ref, gk_ref, gv_ref, gs_ref, lg_ref, u_ref, dq_ref,
                 dv_ref, *k_refs, transposed_k):
    x = x_ref[...].astype(BF16)

    def proj(name):
        lo, n = _IN_OFF[name]
        return _mm(x, w_ref[:, lo:lo + n])

    gq_ref[...] = proj("gq") * GLA_DK ** -0.5
    gk_ref[...] = proj("gk")
    gv_ref[...] = proj("gv")
    gg = proj("gg")
    gs_ref[...] = gg * _sigmoid(gg)
    z = _mm(proj("glr").astype(BF16), wg_ref[...]) + bg_ref[...]
    lg_ref[...] = (jnp.minimum(z, 0.0) - jnp.log(1.0 + jnp.exp(-jnp.abs(z)))) * (1.0 / GLA_TAU)
    u_ref[...] = proj("ca") * _sigmoid(proj("cg"))
    dq_ref[...] = (proj("dq") * DIFF_DH ** -0.5).astype(dq_ref.dtype)
    dv = proj("dv")
    dv_ref[...] = dv
    if transposed_k:
        kt_ref, kt16_ref, v16_ref = k_refs
        kt = _nt(wkt_ref[...], x)
        kt_ref[0] = kt
        kt16_ref[0] = kt.astype(BF16)
        v16_ref[...] = dv.astype(BF16)
    else:
        k_refs[0][...] = proj("dk")


def _in_proj(x2, w_in_p, wk_t, wg_p, bg, nb, transposed_k):
    m, d = x2.shape
    t = m // nb
    tm = min(512, t)
    nj = t // tm
    widths = (GLA_QK, GLA_QK, GLA_VW, GLA_VW, GLA_QK, CONV_CH, DIFF_VW, DIFF_VW)
    dtypes = [F32] * 6 + [BF16 if transposed_k else F32, F32]
    out_specs = [pl.BlockSpec((tm, n), lambda i: (i, 0)) for n in widths]
    out_shape = [jax.ShapeDtypeStruct((m, n), dt) for n, dt in zip(widths, dtypes)]
    if transposed_k:
        kt_spec = pl.BlockSpec((1, DIFF_VW, tm), lambda i: (i // nj, 0, i % nj))
        out_specs += [kt_spec, kt_spec, pl.BlockSpec((tm, DIFF_VW), lambda i: (i, 0))]
        out_shape += [jax.ShapeDtypeStruct((nb, DIFF_VW, t), F32), jax.ShapeDtypeStruct((nb, DIFF_VW, t), BF16),
                      jax.ShapeDtypeStruct((m, DIFF_VW), BF16)]
    else:
        out_specs.append(pl.BlockSpec((tm, DIFF_VW), lambda i: (i, 0)))
        out_shape.append(jax.ShapeDtypeStruct((m, DIFF_VW), F32))
    return pl.pallas_call(
        functools.partial(_inproj_body, transposed_k=transposed_k),
        grid=(m // tm,),
        in_specs=[pl.BlockSpec((tm, d), lambda i: (i, 0)),
                  _const_spec(w_in_p.shape), _const_spec(wk_t.shape), _const_spec(wg_p.shape), _const_spec(bg.shape)],
        out_specs=out_specs,
        out_shape=out_shape,
        compiler_params=_params("parallel"),
        name="in_proj",
    )(x2, w_in_p, wk_t, wg_p, bg)


def _gla_body(q_ref, k_ref, v_ref, lg_ref, gs_ref, nw_ref, s0_ref, o_ref, sout_ref, s_scr, *, cb):
    j = pl.program_id(1)

    @pl.when(j == 0)
    def _():
        s_scr[...] = s0_ref[0]

    q = q_ref[...]
    k = k_ref[...]
    v = v_ref[...]
    r = q.shape[0]
    row = lax.broadcasted_iota(jnp.int32, (r, 1), 0) % cb

    b = lg_ref[...]
    s = 1
    while s < cb:
        b = b + jnp.where(row >= s, pltpu.roll(b, s, 0), 0.0)
        s *= 2

    same_head = (lax.broadcasted_iota(jnp.int32, (GLA_QK, GLA_VW), 0) // GLA_DK
                 == lax.broadcasted_iota(jnp.int32, (GLA_QK, GLA_VW), 1) // GLA_DV)
    expand = same_head.astype(BF16)
    mask_t = (lax.broadcasted_iota(jnp.int32, (GLA_VW, GLA_QK), 0) // GLA_DV
              == lax.broadcasted_iota(jnp.int32, (GLA_VW, GLA_QK), 1) // GLA_DK).astype(F32)

    o = _mm((q * k).astype(BF16), expand) * v
    for delta in range(1, cb):
        ks = pltpu.roll(k, delta, 0)
        bs = pltpu.roll(b, delta, 0)
        vs = pltpu.roll(v, delta, 0)
        p = jnp.where(row >= delta, q * ks * jnp.exp(b - bs), 0.0)
        o = o + _mm(p.astype(BF16), expand) * vs

    st = s_scr[...]
    inter = []
    for t in range(r // cb):
        sl = slice(t * cb, (t + 1) * cb)
        bt = b[sl]
        bend = bt[cb - 1:cb]
        qe = (q[sl] * jnp.exp(bt)).astype(BF16)
        inter.append(_nt(qe, st.astype(BF16)))
        ke = k[sl] * jnp.exp(bend - bt)
        vt = v[sl]
        if cb < 16:
            ke = jnp.concatenate([ke, jnp.zeros((16 - cb, GLA_QK), F32)], axis=0)
            vt = jnp.concatenate([vt, jnp.zeros((16 - cb, GLA_VW), F32)], axis=0)
        kv = lax.dot_general(vt.astype(BF16), ke.astype(BF16), (((0,), (0,)), ((), ())),
                             preferred_element_type=F32)
        st = st * jnp.exp(bend) + kv * mask_t
    s_scr[...] = st
    o = o + (inter[0] if len(inter) == 1 else jnp.concatenate(inter, axis=0))

    grp = (lax.broadcasted_iota(jnp.int32, (GLA_VW, GLA_VW), 0) // GLA_DV
           == lax.broadcasted_iota(jnp.int32, (GLA_VW, GLA_VW), 1) // GLA_DV)
    mean_sq = _mm_split(o * o, jnp.where(grp, 1.0 / GLA_DV, 0.0).astype(BF16))
    o_ref[...] = o * lax.rsqrt(mean_sq + EPS) * nw_ref[...] * gs_ref[...]

    @pl.when(j == pl.num_programs(1) - 1)
    def _():
        sout_ref[0] = st


def _gla(gq, gk, gv, lg, gs, nw, s0_t, nb, t):
    cb = GLA_BLOCK if t % GLA_BLOCK == 0 else t
    r = min(t, 256)
    nj = t // r
    m = nb * t

    def rows(n):
        return pl.BlockSpec((r, n), lambda b, j: (b * nj + j, 0))

    return pl.pallas_call(
        functools.partial(_gla_body, cb=cb),
        grid=(nb, nj),
        in_specs=[rows(GLA_QK), rows(GLA_QK), rows(GLA_VW), rows(GLA_QK), rows(GLA_VW),
                  _const_spec(nw.shape),
                  pl.BlockSpec((1, GLA_VW, GLA_QK), lambda b, j: (b, 0, 0))],
        out_specs=[rows(GLA_VW), pl.BlockSpec((1, GLA_VW, GLA_QK), lambda b, j: (b, 0, 0))],
        out_shape=[jax.ShapeDtypeStruct((m, GLA_VW), F32), jax.ShapeDtypeStruct((nb, GLA_VW, GLA_QK), F32)],
        scratch_shapes=[pltpu.VMEM((GLA_VW, GLA_QK), F32)],
        compiler_params=_params("parallel", "arbitrary"),
        name="gla",
    )(gq, gk, gv, lg, gs, nw, s0_t)


def _conv_body(u_ref, buf_ref, cw_ref, cb_ref, g_ref, beta_ref, y_ref, nbuf_ref, seq_scr):
    j = pl.program_id(1)
    r = u_ref.shape[0]
    hist = CONV_WIDTH - 1

    @pl.when(j == 0)
    def _():
        seq_scr[0:CONV_HALO - hist, :] = jnp.zeros((CONV_HALO - hist, CONV_CH), F32)
        seq_scr[CONV_HALO - hist:CONV_HALO, :] = buf_ref[0]

    @pl.when(j > 0)
    def _():
        seq_scr[0:CONV_HALO, :] = seq_scr[r:r + CONV_HALO, :]

    seq_scr[CONV_HALO:CONV_HALO + r, :] = u_ref[...]

    acc = jnp.zeros((r, CONV_CH), F32) + cb_ref[...]
    for w in range(CONV_WIDTH):
        lo = CONV_HALO - hist + w
        acc = acc + seq_scr[lo:lo + r, :] * cw_ref[w:w + 1, :]
    y = _layer_norm(acc, g_ref[...], beta_ref[...])
    y_ref[...] = y * _sigmoid(y)

    @pl.when(j == pl.num_programs(1) - 1)
    def _():
        nbuf_ref[0] = seq_scr[r + CONV_HALO - hist:r + CONV_HALO, :]


def _conv(u, buf, cw, cb, g, beta, nb, t):
    r = min(t, 512)
    nj = t // r
    assert nj == 1 or r >= CONV_HALO
    hist = CONV_WIDTH - 1
    return pl.pallas_call(
        _conv_body,
        grid=(nb, nj),
        in_specs=[pl.BlockSpec((r, CONV_CH), lambda b, j: (b * nj + j, 0)),
                  pl.BlockSpec((1, hist, CONV_CH), lambda b, j: (b, 0, 0)),
                  _const_spec(cw.shape), _const_spec(cb.shape), _const_spec(g.shape), _const_spec(beta.shape)],
        out_specs=[pl.BlockSpec((r, CONV_CH), lambda b, j: (b * nj + j, 0)),
                   pl.BlockSpec((1, hist, CONV_CH), lambda b, j: (b, 0, 0))],
        out_shape=[jax.ShapeDtypeStruct((nb * t, CONV_CH), F32), jax.ShapeDtypeStruct((nb, hist, CONV_CH), F32)],
        scratch_shapes=[pltpu.VMEM((r + CONV_HALO, CONV_CH), F32)],
        compiler_params=_params("parallel", "arbitrary"),
        name="conv",
    )(u, buf, cw, cb, g, beta)


def _rel_bucket(dist):
    n = jnp.maximum(dist, 0)
    max_exact = REL_BUCKETS // 2
    nf = jnp.maximum(n, 1).astype(F32)
    large = max_exact + (jnp.log(nf / max_exact) / math.log(REL_MAX_DIST / max_exact)
                         * (REL_BUCKETS - max_exact)).astype(jnp.int32)
    large = jnp.minimum(large, REL_BUCKETS - 1)
    return jnp.where(n < max_exact, n, large)


def _bias_body(rb_ref, idx_ref, o_ref):
    h = pl.program_id(1)
    idx = idx_ref[0]
    acc = jnp.full(idx.shape, -jnp.inf, F32)
    for bucket in range(REL_BUCKETS):
        acc = jnp.where(idx == bucket, rb_ref[bucket, h], acc)
    o_ref[0, 0] = acc - rb_ref[REL_BUCKETS - 1, h]


def _bias_tiles(rel_bias, t):
    assert t >= REL_MAX_DIST
    ii = jnp.arange(t, dtype=jnp.int32)[:, None]
    jj = jnp.arange(t, dtype=jnp.int32)[None, :]
    idx = jnp.stack([jnp.where(ii >= jj, _rel_bucket(ii - jj), -1), _rel_bucket(t + ii - jj)])
    nh = rel_bias.shape[1]
    return pl.pallas_call(
        _bias_body,
        grid=(2, nh),
        in_specs=[pl.BlockSpec(memory_space=pltpu.SMEM),
                  pl.BlockSpec((1, t, t), lambda r, h: (r, 0, 0))],
        out_specs=pl.BlockSpec((1, 1, t, t), lambda r, h: (r, h, 0, 0)),
        out_shape=jax.ShapeDtypeStruct((2, nh, t, t), F32),
        compiler_params=_params("arbitrary", "arbitrary"),
        name="rel_bias_tiles",
    )(rel_bias, idx)


def _attn_body(lam_ref, q_ref, kt_ref, v_ref, bias_ref, dnw_ref, o_ref, s_scr, m_scr, l_scr, acc_scr, *, out_scale):
    h = pl.program_id(1)
    i = pl.program_id(2)
    t = q_ref.shape[0]
    lane = lax.broadcasted_iota(jnp.int32, (t, LANES), 1)
    q = q_ref[...]
    zero = jnp.zeros_like(q)
    qs = (jnp.where(lane < DIFF_DH, q, zero), jnp.where(lane >= DIFF_DH, q, zero))

    def cols(j):
        return pl.ds(pl.multiple_of(j * t, t), t)

    def far_tile(j, ms):
        kt = kt_ref[0, :, cols(j)]
        out = []
        for a in (0, 1):
            s = _mm(qs[a], kt)
            s_scr[a, :, cols(j)] = s
            out.append(jnp.maximum(ms[a], _fold_lanes(s, jnp.maximum)))
        return tuple(out)

    neg = jnp.full((t, LANES), -jnp.inf, F32)
    ms = lax.fori_loop(0, jnp.maximum(i - 1, 0), far_tile, (neg, neg))
    m_scr[0] = ms[0]
    m_scr[1] = ms[1]

    def near_tile(j, which):
        kt = kt_ref[0, :, cols(j)]
        for a in (0, 1):
            s = _mm(qs[a], kt) + bias_ref[which, 2 * h + a]
            s_scr[a, :, cols(j)] = s
            m_scr[a] = jnp.maximum(m_scr[a], _fold_lanes(s, jnp.maximum))

    @pl.when(i >= 1)
    def _():
        near_tile(i - 1, 1)

    near_tile(i, 0)

    for a in (0, 1):
        m_scr[a] = jnp.broadcast_to(jnp.max(m_scr[a], axis=-1, keepdims=True), (t, LANES))
    l_scr[...] = jnp.zeros(l_scr.shape, F32)
    acc_scr[...] = jnp.zeros(acc_scr.shape, F32)

    def pv_tile(j, carry):
        vb = v_ref[cols(j), :]
        for a in (0, 1):
            mb = m_scr[a]
            ps = [jnp.exp(s_scr[a, :, pl.ds(pl.multiple_of(j * t + c * LANES, LANES), LANES)] - mb)
                  for c in range(t // LANES)]
            part = ps[0]
            for pc in ps[1:]:
                part = part + pc
            l_scr[a] += part
            acc_scr[a] += _mm(jnp.concatenate(ps, axis=1).astype(BF16), vb)
        return carry

    lax.fori_loop(0, i + 1, pv_tile, 0)

    l0 = jnp.sum(l_scr[0], axis=-1, keepdims=True)
    l1 = jnp.sum(l_scr[1], axis=-1, keepdims=True)
    o = acc_scr[0] / l0 - lam_ref[0] * (acc_scr[1] / l1)
    ms_o = jnp.mean(o * o, axis=-1, keepdims=True)
    o_ref[...] = o * lax.rsqrt(ms_o + EPS) * dnw_ref[...] * out_scale


def _attn_prompt(q16, kt16, v16, bias, lam, dnw, nb, s, out_scale):
    t = bias.shape[-1]
    nq = s // t
    return pl.pallas_call(
        functools.partial(_attn_body, out_scale=out_scale),
        grid=(nb, DIFF_HEADS, nq),
        in_specs=[pl.BlockSpec(memory_space=pltpu.SMEM),
                  pl.BlockSpec((t, LANES), lambda b, h, i: (b * nq + i, h)),
                  pl.BlockSpec((1, LANES, s), lambda b, h, i: (b, h, 0)),
                  pl.BlockSpec((s, LANES), lambda b, h, i: (b, h)),
                  _const_spec(bias.shape), _const_spec(dnw.shape)],
        out_specs=pl.BlockSpec((t, LANES), lambda b, h, i: (b * nq + i, h)),
        out_shape=jax.ShapeDtypeStruct((nb * s, DIFF_VW), F32),
        scratch_shapes=[pltpu.VMEM((2, t, s), F32), pltpu.VMEM((2, t, LANES), F32), pltpu.VMEM((2, t, LANES), F32),
                        pltpu.VMEM((2, t, LANES), F32)],
        compiler_params=_params("parallel", "parallel", "arbitrary"),
        name="diff_attn_prompt",
    )(lam, q16, kt16, v16, bias, dnw)


def _sattn_body(pt_ref, lam_ref, q_ref, kn_ref, vn_ref, d1_ref, d0_ref, dnw_ref, *rest, npg, out_scale):
    del pt_ref
    kp = rest[:npg]
    vp = rest[npg:2 * npg]
    o_ref = rest[2 * npg]
    m_scr, l_scr, acc_scr = rest[2 * npg + 1:]
    p_id = pl.program_id(1)
    last = pl.num_programs(1) - 1
    nh = 2 * DIFF_HEADS
    tq = q_ref.shape[0]
    rows_h = 2 * tq

    q = q_ref[...]
    lane_head = lax.broadcasted_iota(jnp.int32, q.shape, 1) // DIFF_DH
    qs = jnp.concatenate([jnp.where(lane_head == hh, q, 0.0) for hh in range(nh)], axis=0).astype(BF16)

    @pl.when(p_id == 0)
    def _():
        m_scr[...] = jnp.full(m_scr.shape, -jnp.inf, F32)
        l_scr[...] = jnp.zeros(l_scr.shape, F32)
        acc_scr[...] = jnp.zeros(acc_scr.shape, F32)

    def update(s, values_of):
        m_old = m_scr[...]
        m_new = jnp.maximum(m_old, jnp.max(s, axis=-1, keepdims=True))
        alpha = jnp.exp(m_old - m_new)
        p = jnp.exp(s - m_new)
        l_scr[...] = alpha * l_scr[...] + jnp.sum(p, axis=-1, keepdims=True)
        pvs = []
        for h in range(DIFF_HEADS):
            ph = p[h * rows_h:(h + 1) * rows_h].astype(BF16)
            pv = _mm(ph[:, 0:PAGE_SIZE], values_of(0, h))
            for g in range(1, s.shape[1] // PAGE_SIZE):
                pv = pv + _mm(ph[:, g * PAGE_SIZE:(g + 1) * PAGE_SIZE], values_of(g, h))
            pvs.append(pv)
        acc_scr[...] = alpha * acc_scr[...] + jnp.concatenate(pvs, axis=0)
        m_scr[...] = m_new

    ss = []
    for g in range(npg):
        s = _mm(qs, kp[g][...].astype(BF16))
        if g == npg - 1:
            s = s + jnp.where(p_id == last, d1_ref[...], 0.0)
        ss.append(s)
    update(jnp.concatenate(ss, axis=1),
           lambda g, h: vp[g][pl.ds(h, PAGE_SIZE, stride=DIFF_HEADS), :].astype(BF16))

    @pl.when(p_id == last)
    def _():
        pad = jnp.zeros((PAGE_SIZE - tq, DIFF_VW), F32)
        kn = jnp.concatenate([kn_ref[...], pad], axis=0).astype(BF16)
        vn = jnp.concatenate([vn_ref[...], pad], axis=0).astype(BF16)
        update(_nt(qs, kn) + d0_ref[...], lambda g, h: vn[:, h * LANES:(h + 1) * LANES])
        o = acc_scr[...] / l_scr[...]
        lam = lam_ref[0]
        outs = []
        for h in range(DIFF_HEADS):
            oh = o[h * rows_h:h * rows_h + tq] - lam * o[h * rows_h + tq:(h + 1) * rows_h]
            ms = jnp.mean(oh * oh, axis=-1, keepdims=True)
            outs.append(oh * lax.rsqrt(ms + EPS) * dnw_ref[...] * out_scale)
        o_ref[...] = jnp.concatenate(outs, axis=1)


def _attn_sample(dq, dk, dv, cache_kt, cache_v2, page_table, layer, bias, lam, dnw, nb, tq, out_scale):
    n_pages = page_table.shape[1]
    npg = PAGES_PER_STEP
    assert n_pages % npg == 0 and cache_kt.shape[-1] == PAGE_SIZE and tq == 8
    nh = 2 * DIFF_HEADS
    t = bias.shape[-1]
    d1 = bias[1, :, 0:tq, t - PAGE_SIZE:t].reshape(nh * tq, PAGE_SIZE)
    d0 = jnp.concatenate([bias[0, :, 0:tq, 0:tq], jnp.full((nh, tq, PAGE_SIZE - tq), -jnp.inf, F32)],
                         axis=-1).reshape(nh * tq, PAGE_SIZE)

    def page_spec(g):
        return pl.BlockSpec((None, None, DIFF_VW, PAGE_SIZE),
                            lambda b, p, pt: (layer, pt[b * n_pages + p * npg + g], 0, 0))

    def rows_spec():
        return pl.BlockSpec((tq, DIFF_VW), lambda b, p, pt: (b, 0))

    def full_spec(a):
        nd = a.ndim
        return pl.BlockSpec(a.shape, lambda b, p, pt: (0,) * nd)

    grid_spec = pltpu.PrefetchScalarGridSpec(
        num_scalar_prefetch=1,
        grid=(nb, n_pages // npg),
        in_specs=[pl.BlockSpec(memory_space=pltpu.SMEM), rows_spec(), rows_spec(), rows_spec(),
                  full_spec(d1), full_spec(d0), full_spec(dnw)]
                 + [page_spec(g) for g in range(npg)] + [page_spec(g) for g in range(npg)],
        out_specs=rows_spec(),
        scratch_shapes=[pltpu.VMEM((nh * tq, 1), F32), pltpu.VMEM((nh * tq, 1), F32),
                        pltpu.VMEM((nh * tq, LANES), F32)],
    )
    return pl.pallas_call(
        functools.partial(_sattn_body, npg=npg, out_scale=out_scale),
        grid_spec=grid_spec,
        out_shape=jax.ShapeDtypeStruct((nb * tq, DIFF_VW), F32),
        compiler_params=_params("parallel", "arbitrary"),
        name="diff_attn_sample",
    )(page_table.reshape(-1), lam, dq, dk, dv, d1, d0, dnw, *([cache_kt] * npg), *([cache_v2] * npg))


def _outproj_body(oa_ref, ob_ref, oc_ref, x_ref, w_ref, g_ref, b_ref, y_ref):
    mix = _mm(oa_ref[...].astype(BF16), w_ref[0:GLA_VW, :])
    mix = mix + _mm(ob_ref[...].astype(BF16), w_ref[GLA_VW:GLA_VW + CONV_CH, :])
    mix = mix + _mm(oc_ref[...].astype(BF16), w_ref[GLA_VW + CONV_CH:, :])
    y_ref[...] = _layer_norm(ALPHA * x_ref[...] + mix, g_ref[...], b_ref[...])


def _out_proj(oa, ob, oc, x2, w, g, b):
    m, d = x2.shape
    tm = min(512, m)

    def rows(n):
        return pl.BlockSpec((tm, n), lambda i: (i, 0))

    return pl.pallas_call(
        _outproj_body,
        grid=(m // tm,),
        in_specs=[rows(GLA_VW), rows(CONV_CH), rows(DIFF_VW), rows(d),
                  _const_spec(w.shape), _const_spec(g.shape), _const_spec(b.shape)],
        out_specs=rows(d),
        out_shape=jax.ShapeDtypeStruct((m, d), F32),
        compiler_params=_params("parallel"),
        name="out_proj_ln",
    )(oa, ob, oc, x2, w, g, b)


def _ffn_body(x_ref, pe_ref, wg_ref, wu_ref, wd_ref, wpg_ref, wpp_ref, g_ref, b_ref, y_ref, *, chunk):
    x = x_ref[...]
    xb = x.astype(BF16)
    ple = _sigmoid(_mm(xb, wpg_ref[...])) * _mm(pe_ref[...].astype(BF16), wpp_ref[...])
    acc = ALPHA * x + ple
    for c in range(wg_ref.shape[1] // chunk):
        cols = slice(c * chunk, (c + 1) * chunk)
        gate = _mm(xb, wg_ref[:, cols])
        hid = gate * _sigmoid(gate) * _mm(xb, wu_ref[:, cols])
        acc = acc + _mm(hid.astype(BF16), wd_ref[cols, :])
    y_ref[...] = _layer_norm(acc, g_ref[...], b_ref[...])


def _ffn(x2, pe2, wg, wu, wd, wpg, wpp, g, b):
    m, d = x2.shape
    tm = min(512, m)

    def rows(n):
        return pl.BlockSpec((tm, n), lambda i: (i, 0))

    return pl.pallas_call(
        functools.partial(_ffn_body, chunk=256),
        grid=(m // tm,),
        in_specs=[rows(d), rows(pe2.shape[1])] + [_const_spec(a.shape) for a in (wg, wu, wd, wpg, wpp, g, b)],
        out_specs=rows(d),
        out_shape=jax.ShapeDtypeStruct((m, d), F32),
        compiler_params=_params("parallel"),
        name="ffn_ple_ln",
    )(x2, pe2, wg, wu, wd, wpg, wpp, g, b)


def _row(a):
    return a.reshape(1, -1)


def _state_to_t(s):
    nb = s.shape[0]
    eye = jnp.eye(GLA_HEADS, dtype=s.dtype)
    return jnp.einsum("bhde,hg->bhegd", s, eye).reshape(nb, GLA_VW, GLA_QK)


def _state_from_t(s_t):
    nb = s_t.shape[0]
    blocks = s_t.reshape(nb, GLA_HEADS, GLA_DV, GLA_HEADS, GLA_DK)
    diag = jnp.stack([blocks[:, h, :, h, :] for h in range(GLA_HEADS)], axis=1)
    return diag.transpose(0, 1, 3, 2)


def _layer(x2, pe2, nb, t, lw, layer, s0, buf, attend, transposed_k):
    gq, gk, gv, gs, lg, u, dq, dv, *kk = _in_proj(x2, lw["w_in"], lw["wk_t"], lw["wg"], lw["bg"], nb, transposed_k)
    o_a, s_t = _gla(gq, gk, gv, lg, gs, lw["gla_nw"], _state_to_t(s0), nb, t)
    o_b, nbuf = _conv(u, buf, lw["conv_w"], lw["conv_b"], lw["conv_g"], lw["conv_beta"], nb, t)
    lam_init = 0.8 - 0.6 * math.exp(-0.3 * layer)
    if transposed_k:
        o_c = attend(dq, kk[1], kk[2], lw["lam"], lw["dnw"], 1.0 - lam_init)
    else:
        o_c = attend(dq, kk[0], dv, lw["lam"], lw["dnw"], 1.0 - lam_init)
    x1 = _out_proj(o_a, o_b, o_c, x2, lw["w_out"], lw["ln1_g"], lw["ln1_b"])
    y = _ffn(x1, pe2, lw["w_ffn_gate"], lw["w_ffn_up"], lw["w_ffn_down"], lw["w_ple_gate"], lw["w_ple_proj"],
             lw["ln2_g"], lw["ln2_b"])
    return y, kk[0], dv, _state_from_t(s_t), nbuf


def _prep_w_in(w):
    sizes = (GLA_QK, GLA_QK, GLA_VW, GLA_VW, GLA_GATE_RANK, CONV_CH, CONV_CH, DIFF_VW, DIFF_VW, DIFF_VW)
    parts, s = [], 0
    for n in sizes:
        parts.append(w[:, s:s + n])
        s += n
    wk_t = parts[8].T.astype(BF16)
    parts[4] = jnp.pad(parts[4], ((0, 0), (0, LANES - GLA_GATE_RANK)))
    return jnp.concatenate(parts, axis=1).astype(BF16), wk_t


def _layer_weights(i, w_in, gla_w_gate_up, gla_b_gate, gla_norm_w, conv_w, conv_b, conv_ln_g, conv_ln_b, diff_lq1,
                   diff_lk1, diff_lq2, diff_lk2, diff_norm_w, w_out, ln1_g, ln1_b, w_ffn_gate, w_ffn_up, w_ffn_down,
                   w_ple_gate, w_ple_proj, ln2_g, ln2_b):
    lam_init = 0.8 - 0.6 * math.exp(-0.3 * i)
    lam = (jnp.exp(jnp.sum(diff_lq1[i] * diff_lk1[i])) - jnp.exp(jnp.sum(diff_lq2[i] * diff_lk2[i]))
           + lam_init).reshape(1).astype(F32)
    w_in_p, wk_t = _prep_w_in(w_in[i])
    return dict(
        w_in=w_in_p, wk_t=wk_t,
        wg=jnp.pad(gla_w_gate_up[i], ((0, LANES - GLA_GATE_RANK), (0, 0))).astype(BF16),
        bg=_row(gla_b_gate[i]),
        gla_nw=_row(jnp.tile(gla_norm_w[i], GLA_HEADS)),
        conv_w=conv_w[i], conv_b=_row(conv_b[i]), conv_g=_row(conv_ln_g[i]), conv_beta=_row(conv_ln_b[i]),
        lam=lam, dnw=_row(diff_norm_w[i]),
        w_out=w_out[i].astype(BF16), ln1_g=_row(ln1_g[i]), ln1_b=_row(ln1_b[i]),
        w_ffn_gate=w_ffn_gate[i].astype(BF16), w_ffn_up=w_ffn_up[i].astype(BF16),
        w_ffn_down=w_ffn_down[i].astype(BF16), w_ple_gate=w_ple_gate[i].astype(BF16),
        w_ple_proj=w_ple_proj[i].astype(BF16), ln2_g=_row(ln2_g[i]), ln2_b=_row(ln2_b[i]),
    )


def kernel(x_prompt, x_sample, cache_k, cache_v, state_gla, state_conv, page_table, p_prompt, p_sample, w_in, gla_w_gate_up, gla_b_gate, gla_norm_w, conv_w, conv_b, conv_ln_g, conv_ln_b, diff_lq1, diff_lk1, diff_lq2, diff_lk2, diff_norm_w, rel_bias, w_out, ln1_g, ln1_b, w_ffn_gate, w_ffn_up, w_ffn_down, w_ple_gate, w_ple_proj, ln2_g, ln2_b):
    nbp, s, d = x_prompt.shape
    nbs, ts, _ = x_sample.shape
    depth, n_pool = cache_k.shape[:2]
    assert depth == DEPTH and cache_k.shape[2] == PAGE_SIZE
    cache_kt = jnp.transpose(cache_k, (0, 1, 3, 4, 2)).reshape(depth, n_pool, DIFF_VW, PAGE_SIZE)
    cache_v2 = cache_v.reshape(depth, n_pool, PAGE_SIZE * DIFF_HEADS, 2 * DIFF_DH)
    bias = _bias_tiles(rel_bias, ATTN_TILE)

    yp = x_prompt.reshape(nbp * s, d)
    ys = x_sample.reshape(nbs * ts, d)
    zero_state = jnp.zeros((nbp, GLA_HEADS, GLA_DK, GLA_DV), F32)
    zero_buf = jnp.zeros((nbp, CONV_WIDTH - 1, CONV_CH), F32)
    outs = [[] for _ in range(8)]
    for i in range(depth):
        lw = _layer_weights(i, w_in, gla_w_gate_up, gla_b_gate, gla_norm_w, conv_w, conv_b, conv_ln_g, conv_ln_b,
                            diff_lq1, diff_lk1, diff_lq2, diff_lk2, diff_norm_w, w_out, ln1_g, ln1_b, w_ffn_gate,
                            w_ffn_up, w_ffn_down, w_ple_gate, w_ple_proj, ln2_g, ln2_b)

        def attend_prompt(q16, kt16, v16, lam, dnw, scale):
            return _attn_prompt(q16, kt16, v16, bias, lam, dnw, nbp, s, scale)

        def attend_sample(dq, dk, dv, lam, dnw, scale, layer=i):
            return _attn_sample(dq, dk, dv, cache_kt, cache_v2, page_table, layer, bias, lam, dnw, nbs, ts, scale)

        yp, kt_i, v_i, s_i, c_i = _layer(yp, p_prompt[i].reshape(nbp * s, -1), nbp, s, lw, i, zero_state, zero_buf,
                                         attend_prompt, True)
        outs[0].append(kt_i.reshape(nbp, 2 * DIFF_HEADS, DIFF_DH, s).transpose(0, 3, 1, 2))
        outs[1].append(v_i.reshape(nbp, s, DIFF_HEADS, 2 * DIFF_DH))
        outs[2].append(s_i)
        outs[3].append(c_i)
        ys, k_i, v_i, s_i, c_i = _layer(ys, p_sample[i].reshape(nbs * ts, -1), nbs, ts, lw, i, state_gla[i],
                                        state_conv[i], attend_sample, False)
        outs[4].append(k_i.reshape(nbs, ts, 2 * DIFF_HEADS, DIFF_DH))
        outs[5].append(v_i.reshape(nbs, ts, DIFF_HEADS, 2 * DIFF_DH))
        outs[6].append(s_i)
        outs[7].append(c_i)
    return (yp.reshape(nbp, s, d), ys.reshape(nbs, ts, d)) + tuple(jnp.stack(o) for o in outs)
```

```python
import functools
import math

import jax
import jax.numpy as jnp
from jax import lax
from jax.experimental import pallas as pl
from jax.experimental.pallas import tpu as pltpu

F32 = jnp.float32
BF16 = jnp.bfloat16

GLA_HEADS = 4
GLA_DK = 32
GLA_DV = 64
GLA_QK = GLA_HEADS * GLA_DK
GLA_VW = GLA_HEADS * GLA_DV
GLA_GATE_RANK = 16
GLA_TAU = 16.0
CONV_CH = 256
CONV_WIDTH = 31
DIFF_HEADS = 4
DIFF_DH = 64
DIFF_VW = 2 * DIFF_HEADS * DIFF_DH
REL_BUCKETS = 32
REL_MAX_DIST = 128
PAGE_SIZE = 128
DEPTH = 2
ALPHA = (2 * DEPTH) ** 0.25
EPS = 1e-5
LOG2E = math.log2(math.e)

LANES = 128
GLA_BLOCK = 16
ATTN_TILE = 256
PAGES_PER_STEP = 16
CONV_HALO = 32
VMEM_LIMIT = 56 * 1024 * 1024

_IN_OFF = {}
_o = 0
for _name, _n in (("gq", 128), ("gk", 128), ("gv", 256), ("gg", 256), ("glr", 128), ("ca", 256), ("cg", 256),
                  ("dq", 512), ("dk", 512), ("dv", 512)):
    _IN_OFF[_name] = (_o, _n)
    _o += _n
N_IN_PAD = _o


def _params(*sem):
    return pltpu.CompilerParams(dimension_semantics=sem, vmem_limit_bytes=VMEM_LIMIT)


def _const_spec(shape):
    nd = len(shape)
    return pl.BlockSpec(shape, lambda *_: (0,) * nd, pipeline_mode=pl.Buffered(1))


def _nt(a, b):
    return lax.dot_general(a, b, (((1,), (1,)), ((), ())), preferred_element_type=F32)


def _mm(a, b):
    return jnp.dot(a, b, preferred_element_type=F32)


def _mm_split(a, b):
    hi = a.astype(BF16)
    lo = (a - hi.astype(F32)).astype(BF16)
    return _mm(hi, b) + _mm(lo, b)


def _sigmoid(x):
    return 1.0 / (1.0 + jnp.exp(-x))


def _layer_norm(x, g, b):
    mu = jnp.mean(x, axis=-1, keepdims=True)
    xc = x - mu
    var = jnp.mean(xc * xc, axis=-1, keepdims=True)
    return xc * lax.rsqrt(var + EPS) * g + b


def _fold_lanes(x, op):
    acc = x[:, 0:LANES]
    for c in range(1, x.shape[1] // LANES):
        acc = op(acc, x[:, c * LANES:(c + 1) * LANES])
    return acc


def _inproj_body(x_ref, w_ref, wkt_ref, wg_ref, bg_ref, gq_ref, gk_ref, gv_ref, gs_ref, lg_ref, u_ref, dq_ref,
                 dv_ref, *k_refs, transposed_k):
    x = x_ref[...].astype(BF16)

    def proj(name):
        lo, n = _IN_OFF[name]
        return _mm(x, w_ref[:, lo:lo + n])

    gq_ref[...] = proj("gq") * GLA_DK ** -0.5
    gk_ref[...] = proj("gk")
    gv_ref[...] = proj("gv")
    gg = proj("gg")
    gs_ref[...] = gg * _sigmoid(gg)
    z = _mm(proj("glr").astype(BF16), wg_ref[...]) + bg_ref[...]
    lg_ref[...] = (jnp.minimum(z, 0.0) - jnp.log(1.0 + jnp.exp(-jnp.abs(z)))) * (1.0 / GLA_TAU)
    u_ref[...] = proj("ca") * _sigmoid(proj("cg"))
    dq_ref[...] = (proj("dq") * (DIFF_DH ** -0.5 * LOG2E)).astype(dq_ref.dtype)
    dv = proj("dv")
    if transposed_k:
        kt_ref, kt16_ref, v16_ref = k_refs
        tm = x.shape[0]
        for h in range(DIFF_HEADS):
            dv_ref[pl.ds(h, tm, stride=DIFF_HEADS), :] = dv[:, h * LANES:(h + 1) * LANES]
        kt = _nt(wkt_ref[...], x)
        kt_ref[0] = kt
        kt16_ref[0] = kt.astype(BF16)
        v16_ref[...] = dv.astype(BF16)
    else:
        dv_ref[...] = dv
        k_refs[0][...] = proj("dk")


def _in_proj(x2, w_in_p, wk_t, wg_p, bg, nb, transposed_k):
    m, d = x2.shape
    t = m // nb
    tm = min(512, t)
    nj = t // tm
    widths = (GLA_QK, GLA_QK, GLA_VW, GLA_VW, GLA_QK, CONV_CH, DIFF_VW)
    dtypes = [F32] * 6 + [BF16 if transposed_k else F32]
    out_specs = [pl.BlockSpec((tm, n), lambda i: (i, 0)) for n in widths]
    out_shape = [jax.ShapeDtypeStruct((m, n), dt) for n, dt in zip(widths, dtypes)]
    if transposed_k:
        out_specs.append(pl.BlockSpec((tm * DIFF_HEADS, LANES), lambda i: (i, 0)))
        out_shape.append(jax.ShapeDtypeStruct((m * DIFF_HEADS, LANES), F32))
        kt_spec = pl.BlockSpec((1, DIFF_VW, tm), lambda i: (i // nj, 0, i % nj))
        out_specs += [kt_spec, kt_spec, pl.BlockSpec((tm, DIFF_VW), lambda i: (i, 0))]
        out_shape += [jax.ShapeDtypeStruct((nb, DIFF_VW, t), F32), jax.ShapeDtypeStruct((nb, DIFF_VW, t), BF16),
                      jax.ShapeDtypeStruct((m, DIFF_VW), BF16)]
    else:
        out_specs += [pl.BlockSpec((tm, DIFF_VW), lambda i: (i, 0))] * 2
        out_shape += [jax.ShapeDtypeStruct((m, DIFF_VW), F32)] * 2
    return pl.pallas_call(
        functools.partial(_inproj_body, transposed_k=transposed_k),
        grid=(m // tm,),
        in_specs=[pl.BlockSpec((tm, d), lambda i: (i, 0)),
                  _const_spec(w_in_p.shape), _const_spec(wk_t.shape), _const_spec(wg_p.shape), _const_spec(bg.shape)],
        out_specs=out_specs,
        out_shape=out_shape,
        compiler_params=_params("parallel"),
        name="in_proj",
    )(x2, w_in_p, wk_t, wg_p, bg)


def _gla_body(q_ref, k_ref, v_ref, lg_ref, gs_ref, nw_ref, s0_ref, o_ref, sout_ref, s_scr, *, cb):
    j = pl.program_id(1)

    @pl.when(j == 0)
    def _():
        s_scr[...] = s0_ref[0]

    q = q_ref[...]
    k = k_ref[...]
    v = v_ref[...]
    r = q.shape[0]
    row = lax.broadcasted_iota(jnp.int32, (r, 1), 0) % cb

    b = lg_ref[...]
    s = 1
    while s < cb:
        b = b + jnp.where(row >= s, pltpu.roll(b, s, 0), 0.0)
        s *= 2

    same_head = (lax.broadcasted_iota(jnp.int32, (GLA_QK, GLA_VW), 0) // GLA_DK
                 == lax.broadcasted_iota(jnp.int32, (GLA_QK, GLA_VW), 1) // GLA_DV)
    expand = same_head.astype(BF16)
    mask_t = (lax.broadcasted_iota(jnp.int32, (GLA_VW, GLA_QK), 0) // GLA_DV
              == lax.broadcasted_iota(jnp.int32, (GLA_VW, GLA_QK), 1) // GLA_DK).astype(F32)

    o = _mm((q * k).astype(BF16), expand) * v
    for delta in range(1, cb):
        ks = pltpu.roll(k, delta, 0)
        bs = pltpu.roll(b, delta, 0)
        vs = pltpu.roll(v, delta, 0)
        p = jnp.where(row >= delta, q * ks * jnp.exp(b - bs), 0.0)
        o = o + _mm(p.astype(BF16), expand) * vs

    st = s_scr[...]
    inter = []
    for t in range(r // cb):
        sl = slice(t * cb, (t + 1) * cb)
        bt = b[sl]
        bend = bt[cb - 1:cb]
        qe = (q[sl] * jnp.exp(bt)).astype(BF16)
        inter.append(_nt(qe, st.astype(BF16)))
        ke = k[sl] * jnp.exp(bend - bt)
        vt = v[sl]
        if cb < 16:
            ke = jnp.concatenate([ke, jnp.zeros((16 - cb, GLA_QK), F32)], axis=0)
            vt = jnp.concatenate([vt, jnp.zeros((16 - cb, GLA_VW), F32)], axis=0)
        kv = lax.dot_general(vt.astype(BF16), ke.astype(BF16), (((0,), (0,)), ((), ())),
                             preferred_element_type=F32)
        st = st * jnp.exp(bend) + kv * mask_t
    s_scr[...] = st
    o = o + (inter[0] if len(inter) == 1 else jnp.concatenate(inter, axis=0))

    grp = (lax.broadcasted_iota(jnp.int32, (GLA_VW, GLA_VW), 0) // GLA_DV
           == lax.broadcasted_iota(jnp.int32, (GLA_VW, GLA_VW), 1) // GLA_DV)
    mean_sq = _mm_split(o * o, jnp.where(grp, 1.0 / GLA_DV, 0.0).astype(BF16))
    o_ref[...] = o * lax.rsqrt(mean_sq + EPS) * nw_ref[...] * gs_ref[...]

    @pl.when(j == pl.num_programs(1) - 1)
    def _():
        sout_ref[0] = st


def _gla(gq, gk, gv, lg, gs, nw, s0_t, nb, t):
    cb = GLA_BLOCK if t % GLA_BLOCK == 0 else t
    r = min(t, 256)
    nj = t // r
    m = nb * t

    def rows(n):
        return pl.BlockSpec((r, n), lambda b, j: (b * nj + j, 0))

    return pl.pallas_call(
        functools.partial(_gla_body, cb=cb),
        grid=(nb, nj),
        in_specs=[rows(GLA_QK), rows(GLA_QK), rows(GLA_VW), rows(GLA_QK), rows(GLA_VW),
                  _const_spec(nw.shape),
                  pl.BlockSpec((1, GLA_VW, GLA_QK), lambda b, j: (b, 0, 0))],
        out_specs=[rows(GLA_VW), pl.BlockSpec((1, GLA_VW, GLA_QK), lambda b, j: (b, 0, 0))],
        out_shape=[jax.ShapeDtypeStruct((m, GLA_VW), F32), jax.ShapeDtypeStruct((nb, GLA_VW, GLA_QK), F32)],
        scratch_shapes=[pltpu.VMEM((GLA_VW, GLA_QK), F32)],
        compiler_params=_params("parallel", "arbitrary"),
        name="gla",
    )(gq, gk, gv, lg, gs, nw, s0_t)


def _conv_body(u_ref, buf_ref, cw_ref, cb_ref, g_ref, beta_ref, y_ref, nbuf_ref, seq_scr):
    j = pl.program_id(1)
    r = u_ref.shape[0]
    hist = CONV_WIDTH - 1

    @pl.when(j == 0)
    def _():
        seq_scr[0:CONV_HALO - hist, :] = jnp.zeros((CONV_HALO - hist, CONV_CH), F32)
        seq_scr[CONV_HALO - hist:CONV_HALO, :] = buf_ref[0]

    @pl.when(j > 0)
    def _():
        seq_scr[0:CONV_HALO, :] = seq_scr[r:r + CONV_HALO, :]

    seq_scr[CONV_HALO:CONV_HALO + r, :] = u_ref[...]

    acc = jnp.zeros((r, CONV_CH), F32) + cb_ref[...]
    for w in range(CONV_WIDTH):
        lo = CONV_HALO - hist + w
        acc = acc + seq_scr[lo:lo + r, :] * cw_ref[w:w + 1, :]
    y = _layer_norm(acc, g_ref[...], beta_ref[...])
    y_ref[...] = y * _sigmoid(y)

    @pl.when(j == pl.num_programs(1) - 1)
    def _():
        nbuf_ref[0] = seq_scr[r + CONV_HALO - hist:r + CONV_HALO, :]


def _conv(u, buf, cw, cb, g, beta, nb, t):
    r = min(t, 512)
    nj = t // r
    assert nj == 1 or r >= CONV_HALO
    hist = CONV_WIDTH - 1
    return pl.pallas_call(
        _conv_body,
        grid=(nb, nj),
        in_specs=[pl.BlockSpec((r, CONV_CH), lambda b, j: (b * nj + j, 0)),
                  pl.BlockSpec((1, hist, CONV_CH), lambda b, j: (b, 0, 0)),
                  _const_spec(cw.shape), _const_spec(cb.shape), _const_spec(g.shape), _const_spec(beta.shape)],
        out_specs=[pl.BlockSpec((r, CONV_CH), lambda b, j: (b * nj + j, 0)),
                   pl.BlockSpec((1, hist, CONV_CH), lambda b, j: (b, 0, 0))],
        out_shape=[jax.ShapeDtypeStruct((nb * t, CONV_CH), F32), jax.ShapeDtypeStruct((nb, hist, CONV_CH), F32)],
        scratch_shapes=[pltpu.VMEM((r + CONV_HALO, CONV_CH), F32)],
        compiler_params=_params("parallel", "arbitrary"),
        name="conv",
    )(u, buf, cw, cb, g, beta)


def _rel_bucket(dist):
    n = jnp.maximum(dist, 0)
    max_exact = REL_BUCKETS // 2
    nf = jnp.maximum(n, 1).astype(F32)
    large = max_exact + (jnp.log(nf / max_exact) / math.log(REL_MAX_DIST / max_exact)
                         * (REL_BUCKETS - max_exact)).astype(jnp.int32)
    large = jnp.minimum(large, REL_BUCKETS - 1)
    return jnp.where(n < max_exact, n, large)


def _bias_body(rb_ref, idx_ref, o_ref):
    h = pl.program_id(1)
    idx = idx_ref[0]
    acc = jnp.full(idx.shape, -jnp.inf, F32)
    for bucket in range(REL_BUCKETS):
        acc = jnp.where(idx == bucket, rb_ref[bucket, h], acc)
    o_ref[0, 0] = (acc - rb_ref[REL_BUCKETS - 1, h]) * LOG2E


def _bias_tiles(rel_bias, t):
    assert t >= REL_MAX_DIST
    ii = jnp.arange(t, dtype=jnp.int32)[:, None]
    jj = jnp.arange(t, dtype=jnp.int32)[None, :]
    idx = jnp.stack([jnp.where(ii >= jj, _rel_bucket(ii - jj), -1), _rel_bucket(t + ii - jj)])
    nh = rel_bias.shape[1]
    return pl.pallas_call(
        _bias_body,
        grid=(2, nh),
        in_specs=[pl.BlockSpec(memory_space=pltpu.SMEM),
                  pl.BlockSpec((1, t, t), lambda r, h: (r, 0, 0))],
        out_specs=pl.BlockSpec((1, 1, t, t), lambda r, h: (r, h, 0, 0)),
        out_shape=jax.ShapeDtypeStruct((2, nh, t, t), F32),
        compiler_params=_params("arbitrary", "arbitrary"),
        name="rel_bias_tiles",
    )(rel_bias, idx)


def _attn_body(lam_ref, q_ref, kt_ref, v_ref, bias_ref, dnw_ref, o_ref, q_scr, s_scr, m_scr, l_scr, acc_scr, *,
               out_scale):
    i = pl.program_id(1)
    t = q_ref.shape[0]
    nmaps = 2 * DIFF_HEADS
    lane = lax.broadcasted_iota(jnp.int32, (t, LANES), 1)
    for n in range(nmaps):
        qh = q_ref[:, (n // 2) * LANES:(n // 2 + 1) * LANES]
        keep = (lane < DIFF_DH) if n % 2 == 0 else (lane >= DIFF_DH)
        q_scr[n] = jnp.where(keep, qh, jnp.zeros_like(qh))

    def cols(j):
        return pl.ds(pl.multiple_of(j * t, t), t)

    def logits_tile(j, which):
        for n in range(nmaps):
            kt = kt_ref[0, (n // 2) * LANES:(n // 2 + 1) * LANES, cols(j)]
            s = _mm(q_scr[n], kt)
            if which is not None:
                s = s + bias_ref[which, n]
            s_scr[n, :, cols(j)] = s
            m_scr[n] = jnp.maximum(m_scr[n], _fold_lanes(s, jnp.maximum))

    m_scr[...] = jnp.full(m_scr.shape, -jnp.inf, F32)

    def far_tile(j, carry):
        logits_tile(j, None)
        return carry

    lax.fori_loop(0, jnp.maximum(i - 1, 0), far_tile, 0)

    @pl.when(i >= 1)
    def _():
        logits_tile(i - 1, 1)

    logits_tile(i, 0)

    for n in range(nmaps):
        m_scr[n] = jnp.broadcast_to(jnp.max(m_scr[n], axis=-1, keepdims=True), (t, LANES))
    l_scr[...] = jnp.zeros(l_scr.shape, F32)
    acc_scr[...] = jnp.zeros(acc_scr.shape, F32)

    def pv_tile(j, carry):
        for n in range(nmaps):
            vb = v_ref[cols(j), (n // 2) * LANES:(n // 2 + 1) * LANES]
            mb = m_scr[n]
            ps = [jnp.exp2(s_scr[n, :, pl.ds(pl.multiple_of(j * t + c * LANES, LANES), LANES)] - mb)
                  for c in range(t // LANES)]
            part = ps[0]
            for pc in ps[1:]:
                part = part + pc
            l_scr[n] += part
            acc_scr[n] += _mm(jnp.concatenate(ps, axis=1).astype(BF16), vb)
        return carry

    lax.fori_loop(0, i + 1, pv_tile, 0)

    lam = lam_ref[0]
    outs = []
    for hp in range(DIFF_HEADS):
        l0 = jnp.sum(l_scr[2 * hp], axis=-1, keepdims=True)
        l1 = jnp.sum(l_scr[2 * hp + 1], axis=-1, keepdims=True)
        o = acc_scr[2 * hp] / l0 - lam * (acc_scr[2 * hp + 1] / l1)
        ms_o = jnp.mean(o * o, axis=-1, keepdims=True)
        outs.append(o * lax.rsqrt(ms_o + EPS) * dnw_ref[...] * out_scale)
    o_ref[...] = jnp.concatenate(outs, axis=1)


def _attn_prompt(q16, kt16, v16, bias, lam, dnw, nb, s, out_scale):
    t = bias.shape[-1]
    nq = s // t
    nmaps = 2 * DIFF_HEADS
    return pl.pallas_call(
        functools.partial(_attn_body, out_scale=out_scale),
        grid=(nb, nq),
        in_specs=[pl.BlockSpec(memory_space=pltpu.SMEM),
                  pl.BlockSpec((t, DIFF_VW), lambda b, i: (b * nq + i, 0)),
                  pl.BlockSpec((1, DIFF_VW, s), lambda b, i: (b, 0, 0)),
                  pl.BlockSpec((s, DIFF_VW), lambda b, i: (b, 0)),
                  _const_spec(bias.shape), _const_spec(dnw.shape)],
        out_specs=pl.BlockSpec((t, DIFF_VW), lambda b, i: (b * nq + i, 0)),
        out_shape=jax.ShapeDtypeStruct((nb * s, DIFF_VW), F32),
        scratch_shapes=[pltpu.VMEM((nmaps, t, LANES), BF16), pltpu.VMEM((nmaps, t, s), F32),
                        pltpu.VMEM((nmaps, t, LANES), F32), pltpu.VMEM((nmaps, t, LANES), F32),
                        pltpu.VMEM((nmaps, t, LANES), F32)],
        compiler_params=_params("parallel", "arbitrary"),
        name="diff_attn_prompt",
    )(lam, q16, kt16, v16, bias, dnw)


def _sattn_body(pt_ref, lam_ref, q_ref, kn_ref, vn_ref, d1_ref, d0_ref, dnw_ref, *rest, npg, out_scale):
    del pt_ref
    kp = rest[:npg]
    vp = rest[npg:2 * npg]
    o_ref = rest[2 * npg]
    m_scr, l_scr, acc_scr = rest[2 * npg + 1:]
    p_id = pl.program_id(1)
    last = pl.num_programs(1) - 1
    nh = 2 * DIFF_HEADS
    tq = q_ref.shape[0]
    rows_h = 2 * tq

    q = q_ref[...]
    lane_head = lax.broadcasted_iota(jnp.int32, q.shape, 1) // DIFF_DH
    qs = jnp.concatenate([jnp.where(lane_head == hh, q, 0.0) for hh in range(nh)], axis=0).astype(BF16)

    @pl.when(p_id == 0)
    def _():
        m_scr[...] = jnp.full(m_scr.shape, -jnp.inf, F32)
        l_scr[...] = jnp.zeros(l_scr.shape, F32)
        acc_scr[...] = jnp.zeros(acc_scr.shape, F32)

    def update(s, values_of):
        m_old = m_scr[...]
        m_new = jnp.maximum(m_old, jnp.max(s, axis=-1, keepdims=True))
        alpha = jnp.exp2(m_old - m_new)
        p = jnp.exp2(s - m_new)
        l_scr[...] = alpha * l_scr[...] + jnp.sum(p, axis=-1, keepdims=True)
        pvs = []
        for h in range(DIFF_HEADS):
            ph = p[h * rows_h:(h + 1) * rows_h].astype(BF16)
            pv = _mm(ph[:, 0:PAGE_SIZE], values_of(0, h))
            for g in range(1, s.shape[1] // PAGE_SIZE):
                pv = pv + _mm(ph[:, g * PAGE_SIZE:(g + 1) * PAGE_SIZE], values_of(g, h))
            pvs.append(pv)
        acc_scr[...] = alpha * acc_scr[...] + jnp.concatenate(pvs, axis=0)
        m_scr[...] = m_new

    ss = []
    for g in range(npg):
        s = _mm(qs, kp[g][...].astype(BF16))
        if g == npg - 1:
            s = s + jnp.where(p_id == last, d1_ref[...], 0.0)
        ss.append(s)
    update(jnp.concatenate(ss, axis=1),
           lambda g, h: vp[g][pl.ds(h, PAGE_SIZE, stride=DIFF_HEADS), :].astype(BF16))

    @pl.when(p_id == last)
    def _():
        pad = jnp.zeros((PAGE_SIZE - tq, DIFF_VW), F32)
        kn = jnp.concatenate([kn_ref[...], pad], axis=0).astype(BF16)
        vn = jnp.concatenate([vn_ref[...], pad], axis=0).astype(BF16)
        update(_nt(qs, kn) + d0_ref[...], lambda g, h: vn[:, h * LANES:(h + 1) * LANES])
        o = acc_scr[...] / l_scr[...]
        lam = lam_ref[0]
        outs = []
        for h in range(DIFF_HEADS):
            oh = o[h * rows_h:h * rows_h + tq] - lam * o[h * rows_h + tq:(h + 1) * rows_h]
            ms = jnp.mean(oh * oh, axis=-1, keepdims=True)
            outs.append(oh * lax.rsqrt(ms + EPS) * dnw_ref[...] * out_scale)
        o_ref[...] = jnp.concatenate(outs, axis=1)


def _attn_sample(dq, dk, dv, cache_kt, cache_v2, page_table, layer, bias, lam, dnw, nb, tq, out_scale):
    n_pages = page_table.shape[1]
    npg = PAGES_PER_STEP
    assert n_pages % npg == 0 and cache_kt.shape[-1] == PAGE_SIZE and tq == 8
    nh = 2 * DIFF_HEADS
    t = bias.shape[-1]
    d1 = bias[1, :, 0:tq, t - PAGE_SIZE:t].reshape(nh * tq, PAGE_SIZE)
    d0 = jnp.concatenate([bias[0, :, 0:tq, 0:tq], jnp.full((nh, tq, PAGE_SIZE - tq), -jnp.inf, F32)],
                         axis=-1).reshape(nh * tq, PAGE_SIZE)

    def page_spec(g):
        return pl.BlockSpec((None, None, DIFF_VW, PAGE_SIZE),
                            lambda b, p, pt: (layer, pt[b * n_pages + p * npg + g], 0, 0))

    def rows_spec():
        return pl.BlockSpec((tq, DIFF_VW), lambda b, p, pt: (b, 0))

    def full_spec(a):
        nd = a.ndim
        return pl.BlockSpec(a.shape, lambda b, p, pt: (0,) * nd)

    grid_spec = pltpu.PrefetchScalarGridSpec(
        num_scalar_prefetch=1,
        grid=(nb, n_pages // npg),
        in_specs=[pl.BlockSpec(memory_space=pltpu.SMEM), rows_spec(), rows_spec(), rows_spec(),
                  full_spec(d1), full_spec(d0), full_spec(dnw)]
                 + [page_spec(g) for g in range(npg)] + [page_spec(g) for g in range(npg)],
        out_specs=rows_spec(),
        scratch_shapes=[pltpu.VMEM((nh * tq, 1), F32), pltpu.VMEM((nh * tq, 1), F32),
                        pltpu.VMEM((nh * tq, LANES), F32)],
    )
    return pl.pallas_call(
        functools.partial(_sattn_body, npg=npg, out_scale=out_scale),
        grid_spec=grid_spec,
        out_shape=jax.ShapeDtypeStruct((nb * tq, DIFF_VW), F32),
        compiler_params=_params("parallel", "arbitrary"),
        name="diff_attn_sample",
    )(page_table.reshape(-1), lam, dq, dk, dv, d1, d0, dnw, *([cache_kt] * npg), *([cache_v2] * npg))


def _outproj_body(oa_ref, ob_ref, oc_ref, x_ref, w_ref, g_ref, b_ref, y_ref):
    mix = _mm(oa_ref[...].astype(BF16), w_ref[0:GLA_VW, :])
    mix = mix + _mm(ob_ref[...].astype(BF16), w_ref[GLA_VW:GLA_VW + CONV_CH, :])
    mix = mix + _mm(oc_ref[...].astype(BF16), w_ref[GLA_VW + CONV_CH:, :])
    y_ref[...] = _layer_norm(ALPHA * x_ref[...] + mix, g_ref[...], b_ref[...])


def _out_proj(oa, ob, oc, x2, w, g, b):
    m, d = x2.shape
    tm = min(512, m)

    def rows(n):
        return pl.BlockSpec((tm, n), lambda i: (i, 0))

    return pl.pallas_call(
        _outproj_body,
        grid=(m // tm,),
        in_specs=[rows(GLA_VW), rows(CONV_CH), rows(DIFF_VW), rows(d),
                  _const_spec(w.shape), _const_spec(g.shape), _const_spec(b.shape)],
        out_specs=rows(d),
        out_shape=jax.ShapeDtypeStruct((m, d), F32),
        compiler_params=_params("parallel"),
        name="out_proj_ln",
    )(oa, ob, oc, x2, w, g, b)


def _ffn_body(x_ref, pe_ref, wg_ref, wu_ref, wd_ref, wpg_ref, wpp_ref, g_ref, b_ref, y_ref, *, chunk):
    x = x_ref[...]
    xb = x.astype(BF16)
    ple = _sigmoid(_mm(xb, wpg_ref[...])) * _mm(pe_ref[...].astype(BF16), wpp_ref[...])
    acc = ALPHA * x + ple
    for c in range(wg_ref.shape[1] // chunk):
        cols = slice(c * chunk, (c + 1) * chunk)
        gate = _mm(xb, wg_ref[:, cols])
        hid = gate * _sigmoid(gate) * _mm(xb, wu_ref[:, cols])
        acc = acc + _mm(hid.astype(BF16), wd_ref[cols, :])
    y_ref[...] = _layer_norm(acc, g_ref[...], b_ref[...])


def _ffn(x2, pe_all, layer, wg, wu, wd, wpg, wpp, g, b):
    m, d = x2.shape
    tm = min(512, m)
    nt = m // tm

    def rows(n):
        return pl.BlockSpec((tm, n), lambda i: (i, 0))

    return pl.pallas_call(
        functools.partial(_ffn_body, chunk=256),
        grid=(nt,),
        in_specs=[rows(d), pl.BlockSpec((tm, pe_all.shape[1]), lambda i: (layer * nt + i, 0))]
                 + [_const_spec(a.shape) for a in (wg, wu, wd, wpg, wpp, g, b)],
        out_specs=rows(d),
        out_shape=jax.ShapeDtypeStruct((m, d), F32),
        compiler_params=_params("parallel"),
        name="ffn_ple_ln",
    )(x2, pe_all, wg, wu, wd, wpg, wpp, g, b)


def _row(a):
    return a.reshape(1, -1)


def _state_to_t(s):
    nb = s.shape[0]
    eye = jnp.eye(GLA_HEADS, dtype=s.dtype)
    return jnp.einsum("bhde,hg->bhegd", s, eye).reshape(nb, GLA_VW, GLA_QK)


def _state_from_t(s_t):
    nb = s_t.shape[0]
    blocks = s_t.reshape(nb, GLA_HEADS, GLA_DV, GLA_HEADS, GLA_DK)
    diag = jnp.stack([blocks[:, h, :, h, :] for h in range(GLA_HEADS)], axis=1)
    return diag.transpose(0, 1, 3, 2)


def _layer(x2, pe_all, nb, t, lw, layer, s0, buf, attend, transposed_k):
    gq, gk, gv, gs, lg, u, dq, dv, *kk = _in_proj(x2, lw["w_in"], lw["wk_t"], lw["wg"], lw["bg"], nb, transposed_k)
    o_a, s_t = _gla(gq, gk, gv, lg, gs, lw["gla_nw"], _state_to_t(s0), nb, t)
    o_b, nbuf = _conv(u, buf, lw["conv_w"], lw["conv_b"], lw["conv_g"], lw["conv_beta"], nb, t)
    lam_init = 0.8 - 0.6 * math.exp(-0.3 * layer)
    if transposed_k:
        o_c = attend(dq, kk[1], kk[2], lw["lam"], lw["dnw"], 1.0 - lam_init)
    else:
        o_c = attend(dq, kk[0], dv, lw["lam"], lw["dnw"], 1.0 - lam_init)
    x1 = _out_proj(o_a, o_b, o_c, x2, lw["w_out"], lw["ln1_g"], lw["ln1_b"])
    y = _ffn(x1, pe_all, layer, lw["w_ffn_gate"], lw["w_ffn_up"], lw["w_ffn_down"], lw["w_ple_gate"], lw["w_ple_proj"],
             lw["ln2_g"], lw["ln2_b"])
    return y, kk[0], dv, _state_from_t(s_t), nbuf


def _prep_w_in(w):
    sizes = (GLA_QK, GLA_QK, GLA_VW, GLA_VW, GLA_GATE_RANK, CONV_CH, CONV_CH, DIFF_VW, DIFF_VW, DIFF_VW)
    parts, s = [], 0
    for n in sizes:
        parts.append(w[:, s:s + n])
        s += n
    wk_t = parts[8].T.astype(BF16)
    parts[4] = jnp.pad(parts[4], ((0, 0), (0, LANES - GLA_GATE_RANK)))
    return jnp.concatenate(parts, axis=1).astype(BF16), wk_t


def _layer_weights(i, w_in, gla_w_gate_up, gla_b_gate, gla_norm_w, conv_w, conv_b, conv_ln_g, conv_ln_b, diff_lq1,
                   diff_lk1, diff_lq2, diff_lk2, diff_norm_w, w_out, ln1_g, ln1_b, w_ffn_gate, w_ffn_up, w_ffn_down,
                   w_ple_gate, w_ple_proj, ln2_g, ln2_b):
    lam_init = 0.8 - 0.6 * math.exp(-0.3 * i)
    lam = (jnp.exp(jnp.sum(diff_lq1[i] * diff_lk1[i])) - jnp.exp(jnp.sum(diff_lq2[i] * diff_lk2[i]))
           + lam_init).reshape(1).astype(F32)
    w_in_p, wk_t = _prep_w_in(w_in[i])
    return dict(
        w_in=w_in_p, wk_t=wk_t,
        wg=jnp.pad(gla_w_gate_up[i], ((0, LANES - GLA_GATE_RANK), (0, 0))).astype(BF16),
        bg=_row(gla_b_gate[i]),
        gla_nw=_row(jnp.tile(gla_norm_w[i], GLA_HEADS)),
        conv_w=conv_w[i], conv_b=_row(conv_b[i]), conv_g=_row(conv_ln_g[i]), conv_beta=_row(conv_ln_b[i]),
        lam=lam, dnw=_row(diff_norm_w[i]),
        w_out=w_out[i].astype(BF16), ln1_g=_row(ln1_g[i]), ln1_b=_row(ln1_b[i]),
        w_ffn_gate=w_ffn_gate[i].astype(BF16), w_ffn_up=w_ffn_up[i].astype(BF16),
        w_ffn_down=w_ffn_down[i].astype(BF16), w_ple_gate=w_ple_gate[i].astype(BF16),
        w_ple_proj=w_ple_proj[i].astype(BF16), ln2_g=_row(ln2_g[i]), ln2_b=_row(ln2_b[i]),
    )


def kernel(x_prompt, x_sample, cache_k, cache_v, state_gla, state_conv, page_table, p_prompt, p_sample, w_in, gla_w_gate_up, gla_b_gate, gla_norm_w, conv_w, conv_b, conv_ln_g, conv_ln_b, diff_lq1, diff_lk1, diff_lq2, diff_lk2, diff_norm_w, rel_bias, w_out, ln1_g, ln1_b, w_ffn_gate, w_ffn_up, w_ffn_down, w_ple_gate, w_ple_proj, ln2_g, ln2_b):
    nbp, s, d = x_prompt.shape
    nbs, ts, _ = x_sample.shape
    depth, n_pool = cache_k.shape[:2]
    assert depth == DEPTH and cache_k.shape[2] == PAGE_SIZE
    cache_kt = jnp.transpose(cache_k, (0, 1, 3, 4, 2)).reshape(depth, n_pool, DIFF_VW, PAGE_SIZE)
    cache_v2 = cache_v.reshape(depth, n_pool, PAGE_SIZE * DIFF_HEADS, 2 * DIFF_DH)
    bias = _bias_tiles(rel_bias, ATTN_TILE)

    pe_prompt = p_prompt.reshape(depth * nbp * s, -1)
    pe_sample = p_sample.reshape(depth * nbs * ts, -1)
    yp = x_prompt.reshape(nbp * s, d)
    ys = x_sample.reshape(nbs * ts, d)
    zero_state = jnp.zeros((nbp, GLA_HEADS, GLA_DK, GLA_DV), F32)
    zero_buf = jnp.zeros((nbp, CONV_WIDTH - 1, CONV_CH), F32)
    outs = [[] for _ in range(8)]
    for i in range(depth):
        lw = _layer_weights(i, w_in, gla_w_gate_up, gla_b_gate, gla_norm_w, conv_w, conv_b, conv_ln_g, conv_ln_b,
                            diff_lq1, diff_lk1, diff_lq2, diff_lk2, diff_norm_w, w_out, ln1_g, ln1_b, w_ffn_gate,
                            w_ffn_up, w_ffn_down, w_ple_gate, w_ple_proj, ln2_g, ln2_b)

        def attend_prompt(q16, kt16, v16, lam, dnw, scale):
            return _attn_prompt(q16, kt16, v16, bias, lam, dnw, nbp, s, scale)

        def attend_sample(dq, dk, dv, lam, dnw, scale, layer=i):
            return _attn_sample(dq, dk, dv, cache_kt, cache_v2, page_table, layer, bias, lam, dnw, nbs, ts, scale)

        yp, kt_i, v_i, s_i, c_i = _layer(yp, pe_prompt, nbp, s, lw, i, zero_state, zero_buf, attend_prompt, True)
        outs[0].append(kt_i.reshape(nbp, 2 * DIFF_HEADS, DIFF_DH, s).transpose(0, 3, 1, 2))
        outs[1].append(v_i.reshape(nbp, s, DIFF_HEADS, 2 * DIFF_DH))
        outs[2].append(s_i)
        outs[3].append(c_i)
        ys, k_i, v_i, s_i, c_i = _layer(ys, pe_sample, nbs, ts, lw, i, state_gla[i], state_conv[i], attend_sample,
                                        False)
        outs[4].append(k_i.reshape(nbs, ts, 2 * DIFF_HEADS, DIFF_DH))
        outs[5].append(v_i.reshape(nbs, ts, DIFF_HEADS, 2 * DIFF_DH))
        outs[6].append(s_i)
        outs[7].append(c_i)
    return (yp.reshape(nbp, s, d), ys.reshape(nbs, ts, d)) + tuple(jnp.stack(o) for o in outs)
```

```python
import functools
import math

import jax
import jax.numpy as jnp
from jax import lax
from jax.experimental import pallas as pl
from jax.experimental.pallas import tpu as pltpu

F32 = jnp.float32
BF16 = jnp.bfloat16

GLA_HEADS = 4
GLA_DK = 32
GLA_DV = 64
GLA_QK = GLA_HEADS * GLA_DK
GLA_VW = GLA_HEADS * GLA_DV
GLA_GATE_RANK = 16
GLA_TAU = 16.0
CONV_CH = 256
CONV_WIDTH = 31
DIFF_HEADS = 4
DIFF_DH = 64
DIFF_VW = 2 * DIFF_HEADS * DIFF_DH
REL_BUCKETS = 32
REL_MAX_DIST = 128
PAGE_SIZE = 128
DEPTH = 2
ALPHA = (2 * DEPTH) ** 0.25
EPS = 1e-5
LOG2E = math.log2(math.e)

LANES = 128
SUBLANES = 8
GLA_BLOCK = 16
ATTN_TILE = 256
PAGES_PER_STEP = 16
CONV_HALO = 32
VMEM_LIMIT = 56 * 1024 * 1024

_IN_OFF = {}
_o = 0
for _name, _n in (("gq", 128), ("gk", 128), ("gv", 256), ("gg", 256), ("glr", 128), ("ca", 256), ("cg", 256),
                  ("dq", 512), ("dk", 512), ("dv", 512)):
    _IN_OFF[_name] = (_o, _n)
    _o += _n
N_IN_PAD = _o


def _params(*sem):
    return pltpu.CompilerParams(dimension_semantics=sem, vmem_limit_bytes=VMEM_LIMIT)


def _const_spec(shape):
    nd = len(shape)
    return pl.BlockSpec(shape, lambda *_: (0,) * nd, pipeline_mode=pl.Buffered(1))


def _nt(a, b):
    return lax.dot_general(a, b, (((1,), (1,)), ((), ())), preferred_element_type=F32)


def _mm(a, b):
    return jnp.dot(a, b, preferred_element_type=F32)


def _mm_split(a, b):
    hi = a.astype(BF16)
    lo = (a - hi.astype(F32)).astype(BF16)
    return _mm(hi, b) + _mm(lo, b)


def _sigmoid(x):
    return 1.0 / (1.0 + jnp.exp(-x))


def _layer_norm(x, g, b):
    mu = jnp.mean(x, axis=-1, keepdims=True)
    xc = x - mu
    var = jnp.mean(xc * xc, axis=-1, keepdims=True)
    return xc * lax.rsqrt(var + EPS) * g + b


def _fold_lanes(x, op):
    acc = x[:, 0:LANES]
    for c in range(1, x.shape[1] // LANES):
        acc = op(acc, x[:, c * LANES:(c + 1) * LANES])
    return acc


def _inproj_body(x_ref, w_ref, wkt_ref, wg_ref, bg_ref, gq_ref, gk_ref, gv_ref, gs_ref, lg_ref, u_ref, dq_ref,
                 dv_ref, *k_refs, transposed_k):
    x = x_ref[...].astype(BF16)

    def proj(name):
        lo, n = _IN_OFF[name]
        return _mm(x, w_ref[:, lo:lo + n])

    gq_ref[...] = proj("gq") * GLA_DK ** -0.5
    gk_ref[...] = proj("gk")
    gv_ref[...] = proj("gv")
    gg = proj("gg")
    gs_ref[...] = gg * _sigmoid(gg)
    z = _mm(proj("glr").astype(BF16), wg_ref[...]) + bg_ref[...]
    lg_ref[...] = (jnp.minimum(z, 0.0) - jnp.log(1.0 + jnp.exp(-jnp.abs(z)))) * (1.0 / GLA_TAU)
    u_ref[...] = proj("ca") * _sigmoid(proj("cg"))
    dq_ref[...] = (proj("dq") * (DIFF_DH ** -0.5 * LOG2E)).astype(dq_ref.dtype)
    dv = proj("dv")
    if transposed_k:
        kt_ref, kt16_ref, v16_ref = k_refs
        tm = x.shape[0]
        for h in range(DIFF_HEADS):
            dv_ref[pl.ds(h, tm, stride=DIFF_HEADS), :] = dv[:, h * LANES:(h + 1) * LANES]
        kt = _nt(wkt_ref[...], x)
        kt_ref[0] = kt
        kt16_ref[0] = kt.astype(BF16)
        v16_ref[...] = dv.astype(BF16)
    else:
        dv_ref[...] = dv
        k_refs[0][...] = proj("dk")


def _in_proj(x2, w_in_p, wk_t, wg_p, bg, nb, transposed_k):
    m, d = x2.shape
    t = m // nb
    tm = min(512, t if transposed_k else m)
    nj = t // tm
    widths = (GLA_QK, GLA_QK, GLA_VW, GLA_VW, GLA_QK, CONV_CH, DIFF_VW)
    dtypes = [F32] * 6 + [BF16 if transposed_k else F32]
    out_specs = [pl.BlockSpec((tm, n), lambda i: (i, 0)) for n in widths]
    out_shape = [jax.ShapeDtypeStruct((m, n), dt) for n, dt in zip(widths, dtypes)]
    if transposed_k:
        out_specs.append(pl.BlockSpec((tm * DIFF_HEADS, LANES), lambda i: (i, 0)))
        out_shape.append(jax.ShapeDtypeStruct((m * DIFF_HEADS, LANES), F32))
        kt_spec = pl.BlockSpec((1, DIFF_VW, tm), lambda i: (i // nj, 0, i % nj))
        out_specs += [kt_spec, kt_spec, pl.BlockSpec((tm, DIFF_VW), lambda i: (i, 0))]
        out_shape += [jax.ShapeDtypeStruct((nb, DIFF_VW, t), F32), jax.ShapeDtypeStruct((nb, DIFF_VW, t), BF16),
                      jax.ShapeDtypeStruct((m, DIFF_VW), BF16)]
    else:
        out_specs += [pl.BlockSpec((tm, DIFF_VW), lambda i: (i, 0))] * 2
        out_shape += [jax.ShapeDtypeStruct((m, DIFF_VW), F32)] * 2
    return pl.pallas_call(
        functools.partial(_inproj_body, transposed_k=transposed_k),
        grid=(m // tm,),
        in_specs=[pl.BlockSpec((tm, d), lambda i: (i, 0)),
                  _const_spec(w_in_p.shape), _const_spec(wk_t.shape), _const_spec(wg_p.shape), _const_spec(bg.shape)],
        out_specs=out_specs,
        out_shape=out_shape,
        compiler_params=_params("parallel"),
        name="in_proj",
    )(x2, w_in_p, wk_t, wg_p, bg)


def _gla_body(q_ref, k_ref, v_ref, lg_ref, gs_ref, nw_ref, s0_ref, o_ref, sout_ref, s_scr, *, cb):
    j = pl.program_id(1)

    @pl.when(j == 0)
    def _():
        s_scr[...] = s0_ref[0]

    q = q_ref[...]
    k = k_ref[...]
    v = v_ref[...]
    r = q.shape[0]
    row = lax.broadcasted_iota(jnp.int32, (r, 1), 0) % cb

    b = lg_ref[...]
    s = 1
    while s < cb:
        b = b + jnp.where(row >= s, pltpu.roll(b, s, 0), 0.0)
        s *= 2

    same_head = (lax.broadcasted_iota(jnp.int32, (GLA_QK, GLA_VW), 0) // GLA_DK
                 == lax.broadcasted_iota(jnp.int32, (GLA_QK, GLA_VW), 1) // GLA_DV)
    expand = same_head.astype(BF16)
    mask_t = (lax.broadcasted_iota(jnp.int32, (GLA_VW, GLA_QK), 0) // GLA_DV
              == lax.broadcasted_iota(jnp.int32, (GLA_VW, GLA_QK), 1) // GLA_DK).astype(F32)

    o = _mm((q * k).astype(BF16), expand) * v
    for delta in range(1, cb):
        ks = pltpu.roll(k, delta, 0)
        bs = pltpu.roll(b, delta, 0)
        vs = pltpu.roll(v, delta, 0)
        p = jnp.where(row >= delta, q * ks * jnp.exp(b - bs), 0.0)
        o = o + _mm(p.astype(BF16), expand) * vs

    st = s_scr[...]
    inter = []
    for t in range(r // cb):
        sl = slice(t * cb, (t + 1) * cb)
        bt = b[sl]
        bend = bt[cb - 1:cb]
        qe = (q[sl] * jnp.exp(bt)).astype(BF16)
        inter.append(_nt(qe, st.astype(BF16)))
        ke = k[sl] * jnp.exp(bend - bt)
        vt = v[sl]
        if cb < 16:
            ke = jnp.concatenate([ke, jnp.zeros((16 - cb, GLA_QK), F32)], axis=0)
            vt = jnp.concatenate([vt, jnp.zeros((16 - cb, GLA_VW), F32)], axis=0)
        kv = lax.dot_general(vt.astype(BF16), ke.astype(BF16), (((0,), (0,)), ((), ())),
                             preferred_element_type=F32)
        st = st * jnp.exp(bend) + kv * mask_t
    s_scr[...] = st
    o = o + (inter[0] if len(inter) == 1 else jnp.concatenate(inter, axis=0))

    grp = (lax.broadcasted_iota(jnp.int32, (GLA_VW, GLA_VW), 0) // GLA_DV
           == lax.broadcasted_iota(jnp.int32, (GLA_VW, GLA_VW), 1) // GLA_DV)
    mean_sq = _mm_split(o * o, jnp.where(grp, 1.0 / GLA_DV, 0.0).astype(BF16))
    o_ref[...] = o * lax.rsqrt(mean_sq + EPS) * nw_ref[...] * gs_ref[...]

    @pl.when(j == pl.num_programs(1) - 1)
    def _():
        sout_ref[0] = st


def _gla(gq, gk, gv, lg, gs, nw, s0_t, nb, t):
    cb = GLA_BLOCK if t % GLA_BLOCK == 0 else t
    r = min(t, 256)
    nj = t // r
    m = nb * t

    def rows(n):
        return pl.BlockSpec((r, n), lambda b, j: (b * nj + j, 0))

    return pl.pallas_call(
        functools.partial(_gla_body, cb=cb),
        grid=(nb, nj),
        in_specs=[rows(GLA_QK), rows(GLA_QK), rows(GLA_VW), rows(GLA_QK), rows(GLA_VW),
                  _const_spec(nw.shape),
                  pl.BlockSpec((1, GLA_VW, GLA_QK), lambda b, j: (b, 0, 0))],
        out_specs=[rows(GLA_VW), pl.BlockSpec((1, GLA_VW, GLA_QK), lambda b, j: (b, 0, 0))],
        out_shape=[jax.ShapeDtypeStruct((m, GLA_VW), F32), jax.ShapeDtypeStruct((nb, GLA_VW, GLA_QK), F32)],
        scratch_shapes=[pltpu.VMEM((GLA_VW, GLA_QK), F32)],
        compiler_params=_params("parallel", "arbitrary"),
        name="gla",
    )(gq, gk, gv, lg, gs, nw, s0_t)


def _conv_body(u_ref, buf_ref, cw_ref, cb_ref, g_ref, beta_ref, y_ref, nbuf_ref, seq_scr):
    j = pl.program_id(1)
    r = u_ref.shape[0]
    hist = CONV_WIDTH - 1

    @pl.when(j == 0)
    def _():
        seq_scr[0:CONV_HALO - hist, :] = jnp.zeros((CONV_HALO - hist, CONV_CH), F32)
        seq_scr[CONV_HALO - hist:CONV_HALO, :] = buf_ref[0]

    @pl.when(j > 0)
    def _():
        seq_scr[0:CONV_HALO, :] = seq_scr[r:r + CONV_HALO, :]

    seq_scr[CONV_HALO:CONV_HALO + r, :] = u_ref[...]

    window = seq_scr[...]
    rows = window.shape[0]
    acc = jnp.zeros((r, CONV_CH), F32) + cb_ref[...]
    for rho in range(SUBLANES):
        lo = CONV_HALO - hist + rho
        shifted = pltpu.roll(window, rows - lo, 0)
        for k in range(-(-CONV_WIDTH // SUBLANES)):
            w = SUBLANES * k + rho
            if w < CONV_WIDTH:
                acc = acc + shifted[SUBLANES * k:SUBLANES * k + r, :] * cw_ref[w:w + 1, :]
    y = _layer_norm(acc, g_ref[...], beta_ref[...])
    y_ref[...] = y * _sigmoid(y)

    @pl.when(j == pl.num_programs(1) - 1)
    def _():
        nbuf_ref[0] = seq_scr[r + CONV_HALO - hist:r + CONV_HALO, :]


def _conv(u, buf, cw, cb, g, beta, nb, t):
    r = min(t, 512)
    nj = t // r
    assert nj == 1 or r >= CONV_HALO
    hist = CONV_WIDTH - 1
    return pl.pallas_call(
        _conv_body,
        grid=(nb, nj),
        in_specs=[pl.BlockSpec((r, CONV_CH), lambda b, j: (b * nj + j, 0)),
                  pl.BlockSpec((1, hist, CONV_CH), lambda b, j: (b, 0, 0)),
                  _const_spec(cw.shape), _const_spec(cb.shape), _const_spec(g.shape), _const_spec(beta.shape)],
        out_specs=[pl.BlockSpec((r, CONV_CH), lambda b, j: (b * nj + j, 0)),
                   pl.BlockSpec((1, hist, CONV_CH), lambda b, j: (b, 0, 0))],
        out_shape=[jax.ShapeDtypeStruct((nb * t, CONV_CH), F32), jax.ShapeDtypeStruct((nb, hist, CONV_CH), F32)],
        scratch_shapes=[pltpu.VMEM((r + CONV_HALO, CONV_CH), F32)],
        compiler_params=_params("parallel", "arbitrary"),
        name="conv",
    )(u, buf, cw, cb, g, beta)


def _rel_bucket(dist):
    n = jnp.maximum(dist, 0)
    max_exact = REL_BUCKETS // 2
    nf = jnp.maximum(n, 1).astype(F32)
    large = max_exact + (jnp.log(nf / max_exact) / math.log(REL_MAX_DIST / max_exact)
                         * (REL_BUCKETS - max_exact)).astype(jnp.int32)
    large = jnp.minimum(large, REL_BUCKETS - 1)
    return jnp.where(n < max_exact, n, large)


def _bias_body(rb_ref, idx_ref, o_ref):
    h = pl.program_id(1)
    idx = idx_ref[0]
    acc = jnp.full(idx.shape, -jnp.inf, F32)
    for bucket in range(REL_BUCKETS):
        acc = jnp.where(idx == bucket, rb_ref[bucket, h], acc)
    o_ref[0, 0] = (acc - rb_ref[REL_BUCKETS - 1, h]) * LOG2E


def _bias_tiles(rel_bias, t):
    assert t >= REL_MAX_DIST
    ii = jnp.arange(t, dtype=jnp.int32)[:, None]
    jj = jnp.arange(t, dtype=jnp.int32)[None, :]
    idx = jnp.stack([jnp.where(ii >= jj, _rel_bucket(ii - jj), -1), _rel_bucket(t + ii - jj)])
    nh = rel_bias.shape[1]
    return pl.pallas_call(
        _bias_body,
        grid=(2, nh),
        in_specs=[pl.BlockSpec(memory_space=pltpu.SMEM),
                  pl.BlockSpec((1, t, t), lambda r, h: (r, 0, 0))],
        out_specs=pl.BlockSpec((1, 1, t, t), lambda r, h: (r, h, 0, 0)),
        out_shape=jax.ShapeDtypeStruct((2, nh, t, t), F32),
        compiler_params=_params("arbitrary", "arbitrary"),
        name="rel_bias_tiles",
    )(rel_bias, idx)


def _attn_body(lam_ref, q_ref, kt_ref, v_ref, bias_ref, dnw_ref, o_ref, q_scr, s_scr, m_scr, l_scr, acc_scr, *,
               out_scale):
    i = pl.program_id(1)
    t = q_ref.shape[0]
    nmaps = 2 * DIFF_HEADS
    lane = lax.broadcasted_iota(jnp.int32, (t, LANES), 1)
    for n in range(nmaps):
        qh = q_ref[:, (n // 2) * LANES:(n // 2 + 1) * LANES]
        keep = (lane < DIFF_DH) if n % 2 == 0 else (lane >= DIFF_DH)
        q_scr[n] = jnp.where(keep, qh, jnp.zeros_like(qh))

    def cols(j):
        return pl.ds(pl.multiple_of(j * t, t), t)

    def logits_tile(j, which):
        for n in range(nmaps):
            kt = kt_ref[0, (n // 2) * LANES:(n // 2 + 1) * LANES, cols(j)]
            s = _mm(q_scr[n], kt)
            if which is not None:
                s = s + bias_ref[which, n]
            s_scr[n, :, cols(j)] = s
            m_scr[n] = jnp.maximum(m_scr[n], _fold_lanes(s, jnp.maximum))

    m_scr[...] = jnp.full(m_scr.shape, -jnp.inf, F32)

    def far_tile(j, carry):
        logits_tile(j, None)
        return carry

    lax.fori_loop(0, jnp.maximum(i - 1, 0), far_tile, 0)

    @pl.when(i >= 1)
    def _():
        logits_tile(i - 1, 1)

    logits_tile(i, 0)

    for n in range(nmaps):
        m_scr[n] = jnp.broadcast_to(jnp.max(m_scr[n], axis=-1, keepdims=True), (t, LANES))
    l_scr[...] = jnp.zeros(l_scr.shape, F32)
    acc_scr[...] = jnp.zeros(acc_scr.shape, F32)

    def pv_tile(j, carry):
        for n in range(nmaps):
            vb = v_ref[cols(j), (n // 2) * LANES:(n // 2 + 1) * LANES]
            mb = m_scr[n]
            ps = [jnp.exp2(s_scr[n, :, pl.ds(pl.multiple_of(j * t + c * LANES, LANES), LANES)] - mb)
                  for c in range(t // LANES)]
            part = ps[0]
            for pc in ps[1:]:
                part = part + pc
            l_scr[n] += part
            acc_scr[n] += _mm(jnp.concatenate(ps, axis=1).astype(BF16), vb)
        return carry

    lax.fori_loop(0, i + 1, pv_tile, 0)

    lam = lam_ref[0]
    outs = []
    for hp in range(DIFF_HEADS):
        l0 = jnp.sum(l_scr[2 * hp], axis=-1, keepdims=True)
        l1 = jnp.sum(l_scr[2 * hp + 1], axis=-1, keepdims=True)
        o = acc_scr[2 * hp] / l0 - lam * (acc_scr[2 * hp + 1] / l1)
        ms_o = jnp.mean(o * o, axis=-1, keepdims=True)
        outs.append(o * lax.rsqrt(ms_o + EPS) * dnw_ref[...] * out_scale)
    o_ref[...] = jnp.concatenate(outs, axis=1)


def _attn_prompt(q16, kt16, v16, bias, lam, dnw, nb, s, out_scale):
    t = bias.shape[-1]
    nq = s // t
    nmaps = 2 * DIFF_HEADS
    return pl.pallas_call(
        functools.partial(_attn_body, out_scale=out_scale),
        grid=(nb, nq),
        in_specs=[pl.BlockSpec(memory_space=pltpu.SMEM),
                  pl.BlockSpec((t, DIFF_VW), lambda b, i: (b * nq + i, 0)),
                  pl.BlockSpec((1, DIFF_VW, s), lambda b, i: (b, 0, 0)),
                  pl.BlockSpec((s, DIFF_VW), lambda b, i: (b, 0)),
                  _const_spec(bias.shape), _const_spec(dnw.shape)],
        out_specs=pl.BlockSpec((t, DIFF_VW), lambda b, i: (b * nq + i, 0)),
        out_shape=jax.ShapeDtypeStruct((nb * s, DIFF_VW), F32),
        scratch_shapes=[pltpu.VMEM((nmaps, t, LANES), BF16), pltpu.VMEM((nmaps, t, s), F32),
                        pltpu.VMEM((nmaps, t, LANES), F32), pltpu.VMEM((nmaps, t, LANES), F32),
                        pltpu.VMEM((nmaps, t, LANES), F32)],
        compiler_params=_params("parallel", "arbitrary"),
        name="diff_attn_prompt",
    )(lam, q16, kt16, v16, bias, dnw)


def _sattn_body(pt_ref, lam_ref, q_ref, kn_ref, vn_ref, d1_ref, d0_ref, dnw_ref, *rest, npg, out_scale):
    del pt_ref
    kp = rest[:npg]
    vp = rest[npg:2 * npg]
    o_ref = rest[2 * npg]
    m_scr, l_scr, acc_scr = rest[2 * npg + 1:]
    p_id = pl.program_id(1)
    last = pl.num_programs(1) - 1
    nh = 2 * DIFF_HEADS
    tq = q_ref.shape[0]
    rows_h = 2 * tq

    q = q_ref[...]
    lane_head = lax.broadcasted_iota(jnp.int32, q.shape, 1) // DIFF_DH
    qs = jnp.concatenate([jnp.where(lane_head == hh, q, 0.0) for hh in range(nh)], axis=0).astype(BF16)

    @pl.when(p_id == 0)
    def _():
        m_scr[...] = jnp.full(m_scr.shape, -jnp.inf, F32)
        l_scr[...] = jnp.zeros(l_scr.shape, F32)
        acc_scr[...] = jnp.zeros(acc_scr.shape, F32)

    def update(s, values_of):
        m_old = m_scr[...]
        m_new = jnp.maximum(m_old, jnp.max(s, axis=-1, keepdims=True))
        alpha = jnp.exp2(m_old - m_new)
        p = jnp.exp2(s - m_new)
        l_scr[...] = alpha * l_scr[...] + jnp.sum(p, axis=-1, keepdims=True)
        pvs = []
        for h in range(DIFF_HEADS):
            ph = p[h * rows_h:(h + 1) * rows_h].astype(BF16)
            pv = _mm(ph[:, 0:PAGE_SIZE], values_of(0, h))
            for g in range(1, s.shape[1] // PAGE_SIZE):
                pv = pv + _mm(ph[:, g * PAGE_SIZE:(g + 1) * PAGE_SIZE], values_of(g, h))
            pvs.append(pv)
        acc_scr[...] = alpha * acc_scr[...] + jnp.concatenate(pvs, axis=0)
        m_scr[...] = m_new

    ss = []
    for g in range(npg):
        s = _mm(qs, kp[g][...].astype(BF16))
        if g == npg - 1:
            s = s + jnp.where(p_id == last, d1_ref[...], 0.0)
        ss.append(s)
    update(jnp.concatenate(ss, axis=1),
           lambda g, h: vp[g][pl.ds(h, PAGE_SIZE, stride=DIFF_HEADS), :].astype(BF16))

    @pl.when(p_id == last)
    def _():
        pad = jnp.zeros((PAGE_SIZE - tq, DIFF_VW), F32)
        kn = jnp.concatenate([kn_ref[...], pad], axis=0).astype(BF16)
        vn = jnp.concatenate([vn_ref[...], pad], axis=0).astype(BF16)
        update(_nt(qs, kn) + d0_ref[...], lambda g, h: vn[:, h * LANES:(h + 1) * LANES])
        o = acc_scr[...] / l_scr[...]
        lam = lam_ref[0]
        outs = []
        for h in range(DIFF_HEADS):
            oh = o[h * rows_h:h * rows_h + tq] - lam * o[h * rows_h + tq:(h + 1) * rows_h]
            ms = jnp.mean(oh * oh, axis=-1, keepdims=True)
            outs.append(oh * lax.rsqrt(ms + EPS) * dnw_ref[...] * out_scale)
        o_ref[...] = jnp.concatenate(outs, axis=1)


def _attn_sample(dq, dk, dv, cache_kt, cache_v2, page_table, layer, bias, lam, dnw, nb, tq, out_scale):
    n_pages = page_table.shape[1]
    npg = PAGES_PER_STEP
    assert n_pages % npg == 0 and cache_kt.shape[-1] == PAGE_SIZE and tq == 8
    nh = 2 * DIFF_HEADS
    t = bias.shape[-1]
    d1 = bias[1, :, 0:tq, t - PAGE_SIZE:t].reshape(nh * tq, PAGE_SIZE)
    d0 = jnp.concatenate([bias[0, :, 0:tq, 0:tq], jnp.full((nh, tq, PAGE_SIZE - tq), -jnp.inf, F32)],
                         axis=-1).reshape(nh * tq, PAGE_SIZE)

    def page_spec(g):
        return pl.BlockSpec((None, None, DIFF_VW, PAGE_SIZE),
                            lambda b, p, pt: (layer, pt[b * n_pages + p * npg + g], 0, 0))

    def rows_spec():
        return pl.BlockSpec((tq, DIFF_VW), lambda b, p, pt: (b, 0))

    def full_spec(a):
        nd = a.ndim
        return pl.BlockSpec(a.shape, lambda b, p, pt: (0,) * nd)

    grid_spec = pltpu.PrefetchScalarGridSpec(
        num_scalar_prefetch=1,
        grid=(nb, n_pages // npg),
        in_specs=[pl.BlockSpec(memory_space=pltpu.SMEM), rows_spec(), rows_spec(), rows_spec(),
                  full_spec(d1), full_spec(d0), full_spec(dnw)]
                 + [page_spec(g) for g in range(npg)] + [page_spec(g) for g in range(npg)],
        out_specs=rows_spec(),
        scratch_shapes=[pltpu.VMEM((nh * tq, 1), F32), pltpu.VMEM((nh * tq, 1), F32),
                        pltpu.VMEM((nh * tq, LANES), F32)],
    )
    return pl.pallas_call(
        functools.partial(_sattn_body, npg=npg, out_scale=out_scale),
        grid_spec=grid_spec,
        out_shape=jax.ShapeDtypeStruct((nb * tq, DIFF_VW), F32),
        compiler_params=_params("parallel", "arbitrary"),
        name="diff_attn_sample",
    )(page_table.reshape(-1), lam, dq, dk, dv, d1, d0, dnw, *([cache_kt] * npg), *([cache_v2] * npg))


def _mix_ffn_body(oa_ref, ob_ref, oc_ref, x_ref, pe_ref, wo_ref, g1_ref, b1_ref, wg_ref, wu_ref, wd_ref, wpg_ref,
                  wpp_ref, g2_ref, b2_ref, y_ref, *, chunk):
    mix = _mm(oa_ref[...].astype(BF16), wo_ref[0:GLA_VW, :])
    mix = mix + _mm(ob_ref[...].astype(BF16), wo_ref[GLA_VW:GLA_VW + CONV_CH, :])
    mix = mix + _mm(oc_ref[...].astype(BF16), wo_ref[GLA_VW + CONV_CH:, :])
    x = _layer_norm(ALPHA * x_ref[...] + mix, g1_ref[...], b1_ref[...])
    xb = x.astype(BF16)
    ple = _sigmoid(_mm(xb, wpg_ref[...])) * _mm(pe_ref[...].astype(BF16), wpp_ref[...])
    acc = ALPHA * x + ple
    for c in range(wg_ref.shape[1] // chunk):
        cols = slice(c * chunk, (c + 1) * chunk)
        gate = _mm(xb, wg_ref[:, cols])
        hid = gate * _sigmoid(gate) * _mm(xb, wu_ref[:, cols])
        acc = acc + _mm(hid.astype(BF16), wd_ref[cols, :])
    y_ref[...] = _layer_norm(acc, g2_ref[...], b2_ref[...])


def _mix_ffn(oa, ob, oc, x2, pe_all, layer, lw):
    m, d = x2.shape
    tm = min(512, m)
    nt = m // tm
    consts = [lw[n] for n in ("w_out", "ln1_g", "ln1_b", "w_ffn_gate", "w_ffn_up", "w_ffn_down", "w_ple_gate",
                              "w_ple_proj", "ln2_g", "ln2_b")]

    def rows(n):
        return pl.BlockSpec((tm, n), lambda i: (i, 0))

    return pl.pallas_call(
        functools.partial(_mix_ffn_body, chunk=256),
        grid=(nt,),
        in_specs=[rows(GLA_VW), rows(CONV_CH), rows(DIFF_VW), rows(d),
                  pl.BlockSpec((tm, pe_all.shape[1]), lambda i: (layer * nt + i, 0))]
                 + [_const_spec(a.shape) for a in consts],
        out_specs=rows(d),
        out_shape=jax.ShapeDtypeStruct((m, d), F32),
        compiler_params=_params("parallel"),
        name="mix_ffn",
    )(oa, ob, oc, x2, pe_all, *consts)


def _row(a):
    return a.reshape(1, -1)


def _state_to_t(s):
    nb = s.shape[0]
    eye = jnp.eye(GLA_HEADS, dtype=s.dtype)
    return jnp.einsum("bhde,hg->bhegd", s, eye).reshape(nb, GLA_VW, GLA_QK)


def _state_from_t(s_t):
    nb = s_t.shape[0]
    blocks = s_t.reshape(nb, GLA_HEADS, GLA_DV, GLA_HEADS, GLA_DK)
    diag = jnp.stack([blocks[:, h, :, h, :] for h in range(GLA_HEADS)], axis=1)
    return diag.transpose(0, 1, 3, 2)


def _layer(x2, pe_all, nb, t, lw, layer, s0, buf, attend, transposed_k):
    gq, gk, gv, gs, lg, u, dq, dv, *kk = _in_proj(x2, lw["w_in"], lw["wk_t"], lw["wg"], lw["bg"], nb, transposed_k)
    o_a, s_t = _gla(gq, gk, gv, lg, gs, lw["gla_nw"], _state_to_t(s0), nb, t)
    o_b, nbuf = _conv(u, buf, lw["conv_w"], lw["conv_b"], lw["conv_g"], lw["conv_beta"], nb, t)
    lam_init = 0.8 - 0.6 * math.exp(-0.3 * layer)
    if transposed_k:
        o_c = attend(dq, kk[1], kk[2], lw["lam"], lw["dnw"], 1.0 - lam_init)
    else:
        o_c = attend(dq, kk[0], dv, lw["lam"], lw["dnw"], 1.0 - lam_init)
    y = _mix_ffn(o_a, o_b, o_c, x2, pe_all, layer, lw)
    return y, kk[0], dv, _state_from_t(s_t), nbuf


def _prep_w_in(w):
    sizes = (GLA_QK, GLA_QK, GLA_VW, GLA_VW, GLA_GATE_RANK, CONV_CH, CONV_CH, DIFF_VW, DIFF_VW, DIFF_VW)
    parts, s = [], 0
    for n in sizes:
        parts.append(w[:, s:s + n])
        s += n
    wk_t = parts[8].T.astype(BF16)
    parts[4] = jnp.pad(parts[4], ((0, 0), (0, LANES - GLA_GATE_RANK)))
    return jnp.concatenate(parts, axis=1).astype(BF16), wk_t


def _layer_weights(i, w_in, gla_w_gate_up, gla_b_gate, gla_norm_w, conv_w, conv_b, conv_ln_g, conv_ln_b, diff_lq1,
                   diff_lk1, diff_lq2, diff_lk2, diff_norm_w, w_out, ln1_g, ln1_b, w_ffn_gate, w_ffn_up, w_ffn_down,
                   w_ple_gate, w_ple_proj, ln2_g, ln2_b):
    lam_init = 0.8 - 0.6 * math.exp(-0.3 * i)
    lam = (jnp.exp(jnp.sum(diff_lq1[i] * diff_lk1[i])) - jnp.exp(jnp.sum(diff_lq2[i] * diff_lk2[i]))
           + lam_init).reshape(1).astype(F32)
    w_in_p, wk_t = _prep_w_in(w_in[i])
    return dict(
        w_in=w_in_p, wk_t=wk_t,
        wg=jnp.pad(gla_w_gate_up[i], ((0, LANES - GLA_GATE_RANK), (0, 0))).astype(BF16),
        bg=_row(gla_b_gate[i]),
        gla_nw=_row(jnp.tile(gla_norm_w[i], GLA_HEADS)),
        conv_w=conv_w[i], conv_b=_row(conv_b[i]), conv_g=_row(conv_ln_g[i]), conv_beta=_row(conv_ln_b[i]),
        lam=lam, dnw=_row(diff_norm_w[i]),
        w_out=w_out[i].astype(BF16), ln1_g=_row(ln1_g[i]), ln1_b=_row(ln1_b[i]),
        w_ffn_gate=w_ffn_gate[i].astype(BF16), w_ffn_up=w_ffn_up[i].astype(BF16),
        w_ffn_down=w_ffn_down[i].astype(BF16), w_ple_gate=w_ple_gate[i].astype(BF16),
        w_ple_proj=w_ple_proj[i].astype(BF16), ln2_g=_row(ln2_g[i]), ln2_b=_row(ln2_b[i]),
    )


def kernel(x_prompt, x_sample, cache_k, cache_v, state_gla, state_conv, page_table, p_prompt, p_sample, w_in, gla_w_gate_up, gla_b_gate, gla_norm_w, conv_w, conv_b, conv_ln_g, conv_ln_b, diff_lq1, diff_lk1, diff_lq2, diff_lk2, diff_norm_w, rel_bias, w_out, ln1_g, ln1_b, w_ffn_gate, w_ffn_up, w_ffn_down, w_ple_gate, w_ple_proj, ln2_g, ln2_b):
    nbp, s, d = x_prompt.shape
    nbs, ts, _ = x_sample.shape
    depth, n_pool = cache_k.shape[:2]
    assert depth == DEPTH and cache_k.shape[2] == PAGE_SIZE
    cache_kt = jnp.transpose(cache_k, (0, 1, 3, 4, 2)).reshape(depth, n_pool, DIFF_VW, PAGE_SIZE)
    cache_v2 = cache_v.reshape(depth, n_pool, PAGE_SIZE * DIFF_HEADS, 2 * DIFF_DH)
    bias = _bias_tiles(rel_bias, ATTN_TILE)

    pe_prompt = p_prompt.reshape(depth * nbp * s, -1)
    pe_sample = p_sample.reshape(depth * nbs * ts, -1)
    yp = x_prompt.reshape(nbp * s, d)
    ys = x_sample.reshape(nbs * ts, d)
    zero_state = jnp.zeros((nbp, GLA_HEADS, GLA_DK, GLA_DV), F32)
    zero_buf = jnp.zeros((nbp, CONV_WIDTH - 1, CONV_CH), F32)
    outs = [[] for _ in range(8)]
    for i in range(depth):
        lw = _layer_weights(i, w_in, gla_w_gate_up, gla_b_gate, gla_norm_w, conv_w, conv_b, conv_ln_g, conv_ln_b,
                            diff_lq1, diff_lk1, diff_lq2, diff_lk2, diff_norm_w, w_out, ln1_g, ln1_b, w_ffn_gate,
                            w_ffn_up, w_ffn_down, w_ple_gate, w_ple_proj, ln2_g, ln2_b)

        def attend_prompt(q16, kt16, v16, lam, dnw, scale):
            return _attn_prompt(q16, kt16, v16, bias, lam, dnw, nbp, s, scale)

        def attend_sample(dq, dk, dv, lam, dnw, scale, layer=i):
            return _attn_sample(dq, dk, dv, cache_kt, cache_v2, page_table, layer, bias, lam, dnw, nbs, ts, scale)

        yp, kt_i, v_i, s_i, c_i = _layer(yp, pe_prompt, nbp, s, lw, i, zero_state, zero_buf, attend_prompt, True)
        outs[0].append(kt_i.reshape(nbp, 2 * DIFF_HEADS, DIFF_DH, s).transpose(0, 3, 1, 2))
        outs[1].append(v_i.reshape(nbp, s, DIFF_HEADS, 2 * DIFF_DH))
        outs[2].append(s_i)
        outs[3].append(c_i)
        ys, k_i, v_i, s_i, c_i = _layer(ys, pe_sample, nbs, ts, lw, i, state_gla[i], state_conv[i], attend_sample,
                                        False)
        outs[4].append(k_i.reshape(nbs, ts, 2 * DIFF_HEADS, DIFF_DH))
        outs[5].append(v_i.reshape(nbs, ts, DIFF_HEADS, 2 * DIFF_DH))
        outs[6].append(s_i)
        outs[7].append(c_i)
    return (yp.reshape(nbp, s, d), ys.reshape(nbs, ts, d)) + tuple(jnp.stack(o) for o in outs)
```

```python
import functools
import math

import jax
import jax.numpy as jnp
from jax import lax
from jax.experimental import pallas as pl
from jax.experimental.pallas import tpu as pltpu

F32 = jnp.float32
BF16 = jnp.bfloat16

GLA_HEADS = 4
GLA_DK = 32
GLA_DV = 64
GLA_QK = GLA_HEADS * GLA_DK
GLA_VW = GLA_HEADS * GLA_DV
GLA_GATE_RANK = 16
GLA_TAU = 16.0
CONV_CH = 256
CONV_WIDTH = 31
DIFF_HEADS = 4
DIFF_DH = 64
DIFF_VW = 2 * DIFF_HEADS * DIFF_DH
REL_BUCKETS = 32
REL_MAX_DIST = 128
PAGE_SIZE = 128
DEPTH = 2
ALPHA = (2 * DEPTH) ** 0.25
EPS = 1e-5
LOG2E = math.log2(math.e)

LANES = 128
SUBLANES = 8
GLA_BLOCK = 16
GLA_SUBTILE = 128
ATTN_TILE = 256
MIX_ROWS = 512
FFN_CHUNK = 256
PAGES_PER_STEP = 8
CONV_HALO = 32
VMEM_LIMIT = 56 * 1024 * 1024

_IN_OFF = {}
_o = 0
for _name, _n in (("gq", 128), ("gk", 128), ("gv", 256), ("gg", 256), ("glr", 128), ("ca", 256), ("cg", 256),
                  ("dq", 512), ("dk", 512), ("dv", 512)):
    _IN_OFF[_name] = (_o, _n)
    _o += _n
N_IN_PAD = _o


def _params(*sem):
    return pltpu.CompilerParams(dimension_semantics=sem, vmem_limit_bytes=VMEM_LIMIT)


def _const_spec(shape):
    nd = len(shape)
    return pl.BlockSpec(shape, lambda *_: (0,) * nd, pipeline_mode=pl.Buffered(1))


def _nt(a, b):
    return lax.dot_general(a, b, (((1,), (1,)), ((), ())), preferred_element_type=F32)


def _mm(a, b):
    return jnp.dot(a, b, preferred_element_type=F32)


def _mm_split(a, b):
    hi = a.astype(BF16)
    lo = (a - hi.astype(F32)).astype(BF16)
    return _mm(hi, b) + _mm(lo, b)


def _sigmoid(x):
    return 1.0 / (1.0 + jnp.exp(-x))


def _layer_norm(x, g, b):
    mu = jnp.mean(x, axis=-1, keepdims=True)
    xc = x - mu
    var = jnp.mean(xc * xc, axis=-1, keepdims=True)
    return xc * lax.rsqrt(var + EPS) * g + b


def _fold_lanes(x, op):
    acc = x[:, 0:LANES]
    for c in range(1, x.shape[1] // LANES):
        acc = op(acc, x[:, c * LANES:(c + 1) * LANES])
    return acc


def _inproj_body(x_ref, w_ref, wkt_ref, wg_ref, bg_ref, gq_ref, gk_ref, gv_ref, gs_ref, lg_ref, u_ref, dq_ref,
                 dv_ref, *k_refs, transposed_k):
    x = x_ref[...].astype(BF16)

    def proj(name):
        lo, n = _IN_OFF[name]
        return _mm(x, w_ref[:, lo:lo + n])

    gq_ref[...] = proj("gq") * GLA_DK ** -0.5
    gk_ref[...] = proj("gk")
    gv_ref[...] = proj("gv")
    gg = proj("gg")
    gs_ref[...] = gg * _sigmoid(gg)
    z = _mm(proj("glr").astype(BF16), wg_ref[...]) + bg_ref[...]
    lg_ref[...] = (jnp.minimum(z, 0.0) - jnp.log(1.0 + jnp.exp(-jnp.abs(z)))) * (1.0 / GLA_TAU)
    u_ref[...] = proj("ca") * _sigmoid(proj("cg"))
    dq_ref[...] = (proj("dq") * (DIFF_DH ** -0.5 * LOG2E)).astype(dq_ref.dtype)
    dv = proj("dv")
    if transposed_k:
        kt_ref, kt16_ref, v16_ref = k_refs
        tm = x.shape[0]
        for h in range(DIFF_HEADS):
            dv_ref[pl.ds(h, tm, stride=DIFF_HEADS), :] = dv[:, h * LANES:(h + 1) * LANES]
        kt = _nt(wkt_ref[...], x)
        kt_ref[0] = kt
        kt16_ref[0] = kt.astype(BF16)
        v16_ref[...] = dv.astype(BF16)
    else:
        dv_ref[...] = dv
        k_refs[0][...] = proj("dk")


def _in_proj(x2, w_in_p, wk_t, wg_p, bg, nb, transposed_k):
    m, d = x2.shape
    t = m // nb
    tm = min(512, t if transposed_k else m)
    nj = t // tm
    widths = (GLA_QK, GLA_QK, GLA_VW, GLA_VW, GLA_QK, CONV_CH, DIFF_VW)
    dtypes = [F32] * 6 + [BF16 if transposed_k else F32]
    out_specs = [pl.BlockSpec((tm, n), lambda i: (i, 0)) for n in widths]
    out_shape = [jax.ShapeDtypeStruct((m, n), dt) for n, dt in zip(widths, dtypes)]
    if transposed_k:
        out_specs.append(pl.BlockSpec((tm * DIFF_HEADS, LANES), lambda i: (i, 0)))
        out_shape.append(jax.ShapeDtypeStruct((m * DIFF_HEADS, LANES), F32))
        kt_spec = pl.BlockSpec((1, DIFF_VW, tm), lambda i: (i // nj, 0, i % nj))
        out_specs += [kt_spec, kt_spec, pl.BlockSpec((tm, DIFF_VW), lambda i: (i, 0))]
        out_shape += [jax.ShapeDtypeStruct((nb, DIFF_VW, t), F32), jax.ShapeDtypeStruct((nb, DIFF_VW, t), BF16),
                      jax.ShapeDtypeStruct((m, DIFF_VW), BF16)]
    else:
        out_specs += [pl.BlockSpec((tm, DIFF_VW), lambda i: (i, 0))] * 2
        out_shape += [jax.ShapeDtypeStruct((m, DIFF_VW), F32)] * 2
    return pl.pallas_call(
        functools.partial(_inproj_body, transposed_k=transposed_k),
        grid=(m // tm,),
        in_specs=[pl.BlockSpec((tm, d), lambda i: (i, 0)),
                  _const_spec(w_in_p.shape), _const_spec(wk_t.shape), _const_spec(wg_p.shape), _const_spec(bg.shape)],
        out_specs=out_specs,
        out_shape=out_shape,
        compiler_params=_params("parallel"),
        name="in_proj",
    )(x2, w_in_p, wk_t, wg_p, bg)


def _gla_body(q_ref, k_ref, v_ref, lg_ref, gs_ref, nw_ref, s0_ref, o_ref, sout_ref, s_scr, *, cb):
    j = pl.program_id(1)

    @pl.when(j == 0)
    def _():
        s_scr[...] = s0_ref[0]

    r = q_ref.shape[0]
    rs = min(r, GLA_SUBTILE)
    row = lax.broadcasted_iota(jnp.int32, (rs, 1), 0) % cb

    same_head = (lax.broadcasted_iota(jnp.int32, (GLA_QK, GLA_VW), 0) // GLA_DK
                 == lax.broadcasted_iota(jnp.int32, (GLA_QK, GLA_VW), 1) // GLA_DV)
    expand = same_head.astype(BF16)
    mask_t = (lax.broadcasted_iota(jnp.int32, (GLA_VW, GLA_QK), 0) // GLA_DV
              == lax.broadcasted_iota(jnp.int32, (GLA_VW, GLA_QK), 1) // GLA_DK).astype(F32)
    grp = (lax.broadcasted_iota(jnp.int32, (GLA_VW, GLA_VW), 0) // GLA_DV
           == lax.broadcasted_iota(jnp.int32, (GLA_VW, GLA_VW), 1) // GLA_DV)
    head_mean = jnp.where(grp, 1.0 / GLA_DV, 0.0).astype(BF16)

    st = s_scr[...]
    for sub in range(r // rs):
        rows = pl.ds(sub * rs, rs)
        q = q_ref[rows, :]
        k = k_ref[rows, :]
        v = v_ref[rows, :]

        b = lg_ref[rows, :]
        s = 1
        while s < cb:
            b = b + jnp.where(row >= s, pltpu.roll(b, s, 0), 0.0)
            s *= 2

        o = _mm((q * k).astype(BF16), expand) * v
        for delta in range(1, cb):
            ks = pltpu.roll(k, delta, 0)
            bs = pltpu.roll(b, delta, 0)
            vs = pltpu.roll(v, delta, 0)
            p = jnp.where(row >= delta, q * ks * jnp.exp(b - bs), 0.0)
            o = o + _mm(p.astype(BF16), expand) * vs

        inter = []
        for t in range(rs // cb):
            sl = slice(t * cb, (t + 1) * cb)
            bt = b[sl]
            bend = bt[cb - 1:cb]
            qe = (q[sl] * jnp.exp(bt)).astype(BF16)
            inter.append(_nt(qe, st.astype(BF16)))
            ke = k[sl] * jnp.exp(bend - bt)
            vt = v[sl]
            if cb < 16:
                ke = jnp.concatenate([ke, jnp.zeros((16 - cb, GLA_QK), F32)], axis=0)
                vt = jnp.concatenate([vt, jnp.zeros((16 - cb, GLA_VW), F32)], axis=0)
            kv = lax.dot_general(vt.astype(BF16), ke.astype(BF16), (((0,), (0,)), ((), ())),
                                 preferred_element_type=F32)
            st = st * jnp.exp(bend) + kv * mask_t
        o = o + (inter[0] if len(inter) == 1 else jnp.concatenate(inter, axis=0))

        mean_sq = _mm_split(o * o, head_mean)
        o_ref[rows, :] = o * lax.rsqrt(mean_sq + EPS) * nw_ref[...] * gs_ref[rows, :]
    s_scr[...] = st

    @pl.when(j == pl.num_programs(1) - 1)
    def _():
        sout_ref[0] = st


def _gla(gq, gk, gv, lg, gs, nw, s0_t, nb, t):
    cb = GLA_BLOCK if t % GLA_BLOCK == 0 else t
    r = min(t, 256)
    nj = t // r
    m = nb * t

    def rows(n):
        return pl.BlockSpec((r, n), lambda b, j: (b * nj + j, 0))

    return pl.pallas_call(
        functools.partial(_gla_body, cb=cb),
        grid=(nb, nj),
        in_specs=[rows(GLA_QK), rows(GLA_QK), rows(GLA_VW), rows(GLA_QK), rows(GLA_VW),
                  _const_spec(nw.shape),
                  pl.BlockSpec((1, GLA_VW, GLA_QK), lambda b, j: (b, 0, 0))],
        out_specs=[rows(GLA_VW), pl.BlockSpec((1, GLA_VW, GLA_QK), lambda b, j: (b, 0, 0))],
        out_shape=[jax.ShapeDtypeStruct((m, GLA_VW), F32), jax.ShapeDtypeStruct((nb, GLA_VW, GLA_QK), F32)],
        scratch_shapes=[pltpu.VMEM((GLA_VW, GLA_QK), F32)],
        compiler_params=_params("parallel", "arbitrary"),
        name="gla",
    )(gq, gk, gv, lg, gs, nw, s0_t)


def _conv_body(u_ref, buf_ref, cw_ref, cb_ref, g_ref, beta_ref, y_ref, nbuf_ref, seq_scr):
    j = pl.program_id(1)
    r = u_ref.shape[0]
    hist = CONV_WIDTH - 1

    @pl.when(j == 0)
    def _():
        seq_scr[0:CONV_HALO - hist, :] = jnp.zeros((CONV_HALO - hist, CONV_CH), F32)
        seq_scr[CONV_HALO - hist:CONV_HALO, :] = buf_ref[0]

    @pl.when(j > 0)
    def _():
        seq_scr[0:CONV_HALO, :] = seq_scr[r:r + CONV_HALO, :]

    seq_scr[CONV_HALO:CONV_HALO + r, :] = u_ref[...]

    window = seq_scr[...]
    rows = window.shape[0]
    acc = jnp.zeros((r, CONV_CH), F32) + cb_ref[...]
    for rho in range(SUBLANES):
        lo = CONV_HALO - hist + rho
        shifted = pltpu.roll(window, rows - lo, 0)
        for k in range(-(-CONV_WIDTH // SUBLANES)):
            w = SUBLANES * k + rho
            if w < CONV_WIDTH:
                acc = acc + shifted[SUBLANES * k:SUBLANES * k + r, :] * cw_ref[w:w + 1, :]
    y = _layer_norm(acc, g_ref[...], beta_ref[...])
    y_ref[...] = y * _sigmoid(y)

    @pl.when(j == pl.num_programs(1) - 1)
    def _():
        nbuf_ref[0] = seq_scr[r + CONV_HALO - hist:r + CONV_HALO, :]


def _conv(u, buf, cw, cb, g, beta, nb, t):
    r = min(t, 512)
    nj = t // r
    assert nj == 1 or r >= CONV_HALO
    hist = CONV_WIDTH - 1
    return pl.pallas_call(
        _conv_body,
        grid=(nb, nj),
        in_specs=[pl.BlockSpec((r, CONV_CH), lambda b, j: (b * nj + j, 0)),
                  pl.BlockSpec((1, hist, CONV_CH), lambda b, j: (b, 0, 0)),
                  _const_spec(cw.shape), _const_spec(cb.shape), _const_spec(g.shape), _const_spec(beta.shape)],
        out_specs=[pl.BlockSpec((r, CONV_CH), lambda b, j: (b * nj + j, 0)),
                   pl.BlockSpec((1, hist, CONV_CH), lambda b, j: (b, 0, 0))],
        out_shape=[jax.ShapeDtypeStruct((nb * t, CONV_CH), F32), jax.ShapeDtypeStruct((nb, hist, CONV_CH), F32)],
        scratch_shapes=[pltpu.VMEM((r + CONV_HALO, CONV_CH), F32)],
        compiler_params=_params("parallel", "arbitrary"),
        name="conv",
    )(u, buf, cw, cb, g, beta)


def _rel_bucket(dist):
    n = jnp.maximum(dist, 0)
    max_exact = REL_BUCKETS // 2
    nf = jnp.maximum(n, 1).astype(F32)
    large = max_exact + (jnp.log(nf / max_exact) / math.log(REL_MAX_DIST / max_exact)
                         * (REL_BUCKETS - max_exact)).astype(jnp.int32)
    large = jnp.minimum(large, REL_BUCKETS - 1)
    return jnp.where(n < max_exact, n, large)


def _bias_body(rb_ref, idx_ref, o_ref):
    h = pl.program_id(1)
    idx = idx_ref[0]
    acc = jnp.full(idx.shape, -jnp.inf, F32)
    for bucket in range(REL_BUCKETS):
        acc = jnp.where(idx == bucket, rb_ref[bucket, h], acc)
    o_ref[0, 0] = (acc - rb_ref[REL_BUCKETS - 1, h]) * LOG2E


def _bias_tiles(rel_bias, t):
    assert t >= REL_MAX_DIST
    ii = jnp.arange(t, dtype=jnp.int32)[:, None]
    jj = jnp.arange(t, dtype=jnp.int32)[None, :]
    idx = jnp.stack([jnp.where(ii >= jj, _rel_bucket(ii - jj), -1), _rel_bucket(t + ii - jj)])
    nh = rel_bias.shape[1]
    return pl.pallas_call(
        _bias_body,
        grid=(2, nh),
        in_specs=[pl.BlockSpec(memory_space=pltpu.SMEM),
                  pl.BlockSpec((1, t, t), lambda r, h: (r, 0, 0))],
        out_specs=pl.BlockSpec((1, 1, t, t), lambda r, h: (r, h, 0, 0)),
        out_shape=jax.ShapeDtypeStruct((2, nh, t, t), F32),
        compiler_params=_params("arbitrary", "arbitrary"),
        name="rel_bias_tiles",
    )(rel_bias, idx)


def _attn_body(lam_ref, q_ref, kt_ref, v_ref, bias_ref, dnw_ref, o_ref, q_scr, s_scr, m_scr, l_scr, acc_scr, *,
               out_scale):
    i = pl.program_id(1)
    t = q_ref.shape[0]
    nmaps = 2 * DIFF_HEADS
    lane = lax.broadcasted_iota(jnp.int32, (t, LANES), 1)
    for n in range(nmaps):
        qh = q_ref[:, (n // 2) * LANES:(n // 2 + 1) * LANES]
        keep = (lane < DIFF_DH) if n % 2 == 0 else (lane >= DIFF_DH)
        q_scr[n] = jnp.where(keep, qh, jnp.zeros_like(qh))

    def cols(j, ntile=1):
        return pl.ds(pl.multiple_of(j * t, t), ntile * t)

    def logits_tile(j, which, ntile=1):
        for n in range(nmaps):
            kt = kt_ref[0, (n // 2) * LANES:(n // 2 + 1) * LANES, cols(j, ntile)]
            s = _mm(q_scr[n], kt)
            if which is not None:
                s = s + bias_ref[which, n]
            s_scr[n, :, cols(j, ntile)] = s
            m_scr[n] = jnp.maximum(m_scr[n], _fold_lanes(s, jnp.maximum))

    m_scr[...] = jnp.full(m_scr.shape, -jnp.inf, F32)

    nfar = jnp.maximum(i - 1, 0)

    def far_pair(c, carry):
        logits_tile(2 * c, None, 2)
        return carry

    lax.fori_loop(0, nfar // 2, far_pair, 0)

    @pl.when(nfar % 2 == 1)
    def _():
        logits_tile(nfar - 1, None)

    @pl.when(i >= 1)
    def _():
        logits_tile(i - 1, 1)

    logits_tile(i, 0)

    for n in range(nmaps):
        m_scr[n] = jnp.broadcast_to(jnp.max(m_scr[n], axis=-1, keepdims=True), (t, LANES))
    l_scr[...] = jnp.zeros(l_scr.shape, F32)
    acc_scr[...] = jnp.zeros(acc_scr.shape, F32)

    def pv_tile(j, ntile=1):
        for n in range(nmaps):
            vb = v_ref[cols(j, ntile), (n // 2) * LANES:(n // 2 + 1) * LANES]
            mb = m_scr[n]
            ps = [jnp.exp2(s_scr[n, :, pl.ds(pl.multiple_of(j * t + c * LANES, LANES), LANES)] - mb)
                  for c in range(ntile * t // LANES)]
            part = ps[0]
            for pc in ps[1:]:
                part = part + pc
            l_scr[n] += part
            acc_scr[n] += _mm(jnp.concatenate(ps, axis=1).astype(BF16), vb)

    def pv_pair(c, carry):
        pv_tile(2 * c, 2)
        return carry

    lax.fori_loop(0, (i + 1) // 2, pv_pair, 0)

    @pl.when(i % 2 == 0)
    def _():
        pv_tile(i)

    lam = lam_ref[0]
    outs = []
    for hp in range(DIFF_HEADS):
        l0 = jnp.sum(l_scr[2 * hp], axis=-1, keepdims=True)
        l1 = jnp.sum(l_scr[2 * hp + 1], axis=-1, keepdims=True)
        o = acc_scr[2 * hp] / l0 - lam * (acc_scr[2 * hp + 1] / l1)
        ms_o = jnp.mean(o * o, axis=-1, keepdims=True)
        outs.append(o * lax.rsqrt(ms_o + EPS) * dnw_ref[...] * out_scale)
    o_ref[...] = jnp.concatenate(outs, axis=1)


def _attn_prompt(q16, kt16, v16, bias, lam, dnw, nb, s, out_scale):
    t = bias.shape[-1]
    nq = s // t
    nmaps = 2 * DIFF_HEADS
    return pl.pallas_call(
        functools.partial(_attn_body, out_scale=out_scale),
        grid=(nb, nq),
        in_specs=[pl.BlockSpec(memory_space=pltpu.SMEM),
                  pl.BlockSpec((t, DIFF_VW), lambda b, i: (b * nq + i, 0)),
                  pl.BlockSpec((1, DIFF_VW, s), lambda b, i: (b, 0, 0)),
                  pl.BlockSpec((s, DIFF_VW), lambda b, i: (b, 0)),
                  _const_spec(bias.shape), _const_spec(dnw.shape)],
        out_specs=pl.BlockSpec((t, DIFF_VW), lambda b, i: (b * nq + i, 0)),
        out_shape=jax.ShapeDtypeStruct((nb * s, DIFF_VW), F32),
        scratch_shapes=[pltpu.VMEM((nmaps, t, LANES), BF16), pltpu.VMEM((nmaps, t, s), F32),
                        pltpu.VMEM((nmaps, t, LANES), F32), pltpu.VMEM((nmaps, t, LANES), F32),
                        pltpu.VMEM((nmaps, t, LANES), F32)],
        compiler_params=_params("parallel", "arbitrary"),
        name="diff_attn_prompt",
    )(lam, q16, kt16, v16, bias, dnw)


def _sattn_phases(step, nsteps, lam_ref, q_ref, kn_ref, vn_ref, d1_ref, d0_ref, dnw_ref, kp, vp, o_ref, m_scr, l_scr,
                  acc_scr, out_scale):
    npg = len(kp)
    nh = 2 * DIFF_HEADS
    tq = q_ref.shape[0]
    rows_h = 2 * tq

    q = q_ref[...]
    lane_head = lax.broadcasted_iota(jnp.int32, q.shape, 1) // DIFF_DH
    qs = jnp.concatenate([jnp.where(lane_head == hh, q, 0.0) for hh in range(nh)], axis=0).astype(BF16)

    if step == 0:
        m_scr[...] = jnp.full(m_scr.shape, -jnp.inf, F32)
        l_scr[...] = jnp.zeros(l_scr.shape, F32)
        acc_scr[...] = jnp.zeros(acc_scr.shape, F32)

    def update(s, values_of):
        m_old = m_scr[...]
        m_new = jnp.maximum(m_old, jnp.max(s, axis=-1, keepdims=True))
        alpha = jnp.exp2(m_old - m_new)
        p = jnp.exp2(s - m_new)
        l_scr[...] = alpha * l_scr[...] + jnp.sum(p, axis=-1, keepdims=True)
        pvs = []
        for h in range(DIFF_HEADS):
            ph = p[h * rows_h:(h + 1) * rows_h].astype(BF16)
            pv = _mm(ph[:, 0:PAGE_SIZE], values_of(0, h))
            for g in range(1, s.shape[1] // PAGE_SIZE):
                pv = pv + _mm(ph[:, g * PAGE_SIZE:(g + 1) * PAGE_SIZE], values_of(g, h))
            pvs.append(pv)
        acc_scr[...] = alpha * acc_scr[...] + jnp.concatenate(pvs, axis=0)
        m_scr[...] = m_new

    is_last = step == nsteps - 1

    def logits():
        ss = []
        for g in range(npg):
            s = _mm(qs, kp[g][...].astype(BF16))
            if is_last and g == npg - 1:
                s = s + d1_ref[...]
            ss.append(s)
        return jnp.concatenate(ss, axis=1)

    def finish(s):
        update(s, lambda g, h: vp[g][pl.ds(h, PAGE_SIZE, stride=DIFF_HEADS), :].astype(BF16))
        if not is_last:
            return
        pad = jnp.zeros((PAGE_SIZE - tq, DIFF_VW), F32)
        kn = jnp.concatenate([kn_ref[...], pad], axis=0).astype(BF16)
        vn = jnp.concatenate([vn_ref[...], pad], axis=0).astype(BF16)
        update(_nt(qs, kn) + d0_ref[...], lambda g, h: vn[:, h * LANES:(h + 1) * LANES])
        o = acc_scr[...] / l_scr[...]
        lam = lam_ref[0]
        outs = []
        for h in range(DIFF_HEADS):
            oh = o[h * rows_h:h * rows_h + tq] - lam * o[h * rows_h + tq:(h + 1) * rows_h]
            ms = jnp.mean(oh * oh, axis=-1, keepdims=True)
            outs.append(oh * lax.rsqrt(ms + EPS) * dnw_ref[...] * out_scale)
        o_ref[...] = jnp.concatenate(outs, axis=1)

    return logits, finish


_MIX_CONSTS = ("w_out", "ln1_g", "ln1_b", "w_ffn_gate", "w_ffn_up", "w_ffn_down", "w_ple_gate", "w_ple_proj",
               "ln2_g", "ln2_b")


def _mix_head(oa_ref, ob_ref, oc_ref, x_ref, pe_ref, wo_ref, g1_ref, b1_ref, wpg_ref, wpp_ref):
    mix = _mm(oa_ref[...].astype(BF16), wo_ref[0:GLA_VW, :])
    mix = mix + _mm(ob_ref[...].astype(BF16), wo_ref[GLA_VW:GLA_VW + CONV_CH, :])
    mix = mix + _mm(oc_ref[...].astype(BF16), wo_ref[GLA_VW + CONV_CH:, :])
    x = _layer_norm(ALPHA * x_ref[...] + mix, g1_ref[...], b1_ref[...])
    xb = x.astype(BF16)
    ple = _sigmoid(_mm(xb, wpg_ref[...])) * _mm(pe_ref[...].astype(BF16), wpp_ref[...])
    return xb, ALPHA * x + ple


def _ffn_cols(xb, wg_ref, wu_ref, wd_ref, lo, hi):
    acc = None
    c = lo
    while c < hi:
        w = min(FFN_CHUNK, hi - c)
        gate = _mm(xb, wg_ref[:, c:c + w])
        hid = gate * _sigmoid(gate) * _mm(xb, wu_ref[:, c:c + w])
        part = _mm(hid.astype(BF16), wd_ref[c:c + w, :])
        acc = part if acc is None else acc + part
        c += w
    return acc


def _mix_ffn_body(oa_ref, ob_ref, oc_ref, x_ref, pe_ref, wo_ref, g1_ref, b1_ref, wg_ref, wu_ref, wd_ref, wpg_ref,
                  wpp_ref, g2_ref, b2_ref, y_ref):
    xb, base = _mix_head(oa_ref, ob_ref, oc_ref, x_ref, pe_ref, wo_ref, g1_ref, b1_ref, wpg_ref, wpp_ref)
    acc = base + _ffn_cols(xb, wg_ref, wu_ref, wd_ref, 0, wg_ref.shape[1])
    y_ref[...] = _layer_norm(acc, g2_ref[...], b2_ref[...])


def _mix_ffn(oa, ob, oc, x2, pe_all, layer, lw):
    m, d = x2.shape
    tm = min(MIX_ROWS, m)
    nt = m // tm
    consts = [lw[n] for n in _MIX_CONSTS]

    def rows(n):
        return pl.BlockSpec((tm, n), lambda i: (i, 0))

    return pl.pallas_call(
        _mix_ffn_body,
        grid=(nt,),
        in_specs=[rows(GLA_VW), rows(CONV_CH), rows(DIFF_VW), rows(d),
                  pl.BlockSpec((tm, pe_all.shape[1]), lambda i: (layer * nt + i, 0))]
                 + [_const_spec(a.shape) for a in consts],
        out_specs=rows(d),
        out_shape=jax.ShapeDtypeStruct((m, d), F32),
        compiler_params=_params("parallel"),
        name="mix_ffn",
    )(oa, ob, oc, x2, pe_all, *consts)


def _ffn_schedule(nsteps, d_ff):
    nblk = d_ff // LANES
    if nsteps == 1:
        return [(0, d_ff)]
    base, rem = divmod(nblk, nsteps - 1)
    out, lo = [(0, 0)], 0
    for k in range(nsteps - 1):
        hi = lo + (base + (1 if k < rem else 0)) * LANES
        out.append((lo, hi))
        lo = hi
    return out


def _mix_attn_body(pt_ref, lam_ref, q_ref, kn_ref, vn_ref, d1_ref, d0_ref, dnw_ref, oa_ref, ob_ref, oc_ref, x_ref,
                   pe_ref, wo_ref, g1_ref, b1_ref, wg_ref, wu_ref, wd_ref, wpg_ref, wpp_ref, g2_ref, b2_ref, *rest,
                   npg, nsteps, out_scale):
    del pt_ref
    kp = rest[:npg]
    vp = rest[npg:2 * npg]
    y_ref, os_ref = rest[2 * npg:2 * npg + 2]
    m_scr, l_scr, acc_scr, xb_scr, facc_scr = rest[2 * npg + 2:]
    p_id = pl.program_id(1)
    sched = _ffn_schedule(nsteps, wg_ref.shape[1])

    for step in range(nsteps):
        @pl.when(p_id == step)
        def _(step=step):
            logits, finish = _sattn_phases(step, nsteps, lam_ref, q_ref, kn_ref, vn_ref, d1_ref, d0_ref, dnw_ref,
                                           kp, vp, os_ref, m_scr, l_scr, acc_scr, out_scale)
            s = logits()
            lo, hi = sched[step]
            if step == 0:
                xb, acc = _mix_head(oa_ref, ob_ref, oc_ref, x_ref, pe_ref, wo_ref, g1_ref, b1_ref, wpg_ref, wpp_ref)
                xb_scr[...] = xb
            else:
                xb, acc = xb_scr[...], facc_scr[...]
                mid = min(lo + FFN_CHUNK, hi)
                acc = acc + _ffn_cols(xb, wg_ref, wu_ref, wd_ref, lo, mid)
                lo = mid
            finish(s)
            if hi > lo:
                acc = acc + _ffn_cols(xb, wg_ref, wu_ref, wd_ref, lo, hi)
            if step == nsteps - 1:
                y_ref[...] = _layer_norm(acc, g2_ref[...], b2_ref[...])
            else:
                facc_scr[...] = acc


def _mix_ffn_attn(oa, ob, oc, x2, pe_all, layer, lw, dq, dk, dv, cache_kt, cache_v2, page_table, bias, out_scale):
    m, d = x2.shape
    tm = min(MIX_ROWS, m)
    nt = m // tm
    nb, n_pages = page_table.shape
    tq = dq.shape[0] // nb
    npg = PAGES_PER_STEP
    nsteps = n_pages // npg
    assert nb == nt and n_pages % npg == 0 and cache_kt.shape[-1] == PAGE_SIZE and tq == SUBLANES
    nh = 2 * DIFF_HEADS
    t = bias.shape[-1]
    consts = [lw[n] for n in _MIX_CONSTS]
    d1 = bias[1, :, 0:tq, t - PAGE_SIZE:t].reshape(nh * tq, PAGE_SIZE)
    d0 = jnp.concatenate([bias[0, :, 0:tq, 0:tq], jnp.full((nh, tq, PAGE_SIZE - tq), -jnp.inf, F32)],
                         axis=-1).reshape(nh * tq, PAGE_SIZE)

    def page_spec(g):
        return pl.BlockSpec((None, None, DIFF_VW, PAGE_SIZE),
                            lambda i, p, pt: (layer, pt[i * n_pages + p * npg + g], 0, 0))

    def rows(r, n):
        return pl.BlockSpec((r, n), lambda i, p, pt: (i, 0))

    def full_spec(a):
        nd = a.ndim
        return pl.BlockSpec(a.shape, lambda i, p, pt: (0,) * nd)

    grid_spec = pltpu.PrefetchScalarGridSpec(
        num_scalar_prefetch=1,
        grid=(nt, nsteps),
        in_specs=[pl.BlockSpec(memory_space=pltpu.SMEM), rows(tq, DIFF_VW), rows(tq, DIFF_VW), rows(tq, DIFF_VW),
                  full_spec(d1), full_spec(d0), full_spec(lw["dnw"]),
                  rows(tm, GLA_VW), rows(tm, CONV_CH), rows(tm, DIFF_VW), rows(tm, d),
                  pl.BlockSpec((tm, pe_all.shape[1]), lambda i, p, pt: (layer * nt + i, 0))]
                 + [_const_spec(a.shape) for a in consts]
                 + [page_spec(g) for g in range(npg)] + [page_spec(g) for g in range(npg)],
        out_specs=[rows(tm, d), rows(tq, DIFF_VW)],
        scratch_shapes=[pltpu.VMEM((nh * tq, 1), F32), pltpu.VMEM((nh * tq, 1), F32),
                        pltpu.VMEM((nh * tq, LANES), F32), pltpu.VMEM((tm, d), BF16), pltpu.VMEM((tm, d), F32)],
    )
    return pl.pallas_call(
        functools.partial(_mix_attn_body, npg=npg, nsteps=nsteps, out_scale=out_scale),
        grid_spec=grid_spec,
        out_shape=[jax.ShapeDtypeStruct((m, d), F32), jax.ShapeDtypeStruct((nb * tq, DIFF_VW), F32)],
        compiler_params=_params("parallel", "arbitrary"),
        name="mix_ffn_sample_attn",
    )(page_table.reshape(-1), lw["lam"], dq, dk, dv, d1, d0, lw["dnw"], oa, ob, oc, x2, pe_all, *consts,
      *([cache_kt] * npg), *([cache_v2] * npg))


def _row(a):
    return a.reshape(1, -1)


def _state_to_t(s):
    nb = s.shape[0]
    eye = jnp.eye(GLA_HEADS, dtype=s.dtype)
    return jnp.einsum("bhde,hg->bhegd", s, eye).reshape(nb, GLA_VW, GLA_QK)


def _state_from_t(s_t):
    nb = s_t.shape[0]
    blocks = s_t.reshape(nb, GLA_HEADS, GLA_DV, GLA_HEADS, GLA_DK)
    diag = jnp.stack([blocks[:, h, :, h, :] for h in range(GLA_HEADS)], axis=1)
    return diag.transpose(0, 1, 3, 2)


def _token_mixers(x2, nb, t, lw, s0, buf, transposed_k):
    gq, gk, gv, gs, lg, u, dq, dv, *kk = _in_proj(x2, lw["w_in"], lw["wk_t"], lw["wg"], lw["bg"], nb, transposed_k)
    o_a, s_t = _gla(gq, gk, gv, lg, gs, lw["gla_nw"], _state_to_t(s0), nb, t)
    o_b, nbuf = _conv(u, buf, lw["conv_w"], lw["conv_b"], lw["conv_g"], lw["conv_beta"], nb, t)
    return o_a, o_b, dq, dv, kk, _state_from_t(s_t), nbuf


def _prep_w_in(w):
    sizes = (GLA_QK, GLA_QK, GLA_VW, GLA_VW, GLA_GATE_RANK, CONV_CH, CONV_CH, DIFF_VW, DIFF_VW, DIFF_VW)
    parts, s = [], 0
    for n in sizes:
        parts.append(w[:, s:s + n])
        s += n
    wk_t = parts[8].T.astype(BF16)
    parts[4] = jnp.pad(parts[4], ((0, 0), (0, LANES - GLA_GATE_RANK)))
    return jnp.concatenate(parts, axis=1).astype(BF16), wk_t


def _layer_weights(i, w_in, gla_w_gate_up, gla_b_gate, gla_norm_w, conv_w, conv_b, conv_ln_g, conv_ln_b, diff_lq1,
                   diff_lk1, diff_lq2, diff_lk2, diff_norm_w, w_out, ln1_g, ln1_b, w_ffn_gate, w_ffn_up, w_ffn_down,
                   w_ple_gate, w_ple_proj, ln2_g, ln2_b):
    lam_init = 0.8 - 0.6 * math.exp(-0.3 * i)
    lam = (jnp.exp(jnp.sum(diff_lq1[i] * diff_lk1[i])) - jnp.exp(jnp.sum(diff_lq2[i] * diff_lk2[i]))
           + lam_init).reshape(1).astype(F32)
    w_in_p, wk_t = _prep_w_in(w_in[i])
    return dict(
        w_in=w_in_p, wk_t=wk_t,
        wg=jnp.pad(gla_w_gate_up[i], ((0, LANES - GLA_GATE_RANK), (0, 0))).astype(BF16),
        bg=_row(gla_b_gate[i]),
        gla_nw=_row(jnp.tile(gla_norm_w[i], GLA_HEADS)),
        conv_w=conv_w[i], conv_b=_row(conv_b[i]), conv_g=_row(conv_ln_g[i]), conv_beta=_row(conv_ln_b[i]),
        lam=lam, dnw=_row(diff_norm_w[i]),
        w_out=w_out[i].astype(BF16), ln1_g=_row(ln1_g[i]), ln1_b=_row(ln1_b[i]),
        w_ffn_gate=w_ffn_gate[i].astype(BF16), w_ffn_up=w_ffn_up[i].astype(BF16),
        w_ffn_down=w_ffn_down[i].astype(BF16), w_ple_gate=w_ple_gate[i].astype(BF16),
        w_ple_proj=w_ple_proj[i].astype(BF16), ln2_g=_row(ln2_g[i]), ln2_b=_row(ln2_b[i]),
    )


def kernel(x_prompt, x_sample, cache_k, cache_v, state_gla, state_conv, page_table, p_prompt, p_sample, w_in, gla_w_gate_up, gla_b_gate, gla_norm_w, conv_w, conv_b, conv_ln_g, conv_ln_b, diff_lq1, diff_lk1, diff_lq2, diff_lk2, diff_norm_w, rel_bias, w_out, ln1_g, ln1_b, w_ffn_gate, w_ffn_up, w_ffn_down, w_ple_gate, w_ple_proj, ln2_g, ln2_b):
    nbp, s, d = x_prompt.shape
    nbs, ts, _ = x_sample.shape
    depth, n_pool = cache_k.shape[:2]
    assert depth == DEPTH and cache_k.shape[2] == PAGE_SIZE
    cache_kt = jnp.transpose(cache_k, (0, 1, 3, 4, 2)).reshape(depth, n_pool, DIFF_VW, PAGE_SIZE)
    cache_v2 = cache_v.reshape(depth, n_pool, PAGE_SIZE * DIFF_HEADS, 2 * DIFF_DH)
    bias = _bias_tiles(rel_bias, ATTN_TILE)

    pe_prompt = p_prompt.reshape(depth * nbp * s, -1)
    pe_sample = p_sample.reshape(depth * nbs * ts, -1)
    yp = x_prompt.reshape(nbp * s, d)
    ys = x_sample.reshape(nbs * ts, d)
    zero_state = jnp.zeros((nbp, GLA_HEADS, GLA_DK, GLA_DV), F32)
    zero_buf = jnp.zeros((nbp, CONV_WIDTH - 1, CONV_CH), F32)
    outs = [[] for _ in range(8)]
    for i in range(depth):
        lw = _layer_weights(i, w_in, gla_w_gate_up, gla_b_gate, gla_norm_w, conv_w, conv_b, conv_ln_g, conv_ln_b,
                            diff_lq1, diff_lk1, diff_lq2, diff_lk2, diff_norm_w, w_out, ln1_g, ln1_b, w_ffn_gate,
                            w_ffn_up, w_ffn_down, w_ple_gate, w_ple_proj, ln2_g, ln2_b)

        out_scale = 1.0 - (0.8 - 0.6 * math.exp(-0.3 * i))
        pa, pb, q16, v_rows, (kt_i, kt16, v16), s_i, c_i = _token_mixers(yp, nbp, s, lw, zero_state, zero_buf, True)
        pc = _attn_prompt(q16, kt16, v16, bias, lw["lam"], lw["dnw"], nbp, s, out_scale)
        outs[0].append(kt_i.reshape(nbp, 2 * DIFF_HEADS, DIFF_DH, s).transpose(0, 3, 1, 2))
        outs[1].append(v_rows.reshape(nbp, s, DIFF_HEADS, 2 * DIFF_DH))
        outs[2].append(s_i)
        outs[3].append(c_i)
        sa, sb, dq, dv, (dk,), s_i, c_i = _token_mixers(ys, nbs, ts, lw, state_gla[i], state_conv[i], False)
        outs[4].append(dk.reshape(nbs, ts, 2 * DIFF_HEADS, DIFF_DH))
        outs[5].append(dv.reshape(nbs, ts, DIFF_HEADS, 2 * DIFF_DH))
        outs[6].append(s_i)
        outs[7].append(c_i)
        yp, sc = _mix_ffn_attn(pa, pb, pc, yp, pe_prompt, i, lw, dq, dk, dv, cache_kt, cache_v2, page_table, bias,
                               out_scale)
        ys = _mix_ffn(sa, sb, sc, ys, pe_sample, i, lw)
    return (yp.reshape(nbp, s, d), ys.reshape(nbs, ts, d)) + tuple(jnp.stack(o) for o in outs)
```

```python
import functools
import math

import jax
import jax.numpy as jnp
from jax import lax
from jax.experimental import pallas as pl
from jax.experimental.pallas import tpu as pltpu

F32 = jnp.float32
BF16 = jnp.bfloat16

GLA_HEADS = 4
GLA_DK = 32
GLA_DV = 64
GLA_QK = GLA_HEADS * GLA_DK
GLA_VW = GLA_HEADS * GLA_DV
GLA_GATE_RANK = 16
GLA_TAU = 16.0
CONV_CH = 256
CONV_WIDTH = 31
DIFF_HEADS = 4
DIFF_DH = 64
DIFF_VW = 2 * DIFF_HEADS * DIFF_DH
REL_BUCKETS = 32
REL_MAX_DIST = 128
PAGE_SIZE = 128
DEPTH = 2
ALPHA = (2 * DEPTH) ** 0.25
EPS = 1e-5
LOG2E = math.log2(math.e)

LANES = 128
SUBLANES = 8
GLA_BLOCK = 16
GLA_SUBTILE = 128
ATTN_TILE = 256
MIX_ROWS = 512
FFN_CHUNK = 256
PAGES_PER_STEP = 16
CONV_HALO = 32
VMEM_LIMIT = 56 * 1024 * 1024

_IN_OFF = {}
_o = 0
for _name, _n in (("gq", 128), ("gk", 128), ("gv", 256), ("gg", 256), ("glr", 128), ("ca", 256), ("cg", 256),
                  ("dq", 512), ("dk", 512), ("dv", 512)):
    _IN_OFF[_name] = (_o, _n)
    _o += _n
N_IN_PAD = _o


def _params(*sem):
    return pltpu.CompilerParams(dimension_semantics=sem, vmem_limit_bytes=VMEM_LIMIT)


def _const_spec(shape):
    nd = len(shape)
    return pl.BlockSpec(shape, lambda *_: (0,) * nd, pipeline_mode=pl.Buffered(1))


def _nt(a, b):
    return lax.dot_general(a, b, (((1,), (1,)), ((), ())), preferred_element_type=F32)


def _mm(a, b):
    return jnp.dot(a, b, preferred_element_type=F32)


def _mm_split(a, b):
    hi = a.astype(BF16)
    lo = (a - hi.astype(F32)).astype(BF16)
    return _mm(hi, b) + _mm(lo, b)


def _sigmoid(x):
    return 1.0 / (1.0 + jnp.exp(-x))


def _layer_norm(x, g, b):
    mu = jnp.mean(x, axis=-1, keepdims=True)
    xc = x - mu
    var = jnp.mean(xc * xc, axis=-1, keepdims=True)
    return xc * lax.rsqrt(var + EPS) * g + b


def _fold_lanes(x, op):
    acc = x[:, 0:LANES]
    for c in range(1, x.shape[1] // LANES):
        acc = op(acc, x[:, c * LANES:(c + 1) * LANES])
    return acc


def _inproj_body(x_ref, w_ref, wkt_ref, wg_ref, bg_ref, gq_ref, gk_ref, gv_ref, gs_ref, lg_ref, u_ref, dq_ref,
                 dv_ref, *k_refs, transposed_k):
    x = x_ref[...].astype(BF16)

    def proj(name):
        lo, n = _IN_OFF[name]
        return _mm(x, w_ref[:, lo:lo + n])

    gq_ref[...] = proj("gq") * GLA_DK ** -0.5
    gk_ref[...] = proj("gk")
    gv_ref[...] = proj("gv")
    gg = proj("gg")
    gs_ref[...] = gg * _sigmoid(gg)
    z = _mm(proj("glr").astype(BF16), wg_ref[...]) + bg_ref[...]
    lg_ref[...] = (jnp.minimum(z, 0.0) - jnp.log(1.0 + jnp.exp(-jnp.abs(z)))) * (1.0 / GLA_TAU)
    u_ref[...] = proj("ca") * _sigmoid(proj("cg"))
    dq_ref[...] = (proj("dq") * (DIFF_DH ** -0.5 * LOG2E)).astype(dq_ref.dtype)
    dv = proj("dv")
    if transposed_k:
        kt_ref, kt16_ref, v16_ref = k_refs
        tm = x.shape[0]
        for h in range(DIFF_HEADS):
            dv_ref[pl.ds(h, tm, stride=DIFF_HEADS), :] = dv[:, h * LANES:(h + 1) * LANES]
        kt = _nt(wkt_ref[...], x)
        kt_ref[0] = kt
        kt16_ref[0] = kt.astype(BF16)
        v16_ref[...] = dv.astype(BF16)
    else:
        dv_ref[...] = dv
        k_refs[0][...] = proj("dk")


def _in_proj(x2, w_in_p, wk_t, wg_p, bg, nb, transposed_k):
    m, d = x2.shape
    t = m // nb
    tm = min(512, t if transposed_k else m)
    nj = t // tm
    widths = (GLA_QK, GLA_QK, GLA_VW, GLA_VW, GLA_QK, CONV_CH, DIFF_VW)
    dtypes = [F32] * 6 + [BF16 if transposed_k else F32]
    out_specs = [pl.BlockSpec((tm, n), lambda i: (i, 0)) for n in widths]
    out_shape = [jax.ShapeDtypeStruct((m, n), dt) for n, dt in zip(widths, dtypes)]
    if transposed_k:
        out_specs.append(pl.BlockSpec((tm * DIFF_HEADS, LANES), lambda i: (i, 0)))
        out_shape.append(jax.ShapeDtypeStruct((m * DIFF_HEADS, LANES), F32))
        kt_spec = pl.BlockSpec((1, DIFF_VW, tm), lambda i: (i // nj, 0, i % nj))
        out_specs += [kt_spec, kt_spec, pl.BlockSpec((tm, DIFF_VW), lambda i: (i, 0))]
        out_shape += [jax.ShapeDtypeStruct((nb, DIFF_VW, t), F32), jax.ShapeDtypeStruct((nb, DIFF_VW, t), BF16),
                      jax.ShapeDtypeStruct((m, DIFF_VW), BF16)]
    else:
        out_specs += [pl.BlockSpec((tm, DIFF_VW), lambda i: (i, 0))] * 2
        out_shape += [jax.ShapeDtypeStruct((m, DIFF_VW), F32)] * 2
    return pl.pallas_call(
        functools.partial(_inproj_body, transposed_k=transposed_k),
        grid=(m // tm,),
        in_specs=[pl.BlockSpec((tm, d), lambda i: (i, 0)),
                  _const_spec(w_in_p.shape), _const_spec(wk_t.shape), _const_spec(wg_p.shape), _const_spec(bg.shape)],
        out_specs=out_specs,
        out_shape=out_shape,
        compiler_params=_params("parallel"),
        name="in_proj",
    )(x2, w_in_p, wk_t, wg_p, bg)


def _gla_body(q_ref, k_ref, v_ref, lg_ref, gs_ref, nw_ref, s0_ref, o_ref, sout_ref, s_scr, *, cb):
    j = pl.program_id(1)

    @pl.when(j == 0)
    def _():
        s_scr[...] = s0_ref[0]

    r = q_ref.shape[0]
    rs = min(r, GLA_SUBTILE)
    row = lax.broadcasted_iota(jnp.int32, (rs, 1), 0) % cb

    same_head = (lax.broadcasted_iota(jnp.int32, (GLA_QK, GLA_VW), 0) // GLA_DK
                 == lax.broadcasted_iota(jnp.int32, (GLA_QK, GLA_VW), 1) // GLA_DV)
    expand = same_head.astype(BF16)
    mask_t = (lax.broadcasted_iota(jnp.int32, (GLA_VW, GLA_QK), 0) // GLA_DV
              == lax.broadcasted_iota(jnp.int32, (GLA_VW, GLA_QK), 1) // GLA_DK).astype(F32)
    grp = (lax.broadcasted_iota(jnp.int32, (GLA_VW, GLA_VW), 0) // GLA_DV
           == lax.broadcasted_iota(jnp.int32, (GLA_VW, GLA_VW), 1) // GLA_DV)
    head_mean = jnp.where(grp, 1.0 / GLA_DV, 0.0).astype(BF16)

    st = s_scr[...]
    for sub in range(r // rs):
        rows = pl.ds(sub * rs, rs)
        q = q_ref[rows, :]
        k = k_ref[rows, :]
        v = v_ref[rows, :]

        b = lg_ref[rows, :]
        s = 1
        while s < cb:
            b = b + jnp.where(row >= s, pltpu.roll(b, s, 0), 0.0)
            s *= 2

        o = _mm((q * k).astype(BF16), expand) * v
        for delta in range(1, cb):
            ks = pltpu.roll(k, delta, 0)
            bs = pltpu.roll(b, delta, 0)
            vs = pltpu.roll(v, delta, 0)
            p = jnp.where(row >= delta, q * ks * jnp.exp(b - bs), 0.0)
            o = o + _mm(p.astype(BF16), expand) * vs

        inter = []
        for t in range(rs // cb):
            sl = slice(t * cb, (t + 1) * cb)
            bt = b[sl]
            bend = bt[cb - 1:cb]
            qe = (q[sl] * jnp.exp(bt)).astype(BF16)
            inter.append(_nt(qe, st.astype(BF16)))
            ke = k[sl] * jnp.exp(bend - bt)
            vt = v[sl]
            if cb < 16:
                ke = jnp.concatenate([ke, jnp.zeros((16 - cb, GLA_QK), F32)], axis=0)
                vt = jnp.concatenate([vt, jnp.zeros((16 - cb, GLA_VW), F32)], axis=0)
            kv = lax.dot_general(vt.astype(BF16), ke.astype(BF16), (((0,), (0,)), ((), ())),
                                 preferred_element_type=F32)
            st = st * jnp.exp(bend) + kv * mask_t
        o = o + (inter[0] if len(inter) == 1 else jnp.concatenate(inter, axis=0))

        mean_sq = _mm_split(o * o, head_mean)
        o_ref[rows, :] = o * lax.rsqrt(mean_sq + EPS) * nw_ref[...] * gs_ref[rows, :]
    s_scr[...] = st

    @pl.when(j == pl.num_programs(1) - 1)
    def _():
        sout_ref[0] = st


def _gla(gq, gk, gv, lg, gs, nw, s0_t, nb, t):
    cb = GLA_BLOCK if t % GLA_BLOCK == 0 else t
    r = min(t, 256)
    nj = t // r
    m = nb * t

    def rows(n):
        return pl.BlockSpec((r, n), lambda b, j: (b * nj + j, 0))

    return pl.pallas_call(
        functools.partial(_gla_body, cb=cb),
        grid=(nb, nj),
        in_specs=[rows(GLA_QK), rows(GLA_QK), rows(GLA_VW), rows(GLA_QK), rows(GLA_VW),
                  _const_spec(nw.shape),
                  pl.BlockSpec((1, GLA_VW, GLA_QK), lambda b, j: (b, 0, 0))],
        out_specs=[rows(GLA_VW), pl.BlockSpec((1, GLA_VW, GLA_QK), lambda b, j: (b, 0, 0))],
        out_shape=[jax.ShapeDtypeStruct((m, GLA_VW), F32), jax.ShapeDtypeStruct((nb, GLA_VW, GLA_QK), F32)],
        scratch_shapes=[pltpu.VMEM((GLA_VW, GLA_QK), F32)],
        compiler_params=_params("parallel", "arbitrary"),
        name="gla",
    )(gq, gk, gv, lg, gs, nw, s0_t)


def _conv_body(u_ref, buf_ref, cw_ref, cb_ref, g_ref, beta_ref, y_ref, nbuf_ref, seq_scr):
    j = pl.program_id(1)
    r = u_ref.shape[0]
    hist = CONV_WIDTH - 1

    @pl.when(j == 0)
    def _():
        seq_scr[0:CONV_HALO - hist, :] = jnp.zeros((CONV_HALO - hist, CONV_CH), F32)
        seq_scr[CONV_HALO - hist:CONV_HALO, :] = buf_ref[0]

    @pl.when(j > 0)
    def _():
        seq_scr[0:CONV_HALO, :] = seq_scr[r:r + CONV_HALO, :]

    seq_scr[CONV_HALO:CONV_HALO + r, :] = u_ref[...]

    window = seq_scr[...]
    rows = window.shape[0]
    acc = jnp.zeros((r, CONV_CH), F32) + cb_ref[...]
    for rho in range(SUBLANES):
        lo = CONV_HALO - hist + rho
        shifted = pltpu.roll(window, rows - lo, 0)
        for k in range(-(-CONV_WIDTH // SUBLANES)):
            w = SUBLANES * k + rho
            if w < CONV_WIDTH:
                acc = acc + shifted[SUBLANES * k:SUBLANES * k + r, :] * cw_ref[w:w + 1, :]
    y = _layer_norm(acc, g_ref[...], beta_ref[...])
    y_ref[...] = y * _sigmoid(y)

    @pl.when(j == pl.num_programs(1) - 1)
    def _():
        nbuf_ref[0] = seq_scr[r + CONV_HALO - hist:r + CONV_HALO, :]


def _conv(u, buf, cw, cb, g, beta, nb, t):
    r = min(t, 512)
    nj = t // r
    assert nj == 1 or r >= CONV_HALO
    hist = CONV_WIDTH - 1
    return pl.pallas_call(
        _conv_body,
        grid=(nb, nj),
        in_specs=[pl.BlockSpec((r, CONV_CH), lambda b, j: (b * nj + j, 0)),
                  pl.BlockSpec((1, hist, CONV_CH), lambda b, j: (b, 0, 0)),
                  _const_spec(cw.shape), _const_spec(cb.shape), _const_spec(g.shape), _const_spec(beta.shape)],
        out_specs=[pl.BlockSpec((r, CONV_CH), lambda b, j: (b * nj + j, 0)),
                   pl.BlockSpec((1, hist, CONV_CH), lambda b, j: (b, 0, 0))],
        out_shape=[jax.ShapeDtypeStruct((nb * t, CONV_CH), F32), jax.ShapeDtypeStruct((nb, hist, CONV_CH), F32)],
        scratch_shapes=[pltpu.VMEM((r + CONV_HALO, CONV_CH), F32)],
        compiler_params=_params("parallel", "arbitrary"),
        name="conv",
    )(u, buf, cw, cb, g, beta)


def _rel_bucket(dist):
    n = jnp.maximum(dist, 0)
    max_exact = REL_BUCKETS // 2
    nf = jnp.maximum(n, 1).astype(F32)
    large = max_exact + (jnp.log(nf / max_exact) / math.log(REL_MAX_DIST / max_exact)
                         * (REL_BUCKETS - max_exact)).astype(jnp.int32)
    large = jnp.minimum(large, REL_BUCKETS - 1)
    return jnp.where(n < max_exact, n, large)


def _bias_body(rb_ref, idx_ref, o_ref):
    h = pl.program_id(1)
    idx = idx_ref[0]
    acc = jnp.full(idx.shape, -jnp.inf, F32)
    for bucket in range(REL_BUCKETS):
        acc = jnp.where(idx == bucket, rb_ref[bucket, h], acc)
    o_ref[0, 0] = (acc - rb_ref[REL_BUCKETS - 1, h]) * LOG2E


def _bias_tiles(rel_bias, t):
    assert t >= REL_MAX_DIST
    ii = jnp.arange(t, dtype=jnp.int32)[:, None]
    jj = jnp.arange(t, dtype=jnp.int32)[None, :]
    idx = jnp.stack([jnp.where(ii >= jj, _rel_bucket(ii - jj), -1), _rel_bucket(t + ii - jj)])
    nh = rel_bias.shape[1]
    return pl.pallas_call(
        _bias_body,
        grid=(2, nh),
        in_specs=[pl.BlockSpec(memory_space=pltpu.SMEM),
                  pl.BlockSpec((1, t, t), lambda r, h: (r, 0, 0))],
        out_specs=pl.BlockSpec((1, 1, t, t), lambda r, h: (r, h, 0, 0)),
        out_shape=jax.ShapeDtypeStruct((2, nh, t, t), F32),
        compiler_params=_params("arbitrary", "arbitrary"),
        name="rel_bias_tiles",
    )(rel_bias, idx)


def _attn_body(lam_ref, q_ref, kt_ref, v_ref, bias_ref, dnw_ref, o_ref, q_scr, s_scr, m_scr, l_scr, acc_scr, *,
               out_scale):
    i = pl.program_id(1)
    t = q_ref.shape[0]
    nmaps = 2 * DIFF_HEADS
    lane = lax.broadcasted_iota(jnp.int32, (t, LANES), 1)
    for n in range(nmaps):
        qh = q_ref[:, (n // 2) * LANES:(n // 2 + 1) * LANES]
        keep = (lane < DIFF_DH) if n % 2 == 0 else (lane >= DIFF_DH)
        q_scr[n] = jnp.where(keep, qh, jnp.zeros_like(qh))

    def cols(j, ntile=1):
        return pl.ds(pl.multiple_of(j * t, t), ntile * t)

    def logits_tile(j, which, ntile=1):
        for n in range(nmaps):
            kt = kt_ref[0, (n // 2) * LANES:(n // 2 + 1) * LANES, cols(j, ntile)]
            s = _mm(q_scr[n], kt)
            if which is not None:
                s = s + bias_ref[which, n]
            s_scr[n, :, cols(j, ntile)] = s
            m_scr[n] = jnp.maximum(m_scr[n], _fold_lanes(s, jnp.maximum))

    m_scr[...] = jnp.full(m_scr.shape, -jnp.inf, F32)

    nfar = jnp.maximum(i - 1, 0)

    def far_pair(c, carry):
        logits_tile(2 * c, None, 2)
        return carry

    lax.fori_loop(0, nfar // 2, far_pair, 0)

    @pl.when(nfar % 2 == 1)
    def _():
        logits_tile(nfar - 1, None)

    @pl.when(i >= 1)
    def _():
        logits_tile(i - 1, 1)

    logits_tile(i, 0)

    for n in range(nmaps):
        m_scr[n] = jnp.broadcast_to(jnp.max(m_scr[n], axis=-1, keepdims=True), (t, LANES))
    l_scr[...] = jnp.zeros(l_scr.shape, F32)
    acc_scr[...] = jnp.zeros(acc_scr.shape, F32)

    def pv_tile(j, ntile=1):
        for n in range(nmaps):
            vb = v_ref[cols(j, ntile), (n // 2) * LANES:(n // 2 + 1) * LANES]
            mb = m_scr[n]
            ps = [jnp.exp2(s_scr[n, :, pl.ds(pl.multiple_of(j * t + c * LANES, LANES), LANES)] - mb)
                  for c in range(ntile * t // LANES)]
            part = ps[0]
            for pc in ps[1:]:
                part = part + pc
            l_scr[n] += part
            acc_scr[n] += _mm(jnp.concatenate(ps, axis=1).astype(BF16), vb)

    def pv_pair(c, carry):
        pv_tile(2 * c, 2)
        return carry

    lax.fori_loop(0, (i + 1) // 2, pv_pair, 0)

    @pl.when(i % 2 == 0)
    def _():
        pv_tile(i)

    lam = lam_ref[0]
    outs = []
    for hp in range(DIFF_HEADS):
        l0 = jnp.sum(l_scr[2 * hp], axis=-1, keepdims=True)
        l1 = jnp.sum(l_scr[2 * hp + 1], axis=-1, keepdims=True)
        o = acc_scr[2 * hp] / l0 - lam * (acc_scr[2 * hp + 1] / l1)
        ms_o = jnp.mean(o * o, axis=-1, keepdims=True)
        outs.append(o * lax.rsqrt(ms_o + EPS) * dnw_ref[...] * out_scale)
    o_ref[...] = jnp.concatenate(outs, axis=1)


def _attn_prompt(q16, kt16, v16, bias, lam, dnw, nb, s, out_scale):
    t = bias.shape[-1]
    nq = s // t
    nmaps = 2 * DIFF_HEADS
    return pl.pallas_call(
        functools.partial(_attn_body, out_scale=out_scale),
        grid=(nb, nq),
        in_specs=[pl.BlockSpec(memory_space=pltpu.SMEM),
                  pl.BlockSpec((t, DIFF_VW), lambda b, i: (b * nq + i, 0)),
                  pl.BlockSpec((1, DIFF_VW, s), lambda b, i: (b, 0, 0)),
                  pl.BlockSpec((s, DIFF_VW), lambda b, i: (b, 0)),
                  _const_spec(bias.shape), _const_spec(dnw.shape)],
        out_specs=pl.BlockSpec((t, DIFF_VW), lambda b, i: (b * nq + i, 0)),
        out_shape=jax.ShapeDtypeStruct((nb * s, DIFF_VW), F32),
        scratch_shapes=[pltpu.VMEM((nmaps, t, LANES), BF16), pltpu.VMEM((nmaps, t, s), F32),
                        pltpu.VMEM((nmaps, t, LANES), F32), pltpu.VMEM((nmaps, t, LANES), F32),
                        pltpu.VMEM((nmaps, t, LANES), F32)],
        compiler_params=_params("parallel", "arbitrary"),
        name="diff_attn_prompt",
    )(lam, q16, kt16, v16, bias, dnw)


def _sattn_phases(step, nsteps, lam_ref, q_ref, kn_ref, vn_ref, d1_ref, d0_ref, dnw_ref, kp, vp, o_ref, m_scr, l_scr,
                  acc_scr, out_scale):
    npg = len(kp)
    nh = 2 * DIFF_HEADS
    tq = q_ref.shape[0]
    rows_h = 2 * tq

    q = q_ref[...]
    lane_head = lax.broadcasted_iota(jnp.int32, q.shape, 1) // DIFF_DH
    qs = jnp.concatenate([jnp.where(lane_head == hh, q, 0.0) for hh in range(nh)], axis=0).astype(BF16)

    if step == 0:
        m_scr[...] = jnp.full(m_scr.shape, -jnp.inf, F32)
        l_scr[...] = jnp.zeros(l_scr.shape, F32)
        acc_scr[...] = jnp.zeros(acc_scr.shape, F32)

    def update(s, values_of):
        m_old = m_scr[...]
        m_new = jnp.maximum(m_old, jnp.max(s, axis=-1, keepdims=True))
        alpha = jnp.exp2(m_old - m_new)
        p = jnp.exp2(s - m_new)
        l_scr[...] = alpha * l_scr[...] + jnp.sum(p, axis=-1, keepdims=True)
        pvs = [_mm(p[h * rows_h:(h + 1) * rows_h].astype(BF16), values_of(h)) for h in range(DIFF_HEADS)]
        acc_scr[...] = alpha * acc_scr[...] + jnp.concatenate(pvs, axis=0)
        m_scr[...] = m_new

    is_last = step == nsteps - 1

    def logits():
        kt = jnp.concatenate([kp[g][...].astype(BF16) for g in range(npg)], axis=1)
        s = _mm(qs, kt)
        if is_last:
            zeros = jnp.zeros((s.shape[0], s.shape[1] - PAGE_SIZE), F32)
            s = s + jnp.concatenate([zeros, d1_ref[...]], axis=1)
        return s

    def finish(s):
        update(s, lambda h: jnp.concatenate(
            [vp[g][pl.ds(h, PAGE_SIZE, stride=DIFF_HEADS), :].astype(BF16) for g in range(npg)], axis=0))
        if not is_last:
            return
        pad = jnp.zeros((PAGE_SIZE - tq, DIFF_VW), F32)
        kn = jnp.concatenate([kn_ref[...], pad], axis=0).astype(BF16)
        vn = jnp.concatenate([vn_ref[...], pad], axis=0).astype(BF16)
        update(_nt(qs, kn) + d0_ref[...], lambda h: vn[:, h * LANES:(h + 1) * LANES])
        o = acc_scr[...] / l_scr[...]
        lam = lam_ref[0]
        outs = []
        for h in range(DIFF_HEADS):
            oh = o[h * rows_h:h * rows_h + tq] - lam * o[h * rows_h + tq:(h + 1) * rows_h]
            ms = jnp.mean(oh * oh, axis=-1, keepdims=True)
            outs.append(oh * lax.rsqrt(ms + EPS) * dnw_ref[...] * out_scale)
        o_ref[...] = jnp.concatenate(outs, axis=1)

    return logits, finish


_MIX_CONSTS = ("w_out", "ln1_g", "ln1_b", "w_ffn_gate", "w_ffn_up", "w_ffn_down", "w_ple_gate", "w_ple_proj",
               "ln2_g", "ln2_b")


def _mix_head(oa_ref, ob_ref, oc_ref, x_ref, pe_ref, wo_ref, g1_ref, b1_ref, wpg_ref, wpp_ref):
    mix = _mm(oa_ref[...].astype(BF16), wo_ref[0:GLA_VW, :])
    mix = mix + _mm(ob_ref[...].astype(BF16), wo_ref[GLA_VW:GLA_VW + CONV_CH, :])
    mix = mix + _mm(oc_ref[...].astype(BF16), wo_ref[GLA_VW + CONV_CH:, :])
    x = _layer_norm(ALPHA * x_ref[...] + mix, g1_ref[...], b1_ref[...])
    xb = x.astype(BF16)
    ple = _sigmoid(_mm(xb, wpg_ref[...])) * _mm(pe_ref[...].astype(BF16), wpp_ref[...])
    return xb, ALPHA * x + ple


def _ffn_cols(xb, wg_ref, wu_ref, wd_ref, lo, hi):
    acc = None
    c = lo
    while c < hi:
        w = min(FFN_CHUNK, hi - c)
        gate = _mm(xb, wg_ref[:, c:c + w])
        hid = gate * _sigmoid(gate) * _mm(xb, wu_ref[:, c:c + w])
        part = _mm(hid.astype(BF16), wd_ref[c:c + w, :])
        acc = part if acc is None else acc + part
        c += w
    return acc


def _mix_ffn_body(oa_ref, ob_ref, oc_ref, x_ref, pe_ref, wo_ref, g1_ref, b1_ref, wg_ref, wu_ref, wd_ref, wpg_ref,
                  wpp_ref, g2_ref, b2_ref, y_ref):
    xb, base = _mix_head(oa_ref, ob_ref, oc_ref, x_ref, pe_ref, wo_ref, g1_ref, b1_ref, wpg_ref, wpp_ref)
    acc = base + _ffn_cols(xb, wg_ref, wu_ref, wd_ref, 0, wg_ref.shape[1])
    y_ref[...] = _layer_norm(acc, g2_ref[...], b2_ref[...])


def _mix_ffn(oa, ob, oc, x2, pe_all, layer, lw):
    m, d = x2.shape
    tm = min(MIX_ROWS, m)
    nt = m // tm
    consts = [lw[n] for n in _MIX_CONSTS]

    def rows(n):
        return pl.BlockSpec((tm, n), lambda i: (i, 0))

    return pl.pallas_call(
        _mix_ffn_body,
        grid=(nt,),
        in_specs=[rows(GLA_VW), rows(CONV_CH), rows(DIFF_VW), rows(d),
                  pl.BlockSpec((tm, pe_all.shape[1]), lambda i: (layer * nt + i, 0))]
                 + [_const_spec(a.shape) for a in consts],
        out_specs=rows(d),
        out_shape=jax.ShapeDtypeStruct((m, d), F32),
        compiler_params=_params("parallel"),
        name="mix_ffn",
    )(oa, ob, oc, x2, pe_all, *consts)


def _sattn_body(pt_ref, lam_ref, q_ref, kn_ref, vn_ref, d1_ref, d0_ref, dnw_ref, *rest, npg, nsteps, out_scale):
    del pt_ref
    kp = rest[:npg]
    vp = rest[npg:2 * npg]
    o_ref = rest[2 * npg]
    m_scr, l_scr, acc_scr = rest[2 * npg + 1:]
    p_id = pl.program_id(1)
    for step in range(nsteps):
        @pl.when(p_id == step)
        def _(step=step):
            logits, finish = _sattn_phases(step, nsteps, lam_ref, q_ref, kn_ref, vn_ref, d1_ref, d0_ref, dnw_ref,
                                           kp, vp, o_ref, m_scr, l_scr, acc_scr, out_scale)
            finish(logits())


def _attn_sample(dq, dk, dv, cache_kt, cache_v2, page_table, layer, bias, lam, dnw, out_scale):
    nb, n_pages = page_table.shape
    tq = dq.shape[0] // nb
    npg = PAGES_PER_STEP
    nsteps = n_pages // npg
    assert n_pages % npg == 0 and cache_kt.shape[-1] == PAGE_SIZE and tq == SUBLANES
    nh = 2 * DIFF_HEADS
    t = bias.shape[-1]
    d1 = bias[1, :, 0:tq, t - PAGE_SIZE:t].reshape(nh * tq, PAGE_SIZE)
    d0 = jnp.concatenate([bias[0, :, 0:tq, 0:tq], jnp.full((nh, tq, PAGE_SIZE - tq), -jnp.inf, F32)],
                         axis=-1).reshape(nh * tq, PAGE_SIZE)

    def page_spec(g):
        return pl.BlockSpec((None, None, DIFF_VW, PAGE_SIZE),
                            lambda b, p, pt: (layer, pt[b * n_pages + p * npg + g], 0, 0))

    def rows_spec():
        return pl.BlockSpec((tq, DIFF_VW), lambda b, p, pt: (b, 0))

    def full_spec(a):
        nd = a.ndim
        return pl.BlockSpec(a.shape, lambda b, p, pt: (0,) * nd)

    grid_spec = pltpu.PrefetchScalarGridSpec(
        num_scalar_prefetch=1,
        grid=(nb, nsteps),
        in_specs=[pl.BlockSpec(memory_space=pltpu.SMEM), rows_spec(), rows_spec(), rows_spec(),
                  full_spec(d1), full_spec(d0), full_spec(dnw)]
                 + [page_spec(g) for g in range(npg)] + [page_spec(g) for g in range(npg)],
        out_specs=rows_spec(),
        scratch_shapes=[pltpu.VMEM((nh * tq, 1), F32), pltpu.VMEM((nh * tq, 1), F32),
                        pltpu.VMEM((nh * tq, LANES), F32)],
    )
    return pl.pallas_call(
        functools.partial(_sattn_body, npg=npg, nsteps=nsteps, out_scale=out_scale),
        grid_spec=grid_spec,
        out_shape=jax.ShapeDtypeStruct((nb * tq, DIFF_VW), F32),
        compiler_params=_params("parallel", "arbitrary"),
        name="diff_attn_sample",
    )(page_table.reshape(-1), lam, dq, dk, dv, d1, d0, dnw, *([cache_kt] * npg), *([cache_v2] * npg))


def _row(a):
    return a.reshape(1, -1)


def _state_to_t(s):
    nb = s.shape[0]
    eye = jnp.eye(GLA_HEADS, dtype=s.dtype)
    return jnp.einsum("bhde,hg->bhegd", s, eye).reshape(nb, GLA_VW, GLA_QK)


def _state_from_t(s_t):
    nb = s_t.shape[0]
    blocks = s_t.reshape(nb, GLA_HEADS, GLA_DV, GLA_HEADS, GLA_DK)
    diag = jnp.stack([blocks[:, h, :, h, :] for h in range(GLA_HEADS)], axis=1)
    return diag.transpose(0, 1, 3, 2)


def _token_mixers(x2, nb, t, lw, s0, buf, transposed_k):
    gq, gk, gv, gs, lg, u, dq, dv, *kk = _in_proj(x2, lw["w_in"], lw["wk_t"], lw["wg"], lw["bg"], nb, transposed_k)
    o_a, s_t = _gla(gq, gk, gv, lg, gs, lw["gla_nw"], _state_to_t(s0), nb, t)
    o_b, nbuf = _conv(u, buf, lw["conv_w"], lw["conv_b"], lw["conv_g"], lw["conv_beta"], nb, t)
    return o_a, o_b, dq, dv, kk, _state_from_t(s_t), nbuf


def _prep_w_in(w):
    sizes = (GLA_QK, GLA_QK, GLA_VW, GLA_VW, GLA_GATE_RANK, CONV_CH, CONV_CH, DIFF_VW, DIFF_VW, DIFF_VW)
    parts, s = [], 0
    for n in sizes:
        parts.append(w[:, s:s + n])
        s += n
    wk_t = parts[8].T.astype(BF16)
    parts[4] = jnp.pad(parts[4], ((0, 0), (0, LANES - GLA_GATE_RANK)))
    return jnp.concatenate(parts, axis=1).astype(BF16), wk_t


def _layer_weights(i, w_in, gla_w_gate_up, gla_b_gate, gla_norm_w, conv_w, conv_b, conv_ln_g, conv_ln_b, diff_lq1,
                   diff_lk1, diff_lq2, diff_lk2, diff_norm_w, w_out, ln1_g, ln1_b, w_ffn_gate, w_ffn_up, w_ffn_down,
                   w_ple_gate, w_ple_proj, ln2_g, ln2_b):
    lam_init = 0.8 - 0.6 * math.exp(-0.3 * i)
    lam = (jnp.exp(jnp.sum(diff_lq1[i] * diff_lk1[i])) - jnp.exp(jnp.sum(diff_lq2[i] * diff_lk2[i]))
           + lam_init).reshape(1).astype(F32)
    w_in_p, wk_t = _prep_w_in(w_in[i])
    return dict(
        w_in=w_in_p, wk_t=wk_t,
        wg=jnp.pad(gla_w_gate_up[i], ((0, LANES - GLA_GATE_RANK), (0, 0))).astype(BF16),
        bg=_row(gla_b_gate[i]),
        gla_nw=_row(jnp.tile(gla_norm_w[i], GLA_HEADS)),
        conv_w=conv_w[i], conv_b=_row(conv_b[i]), conv_g=_row(conv_ln_g[i]), conv_beta=_row(conv_ln_b[i]),
        lam=lam, dnw=_row(diff_norm_w[i]),
        w_out=w_out[i].astype(BF16), ln1_g=_row(ln1_g[i]), ln1_b=_row(ln1_b[i]),
        w_ffn_gate=w_ffn_gate[i].astype(BF16), w_ffn_up=w_ffn_up[i].astype(BF16),
        w_ffn_down=w_ffn_down[i].astype(BF16), w_ple_gate=w_ple_gate[i].astype(BF16),
        w_ple_proj=w_ple_proj[i].astype(BF16), ln2_g=_row(ln2_g[i]), ln2_b=_row(ln2_b[i]),
    )


def kernel(x_prompt, x_sample, cache_k, cache_v, state_gla, state_conv, page_table, p_prompt, p_sample, w_in, gla_w_gate_up, gla_b_gate, gla_norm_w, conv_w, conv_b, conv_ln_g, conv_ln_b, diff_lq1, diff_lk1, diff_lq2, diff_lk2, diff_norm_w, rel_bias, w_out, ln1_g, ln1_b, w_ffn_gate, w_ffn_up, w_ffn_down, w_ple_gate, w_ple_proj, ln2_g, ln2_b):
    nbp, s, d = x_prompt.shape
    nbs, ts, _ = x_sample.shape
    depth, n_pool = cache_k.shape[:2]
    assert depth == DEPTH and cache_k.shape[2] == PAGE_SIZE
    cache_kt = jnp.transpose(cache_k, (0, 1, 3, 4, 2)).reshape(depth, n_pool, DIFF_VW, PAGE_SIZE)
    cache_v2 = cache_v.reshape(depth, n_pool, PAGE_SIZE * DIFF_HEADS, 2 * DIFF_DH)
    bias = _bias_tiles(rel_bias, ATTN_TILE)

    pe_prompt = p_prompt.reshape(depth * nbp * s, -1)
    pe_sample = p_sample.reshape(depth * nbs * ts, -1)
    yp = x_prompt.reshape(nbp * s, d)
    ys = x_sample.reshape(nbs * ts, d)
    zero_state = jnp.zeros((nbp, GLA_HEADS, GLA_DK, GLA_DV), F32)
    zero_buf = jnp.zeros((nbp, CONV_WIDTH - 1, CONV_CH), F32)
    outs = [[] for _ in range(8)]
    for i in range(depth):
        lw = _layer_weights(i, w_in, gla_w_gate_up, gla_b_gate, gla_norm_w, conv_w, conv_b, conv_ln_g, conv_ln_b,
                            diff_lq1, diff_lk1, diff_lq2, diff_lk2, diff_norm_w, w_out, ln1_g, ln1_b, w_ffn_gate,
                            w_ffn_up, w_ffn_down, w_ple_gate, w_ple_proj, ln2_g, ln2_b)

        out_scale = 1.0 - (0.8 - 0.6 * math.exp(-0.3 * i))
        pa, pb, q16, v_rows, (kt_i, kt16, v16), s_i, c_i = _token_mixers(yp, nbp, s, lw, zero_state, zero_buf, True)
        pc = _attn_prompt(q16, kt16, v16, bias, lw["lam"], lw["dnw"], nbp, s, out_scale)
        outs[0].append(kt_i.reshape(nbp, 2 * DIFF_HEADS, DIFF_DH, s).transpose(0, 3, 1, 2))
        outs[1].append(v_rows.reshape(nbp, s, DIFF_HEADS, 2 * DIFF_DH))
        outs[2].append(s_i)
        outs[3].append(c_i)
        sa, sb, dq, dv, (dk,), s_i, c_i = _token_mixers(ys, nbs, ts, lw, state_gla[i], state_conv[i], False)
        outs[4].append(dk.reshape(nbs, ts, 2 * DIFF_HEADS, DIFF_DH))
        outs[5].append(dv.reshape(nbs, ts, DIFF_HEADS, 2 * DIFF_DH))
        outs[6].append(s_i)
        outs[7].append(c_i)
        sc = _attn_sample(dq, dk, dv, cache_kt, cache_v2, page_table, i, bias, lw["lam"], lw["dnw"], out_scale)
        yp = _mix_ffn(pa, pb, pc, yp, pe_prompt, i, lw)
        ys = _mix_ffn(sa, sb, sc, ys, pe_sample, i, lw)
    return (yp.reshape(nbp, s, d), ys.reshape(nbs, ts, d)) + tuple(jnp.stack(o) for o in outs)
```

```python
import functools
import math

import jax
import jax.numpy as jnp
from jax import lax
from jax.experimental import pallas as pl
from jax.experimental.pallas import tpu as pltpu

F32 = jnp.float32
BF16 = jnp.bfloat16

GLA_HEADS = 4
GLA_DK = 32
GLA_DV = 64
GLA_QK = GLA_HEADS * GLA_DK
GLA_VW = GLA_HEADS * GLA_DV
GLA_GATE_RANK = 16
GLA_TAU = 16.0
CONV_CH = 256
CONV_WIDTH = 31
DIFF_HEADS = 4
DIFF_DH = 64
DIFF_VW = 2 * DIFF_HEADS * DIFF_DH
REL_BUCKETS = 32
REL_MAX_DIST = 128
PAGE_SIZE = 128
DEPTH = 2
ALPHA = (2 * DEPTH) ** 0.25
EPS = 1e-5
LOG2E = math.log2(math.e)

LANES = 128
SUBLANES = 8
GLA_BLOCK = 16
GLA_SUBTILE = 128
ATTN_TILE = 256
MIX_ROWS = 512
FFN_CHUNK = 256
PAGES_PER_STEP = 16
CONV_HALO = 32
VMEM_LIMIT = 56 * 1024 * 1024

_IN_OFF = {}
_o = 0
for _name, _n in (("gq", 128), ("gk", 128), ("gv", 256), ("gg", 256), ("glr", 128), ("ca", 256), ("cg", 256),
                  ("dq", 512), ("dk", 512), ("dv", 512)):
    _IN_OFF[_name] = (_o, _n)
    _o += _n
N_IN_PAD = _o


def _params(*sem):
    return pltpu.CompilerParams(dimension_semantics=sem, vmem_limit_bytes=VMEM_LIMIT)


def _const_spec(shape):
    nd = len(shape)
    return pl.BlockSpec(shape, lambda *_: (0,) * nd, pipeline_mode=pl.Buffered(1))


def _nt(a, b):
    return lax.dot_general(a, b, (((1,), (1,)), ((), ())), preferred_element_type=F32)


def _mm(a, b):
    return jnp.dot(a, b, preferred_element_type=F32)


def _mm_split(a, b):
    hi = a.astype(BF16)
    lo = (a - hi.astype(F32)).astype(BF16)
    return _mm(hi, b) + _mm(lo, b)


def _sigmoid(x):
    return 1.0 / (1.0 + jnp.exp(-x))


def _layer_norm(x, g, b):
    mu = jnp.mean(x, axis=-1, keepdims=True)
    xc = x - mu
    var = jnp.mean(xc * xc, axis=-1, keepdims=True)
    return xc * lax.rsqrt(var + EPS) * g + b


def _fold_lanes(x, op):
    acc = x[:, 0:LANES]
    for c in range(1, x.shape[1] // LANES):
        acc = op(acc, x[:, c * LANES:(c + 1) * LANES])
    return acc


def _inproj_body(x_ref, w_ref, wkt_ref, wg_ref, bg_ref, gq_ref, gk_ref, gv_ref, gs_ref, lg_ref, u_ref, dq_ref,
                 dv_ref, *k_refs, transposed_k):
    x = x_ref[...].astype(BF16)

    def proj(name):
        lo, n = _IN_OFF[name]
        return _mm(x, w_ref[:, lo:lo + n])

    gq_ref[...] = proj("gq") * GLA_DK ** -0.5
    gk_ref[...] = proj("gk")
    gv_ref[...] = proj("gv")
    gg = proj("gg")
    gs_ref[...] = gg * _sigmoid(gg)
    z = _mm(proj("glr").astype(BF16), wg_ref[...]) + bg_ref[...]
    lg_ref[...] = (jnp.minimum(z, 0.0) - jnp.log(1.0 + jnp.exp(-jnp.abs(z)))) * (1.0 / GLA_TAU)
    u_ref[...] = proj("ca") * _sigmoid(proj("cg"))
    dq_ref[...] = (proj("dq") * (DIFF_DH ** -0.5 * LOG2E)).astype(dq_ref.dtype)
    dv = proj("dv")
    if transposed_k:
        kt_ref, kt16_ref, v16_ref = k_refs
        tm = x.shape[0]
        for h in range(DIFF_HEADS):
            dv_ref[pl.ds(h, tm, stride=DIFF_HEADS), :] = dv[:, h * LANES:(h + 1) * LANES]
        kt = _nt(wkt_ref[...], x)
        kt_ref[0] = kt
        kt16_ref[0] = kt.astype(BF16)
        v16_ref[...] = dv.astype(BF16)
    else:
        dv_ref[...] = dv
        k_refs[0][...] = proj("dk")


def _in_proj(x2, w_in_p, wk_t, wg_p, bg, nb, transposed_k):
    m, d = x2.shape
    t = m // nb
    tm = min(512, t if transposed_k else m)
    nj = t // tm
    widths = (GLA_QK, GLA_QK, GLA_VW, GLA_VW, GLA_QK, CONV_CH, DIFF_VW)
    dtypes = [F32] * 6 + [BF16 if transposed_k else F32]
    out_specs = [pl.BlockSpec((tm, n), lambda i: (i, 0)) for n in widths]
    out_shape = [jax.ShapeDtypeStruct((m, n), dt) for n, dt in zip(widths, dtypes)]
    if transposed_k:
        out_specs.append(pl.BlockSpec((tm * DIFF_HEADS, LANES), lambda i: (i, 0)))
        out_shape.append(jax.ShapeDtypeStruct((m * DIFF_HEADS, LANES), F32))
        kt_spec = pl.BlockSpec((1, DIFF_VW, tm), lambda i: (i // nj, 0, i % nj))
        out_specs += [kt_spec, kt_spec, pl.BlockSpec((tm, DIFF_VW), lambda i: (i, 0))]
        out_shape += [jax.ShapeDtypeStruct((nb, DIFF_VW, t), F32), jax.ShapeDtypeStruct((nb, DIFF_VW, t), BF16),
                      jax.ShapeDtypeStruct((m, DIFF_VW), BF16)]
    else:
        out_specs += [pl.BlockSpec((tm, DIFF_VW), lambda i: (i, 0))] * 2
        out_shape += [jax.ShapeDtypeStruct((m, DIFF_VW), F32)] * 2
    return pl.pallas_call(
        functools.partial(_inproj_body, transposed_k=transposed_k),
        grid=(m // tm,),
        in_specs=[pl.BlockSpec((tm, d), lambda i: (i, 0)),
                  _const_spec(w_in_p.shape), _const_spec(wk_t.shape), _const_spec(wg_p.shape), _const_spec(bg.shape)],
        out_specs=out_specs,
        out_shape=out_shape,
        compiler_params=_params("parallel"),
        name="in_proj",
    )(x2, w_in_p, wk_t, wg_p, bg)


def _gla_body(q_ref, k_ref, v_ref, lg_ref, gs_ref, nw_ref, s0_ref, o_ref, sout_ref, s_scr, *, cb):
    j = pl.program_id(1)

    @pl.when(j == 0)
    def _():
        s_scr[...] = s0_ref[0]

    r = q_ref.shape[0]
    rs = min(r, GLA_SUBTILE)
    row = lax.broadcasted_iota(jnp.int32, (rs, 1), 0) % cb

    same_head = (lax.broadcasted_iota(jnp.int32, (GLA_QK, GLA_VW), 0) // GLA_DK
                 == lax.broadcasted_iota(jnp.int32, (GLA_QK, GLA_VW), 1) // GLA_DV)
    expand = same_head.astype(BF16)
    mask_t = (lax.broadcasted_iota(jnp.int32, (GLA_VW, GLA_QK), 0) // GLA_DV
              == lax.broadcasted_iota(jnp.int32, (GLA_VW, GLA_QK), 1) // GLA_DK).astype(F32)
    grp = (lax.broadcasted_iota(jnp.int32, (GLA_VW, GLA_VW), 0) // GLA_DV
           == lax.broadcasted_iota(jnp.int32, (GLA_VW, GLA_VW), 1) // GLA_DV)
    head_mean = jnp.where(grp, 1.0 / GLA_DV, 0.0).astype(BF16)

    st = s_scr[...]
    for sub in range(r // rs):
        rows = pl.ds(sub * rs, rs)
        q = q_ref[rows, :]
        k = k_ref[rows, :]
        v = v_ref[rows, :]

        b = lg_ref[rows, :]
        s = 1
        while s < cb:
            b = b + jnp.where(row >= s, pltpu.roll(b, s, 0), 0.0)
            s *= 2

        o = _mm((q * k).astype(BF16), expand) * v
        for delta in range(1, cb):
            ks = pltpu.roll(k, delta, 0)
            bs = pltpu.roll(b, delta, 0)
            vs = pltpu.roll(v, delta, 0)
            p = jnp.where(row >= delta, q * ks * jnp.exp(b - bs), 0.0)
            o = o + _mm(p.astype(BF16), expand) * vs

        inter = []
        for t in range(rs // cb):
            sl = slice(t * cb, (t + 1) * cb)
            bt = b[sl]
            bend = bt[cb - 1:cb]
            qe = (q[sl] * jnp.exp(bt)).astype(BF16)
            inter.append(_nt(qe, st.astype(BF16)))
            ke = k[sl] * jnp.exp(bend - bt)
            vt = v[sl]
            if cb < 16:
                ke = jnp.concatenate([ke, jnp.zeros((16 - cb, GLA_QK), F32)], axis=0)
                vt = jnp.concatenate([vt, jnp.zeros((16 - cb, GLA_VW), F32)], axis=0)
            kv = lax.dot_general(vt.astype(BF16), ke.astype(BF16), (((0,), (0,)), ((), ())),
                                 preferred_element_type=F32)
            st = st * jnp.exp(bend) + kv * mask_t
        o = o + (inter[0] if len(inter) == 1 else jnp.concatenate(inter, axis=0))

        mean_sq = _mm_split(o * o, head_mean)
        o_ref[rows, :] = o * lax.rsqrt(mean_sq + EPS) * nw_ref[...] * gs_ref[rows, :]
    s_scr[...] = st

    @pl.when(j == pl.num_programs(1) - 1)
    def _():
        sout_ref[0] = st


def _gla(gq, gk, gv, lg, gs, nw, s0_t, nb, t):
    cb = GLA_BLOCK if t % GLA_BLOCK == 0 else t
    r = min(t, 256)
    nj = t // r
    m = nb * t

    def rows(n):
        return pl.BlockSpec((r, n), lambda b, j: (b * nj + j, 0))

    return pl.pallas_call(
        functools.partial(_gla_body, cb=cb),
        grid=(nb, nj),
        in_specs=[rows(GLA_QK), rows(GLA_QK), rows(GLA_VW), rows(GLA_QK), rows(GLA_VW),
                  _const_spec(nw.shape),
                  pl.BlockSpec((1, GLA_VW, GLA_QK), lambda b, j: (b, 0, 0))],
        out_specs=[rows(GLA_VW), pl.BlockSpec((1, GLA_VW, GLA_QK), lambda b, j: (b, 0, 0))],
        out_shape=[jax.ShapeDtypeStruct((m, GLA_VW), F32), jax.ShapeDtypeStruct((nb, GLA_VW, GLA_QK), F32)],
        scratch_shapes=[pltpu.VMEM((GLA_VW, GLA_QK), F32)],
        compiler_params=_params("parallel", "arbitrary"),
        name="gla",
    )(gq, gk, gv, lg, gs, nw, s0_t)


def _conv_body(u_ref, buf_ref, cw_ref, cb_ref, g_ref, beta_ref, y_ref, nbuf_ref, seq_scr):
    j = pl.program_id(1)
    r = u_ref.shape[0]
    hist = CONV_WIDTH - 1

    @pl.when(j == 0)
    def _():
        seq_scr[0:CONV_HALO - hist, :] = jnp.zeros((CONV_HALO - hist, CONV_CH), F32)
        seq_scr[CONV_HALO - hist:CONV_HALO, :] = buf_ref[0]

    @pl.when(j > 0)
    def _():
        seq_scr[0:CONV_HALO, :] = seq_scr[r:r + CONV_HALO, :]

    seq_scr[CONV_HALO:CONV_HALO + r, :] = u_ref[...]

    window = seq_scr[...]
    rows = window.shape[0]
    acc = jnp.zeros((r, CONV_CH), F32) + cb_ref[...]
    for rho in range(SUBLANES):
        lo = CONV_HALO - hist + rho
        shifted = pltpu.roll(window, rows - lo, 0)
        for k in range(-(-CONV_WIDTH // SUBLANES)):
            w = SUBLANES * k + rho
            if w < CONV_WIDTH:
                acc = acc + shifted[SUBLANES * k:SUBLANES * k + r, :] * cw_ref[w:w + 1, :]
    y = _layer_norm(acc, g_ref[...], beta_ref[...])
    y_ref[...] = y * _sigmoid(y)

    @pl.when(j == pl.num_programs(1) - 1)
    def _():
        nbuf_ref[0] = seq_scr[r + CONV_HALO - hist:r + CONV_HALO, :]


def _conv(u, buf, cw, cb, g, beta, nb, t):
    r = min(t, 512)
    nj = t // r
    assert nj == 1 or r >= CONV_HALO
    hist = CONV_WIDTH - 1
    return pl.pallas_call(
        _conv_body,
        grid=(nb, nj),
        in_specs=[pl.BlockSpec((r, CONV_CH), lambda b, j: (b * nj + j, 0)),
                  pl.BlockSpec((1, hist, CONV_CH), lambda b, j: (b, 0, 0)),
                  _const_spec(cw.shape), _const_spec(cb.shape), _const_spec(g.shape), _const_spec(beta.shape)],
        out_specs=[pl.BlockSpec((r, CONV_CH), lambda b, j: (b * nj + j, 0)),
                   pl.BlockSpec((1, hist, CONV_CH), lambda b, j: (b, 0, 0))],
        out_shape=[jax.ShapeDtypeStruct((nb * t, CONV_CH), F32), jax.ShapeDtypeStruct((nb, hist, CONV_CH), F32)],
        scratch_shapes=[pltpu.VMEM((r + CONV_HALO, CONV_CH), F32)],
        compiler_params=_params("parallel", "arbitrary"),
        name="conv",
    )(u, buf, cw, cb, g, beta)


def _rel_bucket(dist):
    n = jnp.maximum(dist, 0)
    max_exact = REL_BUCKETS // 2
    nf = jnp.maximum(n, 1).astype(F32)
    large = max_exact + (jnp.log(nf / max_exact) / math.log(REL_MAX_DIST / max_exact)
                         * (REL_BUCKETS - max_exact)).astype(jnp.int32)
    large = jnp.minimum(large, REL_BUCKETS - 1)
    return jnp.where(n < max_exact, n, large)


def _bias_body(rb_ref, idx_ref, o_ref):
    h = pl.program_id(1)
    idx = idx_ref[0]
    acc = jnp.full(idx.shape, -jnp.inf, F32)
    for bucket in range(REL_BUCKETS):
        acc = jnp.where(idx == bucket, rb_ref[bucket, h], acc)
    o_ref[0, 0] = (acc - rb_ref[REL_BUCKETS - 1, h]) * LOG2E


def _bias_tiles(rel_bias, t):
    assert t >= REL_MAX_DIST
    ii = jnp.arange(t, dtype=jnp.int32)[:, None]
    jj = jnp.arange(t, dtype=jnp.int32)[None, :]
    idx = jnp.stack([jnp.where(ii >= jj, _rel_bucket(ii - jj), -1), _rel_bucket(t + ii - jj)])
    nh = rel_bias.shape[1]
    return pl.pallas_call(
        _bias_body,
        grid=(2, nh),
        in_specs=[pl.BlockSpec(memory_space=pltpu.SMEM),
                  pl.BlockSpec((1, t, t), lambda r, h: (r, 0, 0))],
        out_specs=pl.BlockSpec((1, 1, t, t), lambda r, h: (r, h, 0, 0)),
        out_shape=jax.ShapeDtypeStruct((2, nh, t, t), F32),
        compiler_params=_params("arbitrary", "arbitrary"),
        name="rel_bias_tiles",
    )(rel_bias, idx)


def _attn_body(lam_ref, q_ref, kt_ref, v_ref, bias_ref, dnw_ref, o_ref, q_scr, s_scr, m_scr, l_scr, acc_scr, *,
               out_scale):
    i = pl.program_id(1)
    t = q_ref.shape[0]
    nmaps = 2 * DIFF_HEADS
    lane = lax.broadcasted_iota(jnp.int32, (t, LANES), 1)
    for n in range(nmaps):
        qh = q_ref[:, (n // 2) * LANES:(n // 2 + 1) * LANES]
        keep = (lane < DIFF_DH) if n % 2 == 0 else (lane >= DIFF_DH)
        q_scr[n] = jnp.where(keep, qh, jnp.zeros_like(qh))
    m_scr[...] = jnp.full(m_scr.shape, -jnp.inf, F32)
    l_scr[...] = jnp.zeros(l_scr.shape, F32)
    acc_scr[...] = jnp.zeros(acc_scr.shape, F32)

    def tile(j, which, ntile=1):
        width = ntile * t
        keys = pl.ds(pl.multiple_of(j * t, t), width)
        for n in range(nmaps):
            s = _mm(q_scr[n], kt_ref[0, (n // 2) * LANES:(n // 2 + 1) * LANES, keys])
            if which == "near":
                s = s + jnp.concatenate([bias_ref[1, n], bias_ref[0, n]], axis=1)
            elif which is not None:
                s = s + bias_ref[which, n]
            s_scr[n, :, 0:width] = s
        for n in range(nmaps):
            s = s_scr[n, :, 0:width]
            m_old = m_scr[n]
            row_max = jnp.max(_fold_lanes(s, jnp.maximum), axis=-1, keepdims=True)
            m_new = jnp.maximum(m_old, jnp.broadcast_to(row_max, (t, LANES)))
            alpha = jnp.exp2(m_old - m_new)
            ps = [jnp.exp2(s[:, c * LANES:(c + 1) * LANES] - m_new) for c in range(width // LANES)]
            part = ps[0]
            for pc in ps[1:]:
                part = part + pc
            l_scr[n] = alpha * l_scr[n] + part
            pv = _mm(jnp.concatenate(ps, axis=1).astype(BF16), v_ref[keys, (n // 2) * LANES:(n // 2 + 1) * LANES])
            acc_scr[n] = alpha * acc_scr[n] + pv
            m_scr[n] = m_new

    nfar = jnp.maximum(i - 1, 0)

    def far_pair(c, carry):
        tile(2 * c, None, 2)
        return carry

    lax.fori_loop(0, nfar // 2, far_pair, 0)

    @pl.when(nfar % 2 == 1)
    def _():
        tile(nfar - 1, None)

    @pl.when(i >= 1)
    def _():
        tile(i - 1, "near", 2)

    @pl.when(i == 0)
    def _():
        tile(0, 0)

    lam = lam_ref[0]
    outs = []
    for hp in range(DIFF_HEADS):
        l0 = jnp.sum(l_scr[2 * hp], axis=-1, keepdims=True)
        l1 = jnp.sum(l_scr[2 * hp + 1], axis=-1, keepdims=True)
        o = acc_scr[2 * hp] / l0 - lam * (acc_scr[2 * hp + 1] / l1)
        ms_o = jnp.mean(o * o, axis=-1, keepdims=True)
        outs.append(o * lax.rsqrt(ms_o + EPS) * dnw_ref[...] * out_scale)
    o_ref[...] = jnp.concatenate(outs, axis=1)


def _attn_prompt(q16, kt16, v16, bias, lam, dnw, nb, s, out_scale):
    t = bias.shape[-1]
    nq = s // t
    nmaps = 2 * DIFF_HEADS
    return pl.pallas_call(
        functools.partial(_attn_body, out_scale=out_scale),
        grid=(nb, nq),
        in_specs=[pl.BlockSpec(memory_space=pltpu.SMEM),
                  pl.BlockSpec((t, DIFF_VW), lambda b, i: (b * nq + i, 0)),
                  pl.BlockSpec((1, DIFF_VW, s), lambda b, i: (b, 0, 0)),
                  pl.BlockSpec((s, DIFF_VW), lambda b, i: (b, 0)),
                  _const_spec(bias.shape), _const_spec(dnw.shape)],
        out_specs=pl.BlockSpec((t, DIFF_VW), lambda b, i: (b * nq + i, 0)),
        out_shape=jax.ShapeDtypeStruct((nb * s, DIFF_VW), F32),
        scratch_shapes=[pltpu.VMEM((nmaps, t, LANES), BF16), pltpu.VMEM((nmaps, t, 2 * t), F32),
                        pltpu.VMEM((nmaps, t, LANES), F32), pltpu.VMEM((nmaps, t, LANES), F32),
                        pltpu.VMEM((nmaps, t, LANES), F32)],
        compiler_params=_params("parallel", "arbitrary"),
        name="diff_attn_prompt",
    )(lam, q16, kt16, v16, bias, dnw)


def _sattn_phases(step, nsteps, lam_ref, q_ref, kn_ref, vn_ref, d1_ref, d0_ref, dnw_ref, kp, vp, o_ref, m_scr, l_scr,
                  acc_scr, out_scale):
    npg = len(kp)
    nh = 2 * DIFF_HEADS
    tq = q_ref.shape[0]
    rows_h = 2 * tq

    q = q_ref[...]
    lane_head = lax.broadcasted_iota(jnp.int32, q.shape, 1) // DIFF_DH
    qs = jnp.concatenate([jnp.where(lane_head == hh, q, 0.0) for hh in range(nh)], axis=0).astype(BF16)

    if step == 0:
        m_scr[...] = jnp.full(m_scr.shape, -jnp.inf, F32)
        l_scr[...] = jnp.zeros(l_scr.shape, F32)
        acc_scr[...] = jnp.zeros(acc_scr.shape, F32)

    def update(s, values_of):
        m_old = m_scr[...]
        m_new = jnp.maximum(m_old, jnp.max(s, axis=-1, keepdims=True))
        alpha = jnp.exp2(m_old - m_new)
        p = jnp.exp2(s - m_new)
        l_scr[...] = alpha * l_scr[...] + jnp.sum(p, axis=-1, keepdims=True)
        pvs = [_mm(p[h * rows_h:(h + 1) * rows_h].astype(BF16), values_of(h)) for h in range(DIFF_HEADS)]
        acc_scr[...] = alpha * acc_scr[...] + jnp.concatenate(pvs, axis=0)
        m_scr[...] = m_new

    is_last = step == nsteps - 1

    def logits():
        kt = jnp.concatenate([kp[g][...].astype(BF16) for g in range(npg)], axis=1)
        s = _mm(qs, kt)
        if is_last:
            zeros = jnp.zeros((s.shape[0], s.shape[1] - PAGE_SIZE), F32)
            s = s + jnp.concatenate([zeros, d1_ref[...]], axis=1)
        return s

    def finish(s):
        update(s, lambda h: jnp.concatenate(
            [vp[g][pl.ds(h, PAGE_SIZE, stride=DIFF_HEADS), :].astype(BF16) for g in range(npg)], axis=0))
        if not is_last:
            return
        pad = jnp.zeros((PAGE_SIZE - tq, DIFF_VW), F32)
        kn = jnp.concatenate([kn_ref[...], pad], axis=0).astype(BF16)
        vn = jnp.concatenate([vn_ref[...], pad], axis=0).astype(BF16)
        update(_nt(qs, kn) + d0_ref[...], lambda h: vn[:, h * LANES:(h + 1) * LANES])
        o = acc_scr[...] / l_scr[...]
        lam = lam_ref[0]
        outs = []
        for h in range(DIFF_HEADS):
            oh = o[h * rows_h:h * rows_h + tq] - lam * o[h * rows_h + tq:(h + 1) * rows_h]
            ms = jnp.mean(oh * oh, axis=-1, keepdims=True)
            outs.append(oh * lax.rsqrt(ms + EPS) * dnw_ref[...] * out_scale)
        o_ref[...] = jnp.concatenate(outs, axis=1)

    return logits, finish


_MIX_CONSTS = ("w_out", "ln1_g", "ln1_b", "w_ffn_gate", "w_ffn_up", "w_ffn_down", "w_ple_gate", "w_ple_proj",
               "ln2_g", "ln2_b")


def _mix_head(oa_ref, ob_ref, oc_ref, x_ref, pe_ref, wo_ref, g1_ref, b1_ref, wpg_ref, wpp_ref):
    mix = _mm(oa_ref[...].astype(BF16), wo_ref[0:GLA_VW, :])
    mix = mix + _mm(ob_ref[...].astype(BF16), wo_ref[GLA_VW:GLA_VW + CONV_CH, :])
    mix = mix + _mm(oc_ref[...].astype(BF16), wo_ref[GLA_VW + CONV_CH:, :])
    x = _layer_norm(ALPHA * x_ref[...] + mix, g1_ref[...], b1_ref[...])
    xb = x.astype(BF16)
    ple = _sigmoid(_mm(xb, wpg_ref[...])) * _mm(pe_ref[...].astype(BF16), wpp_ref[...])
    return xb, ALPHA * x + ple


def _ffn_cols(xb, wg_ref, wu_ref, wd_ref, lo, hi):
    acc = None
    c = lo
    while c < hi:
        w = min(FFN_CHUNK, hi - c)
        gate = _mm(xb, wg_ref[:, c:c + w])
        hid = gate * _sigmoid(gate) * _mm(xb, wu_ref[:, c:c + w])
        part = _mm(hid.astype(BF16), wd_ref[c:c + w, :])
        acc = part if acc is None else acc + part
        c += w
    return acc


def _mix_ffn_body(oa_ref, ob_ref, oc_ref, x_ref, pe_ref, wo_ref, g1_ref, b1_ref, wg_ref, wu_ref, wd_ref, wpg_ref,
                  wpp_ref, g2_ref, b2_ref, y_ref):
    xb, base = _mix_head(oa_ref, ob_ref, oc_ref, x_ref, pe_ref, wo_ref, g1_ref, b1_ref, wpg_ref, wpp_ref)
    acc = base + _ffn_cols(xb, wg_ref, wu_ref, wd_ref, 0, wg_ref.shape[1])
    y_ref[...] = _layer_norm(acc, g2_ref[...], b2_ref[...])


def _mix_ffn(oa, ob, oc, x2, pe_all, layer, lw):
    m, d = x2.shape
    tm = min(MIX_ROWS, m)
    nt = m // tm
    consts = [lw[n] for n in _MIX_CONSTS]

    def rows(n):
        return pl.BlockSpec((tm, n), lambda i: (i, 0))

    return pl.pallas_call(
        _mix_ffn_body,
        grid=(nt,),
        in_specs=[rows(GLA_VW), rows(CONV_CH), rows(DIFF_VW), rows(d),
                  pl.BlockSpec((tm, pe_all.shape[1]), lambda i: (layer * nt + i, 0))]
                 + [_const_spec(a.shape) for a in consts],
        out_specs=rows(d),
        out_shape=jax.ShapeDtypeStruct((m, d), F32),
        compiler_params=_params("parallel"),
        name="mix_ffn",
    )(oa, ob, oc, x2, pe_all, *consts)


def _sattn_body(pt_ref, lam_ref, q_ref, kn_ref, vn_ref, d1_ref, d0_ref, dnw_ref, *rest, npg, nsteps, out_scale):
    del pt_ref
    kp = rest[:npg]
    vp = rest[npg:2 * npg]
    o_ref = rest[2 * npg]
    m_scr, l_scr, acc_scr = rest[2 * npg + 1:]
    p_id = pl.program_id(1)
    for step in range(nsteps):
        @pl.when(p_id == step)
        def _(step=step):
            logits, finish = _sattn_phases(step, nsteps, lam_ref, q_ref, kn_ref, vn_ref, d1_ref, d0_ref, dnw_ref,
                                           kp, vp, o_ref, m_scr, l_scr, acc_scr, out_scale)
            finish(logits())


def _attn_sample(dq, dk, dv, cache_kt, cache_v2, page_table, layer, bias, lam, dnw, out_scale):
    nb, n_pages = page_table.shape
    tq = dq.shape[0] // nb
    npg = PAGES_PER_STEP
    nsteps = n_pages // npg
    assert n_pages % npg == 0 and cache_kt.shape[-1] == PAGE_SIZE and tq == SUBLANES
    nh = 2 * DIFF_HEADS
    t = bias.shape[-1]
    d1 = bias[1, :, 0:tq, t - PAGE_SIZE:t].reshape(nh * tq, PAGE_SIZE)
    d0 = jnp.concatenate([bias[0, :, 0:tq, 0:tq], jnp.full((nh, tq, PAGE_SIZE - tq), -jnp.inf, F32)],
                         axis=-1).reshape(nh * tq, PAGE_SIZE)

    def page_spec(g):
        return pl.BlockSpec((None, None, DIFF_VW, PAGE_SIZE),
                            lambda b, p, pt: (layer, pt[b * n_pages + p * npg + g], 0, 0))

    def rows_spec():
        return pl.BlockSpec((tq, DIFF_VW), lambda b, p, pt: (b, 0))

    def full_spec(a):
        nd = a.ndim
        return pl.BlockSpec(a.shape, lambda b, p, pt: (0,) * nd)

    grid_spec = pltpu.PrefetchScalarGridSpec(
        num_scalar_prefetch=1,
        grid=(nb, nsteps),
        in_specs=[pl.BlockSpec(memory_space=pltpu.SMEM), rows_spec(), rows_spec(), rows_spec(),
                  full_spec(d1), full_spec(d0), full_spec(dnw)]
                 + [page_spec(g) for g in range(npg)] + [page_spec(g) for g in range(npg)],
        out_specs=rows_spec(),
        scratch_shapes=[pltpu.VMEM((nh * tq, 1), F32), pltpu.VMEM((nh * tq, 1), F32),
                        pltpu.VMEM((nh * tq, LANES), F32)],
    )
    return pl.pallas_call(
        functools.partial(_sattn_body, npg=npg, nsteps=nsteps, out_scale=out_scale),
        grid_spec=grid_spec,
        out_shape=jax.ShapeDtypeStruct((nb * tq, DIFF_VW), F32),
        compiler_params=_params("parallel", "arbitrary"),
        name="diff_attn_sample",
    )(page_table.reshape(-1), lam, dq, dk, dv, d1, d0, dnw, *([cache_kt] * npg), *([cache_v2] * npg))


def _row(a):
    return a.reshape(1, -1)


def _state_to_t(s):
    nb = s.shape[0]
    eye = jnp.eye(GLA_HEADS, dtype=s.dtype)
    return jnp.einsum("bhde,hg->bhegd", s, eye).reshape(nb, GLA_VW, GLA_QK)


def _state_from_t(s_t):
    nb = s_t.shape[0]
    blocks = s_t.reshape(nb, GLA_HEADS, GLA_DV, GLA_HEADS, GLA_DK)
    diag = jnp.stack([blocks[:, h, :, h, :] for h in range(GLA_HEADS)], axis=1)
    return diag.transpose(0, 1, 3, 2)


def _token_mixers(x2, nb, t, lw, s0, buf, transposed_k):
    gq, gk, gv, gs, lg, u, dq, dv, *kk = _in_proj(x2, lw["w_in"], lw["wk_t"], lw["wg"], lw["bg"], nb, transposed_k)
    o_a, s_t = _gla(gq, gk, gv, lg, gs, lw["gla_nw"], _state_to_t(s0), nb, t)
    o_b, nbuf = _conv(u, buf, lw["conv_w"], lw["conv_b"], lw["conv_g"], lw["conv_beta"], nb, t)
    return o_a, o_b, dq, dv, kk, _state_from_t(s_t), nbuf


def _prep_w_in(w):
    sizes = (GLA_QK, GLA_QK, GLA_VW, GLA_VW, GLA_GATE_RANK, CONV_CH, CONV_CH, DIFF_VW, DIFF_VW, DIFF_VW)
    parts, s = [], 0
    for n in sizes:
        parts.append(w[:, s:s + n])
        s += n
    wk_t = parts[8].T.astype(BF16)
    parts[4] = jnp.pad(parts[4], ((0, 0), (0, LANES - GLA_GATE_RANK)))
    return jnp.concatenate(parts, axis=1).astype(BF16), wk_t


def _layer_weights(i, w_in, gla_w_gate_up, gla_b_gate, gla_norm_w, conv_w, conv_b, conv_ln_g, conv_ln_b, diff_lq1,
                   diff_lk1, diff_lq2, diff_lk2, diff_norm_w, w_out, ln1_g, ln1_b, w_ffn_gate, w_ffn_up, w_ffn_down,
                   w_ple_gate, w_ple_proj, ln2_g, ln2_b):
    lam_init = 0.8 - 0.6 * math.exp(-0.3 * i)
    lam = (jnp.exp(jnp.sum(diff_lq1[i] * diff_lk1[i])) - jnp.exp(jnp.sum(diff_lq2[i] * diff_lk2[i]))
           + lam_init).reshape(1).astype(F32)
    w_in_p, wk_t = _prep_w_in(w_in[i])
    return dict(
        w_in=w_in_p, wk_t=wk_t,
        wg=jnp.pad(gla_w_gate_up[i], ((0, LANES - GLA_GATE_RANK), (0, 0))).astype(BF16),
        bg=_row(gla_b_gate[i]),
        gla_nw=_row(jnp.tile(gla_norm_w[i], GLA_HEADS)),
        conv_w=conv_w[i], conv_b=_row(conv_b[i]), conv_g=_row(conv_ln_g[i]), conv_beta=_row(conv_ln_b[i]),
        lam=lam, dnw=_row(diff_norm_w[i]),
        w_out=w_out[i].astype(BF16), ln1_g=_row(ln1_g[i]), ln1_b=_row(ln1_b[i]),
        w_ffn_gate=w_ffn_gate[i].astype(BF16), w_ffn_up=w_ffn_up[i].astype(BF16),
        w_ffn_down=w_ffn_down[i].astype(BF16), w_ple_gate=w_ple_gate[i].astype(BF16),
        w_ple_proj=w_ple_proj[i].astype(BF16), ln2_g=_row(ln2_g[i]), ln2_b=_row(ln2_b[i]),
    )


def kernel(x_prompt, x_sample, cache_k, cache_v, state_gla, state_conv, page_table, p_prompt, p_sample, w_in, gla_w_gate_up, gla_b_gate, gla_norm_w, conv_w, conv_b, conv_ln_g, conv_ln_b, diff_lq1, diff_lk1, diff_lq2, diff_lk2, diff_norm_w, rel_bias, w_out, ln1_g, ln1_b, w_ffn_gate, w_ffn_up, w_ffn_down, w_ple_gate, w_ple_proj, ln2_g, ln2_b):
    nbp, s, d = x_prompt.shape
    nbs, ts, _ = x_sample.shape
    depth, n_pool = cache_k.shape[:2]
    assert depth == DEPTH and cache_k.shape[2] == PAGE_SIZE
    cache_kt = jnp.transpose(cache_k, (0, 1, 3, 4, 2)).reshape(depth, n_pool, DIFF_VW, PAGE_SIZE)
    cache_v2 = cache_v.reshape(depth, n_pool, PAGE_SIZE * DIFF_HEADS, 2 * DIFF_DH)
    bias = _bias_tiles(rel_bias, ATTN_TILE)

    pe_prompt = p_prompt.reshape(depth * nbp * s, -1)
    pe_sample = p_sample.reshape(depth * nbs * ts, -1)
    yp = x_prompt.reshape(nbp * s, d)
    ys = x_sample.reshape(nbs * ts, d)
    zero_state = jnp.zeros((nbp, GLA_HEADS, GLA_DK, GLA_DV), F32)
    zero_buf = jnp.zeros((nbp, CONV_WIDTH - 1, CONV_CH), F32)
    outs = [[] for _ in range(8)]
    for i in range(depth):
        lw = _layer_weights(i, w_in, gla_w_gate_up, gla_b_gate, gla_norm_w, conv_w, conv_b, conv_ln_g, conv_ln_b,
                            diff_lq1, diff_lk1, diff_lq2, diff_lk2, diff_norm_w, w_out, ln1_g, ln1_b, w_ffn_gate,
                            w_ffn_up, w_ffn_down, w_ple_gate, w_ple_proj, ln2_g, ln2_b)

        out_scale = 1.0 - (0.8 - 0.6 * math.exp(-0.3 * i))
        pa, pb, q16, v_rows, (kt_i, kt16, v16), s_i, c_i = _token_mixers(yp, nbp, s, lw, zero_state, zero_buf, True)
        pc = _attn_prompt(q16, kt16, v16, bias, lw["lam"], lw["dnw"], nbp, s, out_scale)
        outs[0].append(kt_i.reshape(nbp, 2 * DIFF_HEADS, DIFF_DH, s).transpose(0, 3, 1, 2))
        outs[1].append(v_rows.reshape(nbp, s, DIFF_HEADS, 2 * DIFF_DH))
        outs[2].append(s_i)
        outs[3].append(c_i)
        sa, sb, dq, dv, (dk,), s_i, c_i = _token_mixers(ys, nbs, ts, lw, state_gla[i], state_conv[i], False)
        outs[4].append(dk.reshape(nbs, ts, 2 * DIFF_HEADS, DIFF_DH))
        outs[5].append(dv.reshape(nbs, ts, DIFF_HEADS, 2 * DIFF_DH))
        outs[6].append(s_i)
        outs[7].append(c_i)
        sc = _attn_sample(dq, dk, dv, cache_kt, cache_v2, page_table, i, bias, lw["lam"], lw["dnw"], out_scale)
        yp = _mix_ffn(pa, pb, pc, yp, pe_prompt, i, lw)
        ys = _mix_ffn(sa, sb, sc, ys, pe_sample, i, lw)
    return (yp.reshape(nbp, s, d), ys.reshape(nbs, ts, d)) + tuple(jnp.stack(o) for o in outs)
```

```python
import functools
import math

import jax
import jax.numpy as jnp
from jax import lax
from jax.experimental import pallas as pl
from jax.experimental.pallas import tpu as pltpu

F32 = jnp.float32
BF16 = jnp.bfloat16

GLA_HEADS = 4
GLA_DK = 32
GLA_DV = 64
GLA_QK = GLA_HEADS * GLA_DK
GLA_VW = GLA_HEADS * GLA_DV
GLA_GATE_RANK = 16
GLA_TAU = 16.0
CONV_CH = 256
CONV_WIDTH = 31
DIFF_HEADS = 4
DIFF_DH = 64
DIFF_VW = 2 * DIFF_HEADS * DIFF_DH
REL_BUCKETS = 32
REL_MAX_DIST = 128
PAGE_SIZE = 128
DEPTH = 2
ALPHA = (2 * DEPTH) ** 0.25
EPS = 1e-5
LOG2E = math.log2(math.e)

LANES = 128
SUBLANES = 8
GLA_BLOCK = 16
GLA_SUBTILE = 128
ATTN_TILE = 256
MIX_ROWS = 512
FFN_CHUNK = 256
PAGES_PER_STEP = 16
CONV_HALO = 32
VMEM_LIMIT = 56 * 1024 * 1024

_IN_OFF = {}
_o = 0
for _name, _n in (("gq", 128), ("gk", 128), ("gv", 256), ("gg", 256), ("glr", 128), ("ca", 256), ("cg", 256),
                  ("dq", 512), ("dk", 512), ("dv", 512)):
    _IN_OFF[_name] = (_o, _n)
    _o += _n
N_IN_PAD = _o


def _params(*sem):
    return pltpu.CompilerParams(dimension_semantics=sem, vmem_limit_bytes=VMEM_LIMIT)


def _const_spec(shape):
    nd = len(shape)
    return pl.BlockSpec(shape, lambda *_: (0,) * nd, pipeline_mode=pl.Buffered(1))


def _nt(a, b):
    return lax.dot_general(a, b, (((1,), (1,)), ((), ())), preferred_element_type=F32)


def _mm(a, b):
    return jnp.dot(a, b, preferred_element_type=F32)


def _mm_split(a, b):
    hi = a.astype(BF16)
    lo = (a - hi.astype(F32)).astype(BF16)
    return _mm(hi, b) + _mm(lo, b)


def _sigmoid(x):
    return 1.0 / (1.0 + jnp.exp(-x))


def _layer_norm(x, g, b):
    mu = jnp.mean(x, axis=-1, keepdims=True)
    xc = x - mu
    var = jnp.mean(xc * xc, axis=-1, keepdims=True)
    return xc * lax.rsqrt(var + EPS) * g + b


def _fold_lanes(x, op):
    acc = x[:, 0:LANES]
    for c in range(1, x.shape[1] // LANES):
        acc = op(acc, x[:, c * LANES:(c + 1) * LANES])
    return acc


def _inproj_body(x_ref, w_ref, wkt_ref, wg_ref, bg_ref, gq_ref, gk_ref, gv_ref, gs_ref, lg_ref, u_ref, dq_ref,
                 dv_ref, *k_refs, transposed_k):
    x = x_ref[...].astype(BF16)

    def proj(name):
        lo, n = _IN_OFF[name]
        return _mm(x, w_ref[:, lo:lo + n])

    gq_ref[...] = proj("gq") * GLA_DK ** -0.5
    gk_ref[...] = proj("gk")
    gv_ref[...] = proj("gv")
    gg = proj("gg")
    gs_ref[...] = gg * _sigmoid(gg)
    z = _mm(proj("glr").astype(BF16), wg_ref[...]) + bg_ref[...]
    lg_ref[...] = (jnp.minimum(z, 0.0) - jnp.log(1.0 + jnp.exp(-jnp.abs(z)))) * (1.0 / GLA_TAU)
    u_ref[...] = proj("ca") * _sigmoid(proj("cg"))
    dq_ref[...] = (proj("dq") * (DIFF_DH ** -0.5 * LOG2E)).astype(dq_ref.dtype)
    dv = proj("dv")
    if transposed_k:
        kt_ref, kt16_ref, v16_ref = k_refs
        tm = x.shape[0]
        for h in range(DIFF_HEADS):
            dv_ref[pl.ds(h, tm, stride=DIFF_HEADS), :] = dv[:, h * LANES:(h + 1) * LANES]
        kt = _nt(wkt_ref[...], x)
        kt_ref[0] = kt
        kt16_ref[0] = kt.astype(BF16)
        v16_ref[...] = dv.astype(BF16)
    else:
        dv_ref[...] = dv
        k_refs[0][...] = proj("dk")


def _in_proj(x2, w_in_p, wk_t, wg_p, bg, nb, transposed_k):
    m, d = x2.shape
    t = m // nb
    tm = min(512, t if transposed_k else m)
    nj = t // tm
    widths = (GLA_QK, GLA_QK, GLA_VW, GLA_VW, GLA_QK, CONV_CH, DIFF_VW)
    dtypes = [F32] * 6 + [BF16 if transposed_k else F32]
    out_specs = [pl.BlockSpec((tm, n), lambda i: (i, 0)) for n in widths]
    out_shape = [jax.ShapeDtypeStruct((m, n), dt) for n, dt in zip(widths, dtypes)]
    if transposed_k:
        out_specs.append(pl.BlockSpec((tm * DIFF_HEADS, LANES), lambda i: (i, 0)))
        out_shape.append(jax.ShapeDtypeStruct((m * DIFF_HEADS, LANES), F32))
        kt_spec = pl.BlockSpec((1, DIFF_VW, tm), lambda i: (i // nj, 0, i % nj))
        out_specs += [kt_spec, kt_spec, pl.BlockSpec((tm, DIFF_VW), lambda i: (i, 0))]
        out_shape += [jax.ShapeDtypeStruct((nb, DIFF_VW, t), F32), jax.ShapeDtypeStruct((nb, DIFF_VW, t), BF16),
                      jax.ShapeDtypeStruct((m, DIFF_VW), BF16)]
    else:
        out_specs += [pl.BlockSpec((tm, DIFF_VW), lambda i: (i, 0))] * 2
        out_shape += [jax.ShapeDtypeStruct((m, DIFF_VW), F32)] * 2
    return pl.pallas_call(
        functools.partial(_inproj_body, transposed_k=transposed_k),
        grid=(m // tm,),
        in_specs=[pl.BlockSpec((tm, d), lambda i: (i, 0)),
                  _const_spec(w_in_p.shape), _const_spec(wk_t.shape), _const_spec(wg_p.shape), _const_spec(bg.shape)],
        out_specs=out_specs,
        out_shape=out_shape,
        compiler_params=_params("parallel"),
        name="in_proj",
    )(x2, w_in_p, wk_t, wg_p, bg)


def _gla_body(q_ref, k_ref, v_ref, lg_ref, gs_ref, nw_ref, s0_ref, o_ref, sout_ref, s_scr, *, cb):
    j = pl.program_id(1)

    @pl.when(j == 0)
    def _():
        s_scr[...] = s0_ref[0]

    r = q_ref.shape[0]
    rs = min(r, GLA_SUBTILE)
    row = lax.broadcasted_iota(jnp.int32, (rs, 1), 0) % cb

    same_head = (lax.broadcasted_iota(jnp.int32, (GLA_QK, GLA_VW), 0) // GLA_DK
                 == lax.broadcasted_iota(jnp.int32, (GLA_QK, GLA_VW), 1) // GLA_DV)
    expand = same_head.astype(BF16)
    mask_t = (lax.broadcasted_iota(jnp.int32, (GLA_VW, GLA_QK), 0) // GLA_DV
              == lax.broadcasted_iota(jnp.int32, (GLA_VW, GLA_QK), 1) // GLA_DK).astype(F32)
    grp = (lax.broadcasted_iota(jnp.int32, (GLA_VW, GLA_VW), 0) // GLA_DV
           == lax.broadcasted_iota(jnp.int32, (GLA_VW, GLA_VW), 1) // GLA_DV)
    head_mean = jnp.where(grp, 1.0 / GLA_DV, 0.0).astype(BF16)

    st = s_scr[...]
    for sub in range(r // rs):
        rows = pl.ds(sub * rs, rs)
        q = q_ref[rows, :]
        k = k_ref[rows, :]
        v = v_ref[rows, :]

        b = lg_ref[rows, :]
        s = 1
        while s < cb:
            b = b + jnp.where(row >= s, pltpu.roll(b, s, 0), 0.0)
            s *= 2

        o = _mm((q * k).astype(BF16), expand) * v
        for delta in range(1, cb):
            ks = pltpu.roll(k, delta, 0)
            bs = pltpu.roll(b, delta, 0)
            vs = pltpu.roll(v, delta, 0)
            p = jnp.where(row >= delta, q * ks * jnp.exp(b - bs), 0.0)
            o = o + _mm(p.astype(BF16), expand) * vs

        inter = []
        for t in range(rs // cb):
            sl = slice(t * cb, (t + 1) * cb)
            bt = b[sl]
            bend = bt[cb - 1:cb]
            qe = (q[sl] * jnp.exp(bt)).astype(BF16)
            inter.append(_nt(qe, st.astype(BF16)))
            ke = k[sl] * jnp.exp(bend - bt)
            vt = v[sl]
            if cb < 16:
                ke = jnp.concatenate([ke, jnp.zeros((16 - cb, GLA_QK), F32)], axis=0)
                vt = jnp.concatenate([vt, jnp.zeros((16 - cb, GLA_VW), F32)], axis=0)
            kv = lax.dot_general(vt.astype(BF16), ke.astype(BF16), (((0,), (0,)), ((), ())),
                                 preferred_element_type=F32)
            st = st * jnp.exp(bend) + kv * mask_t
        o = o + (inter[0] if len(inter) == 1 else jnp.concatenate(inter, axis=0))

        mean_sq = _mm_split(o * o, head_mean)
        o_ref[rows, :] = o * lax.rsqrt(mean_sq + EPS) * nw_ref[...] * gs_ref[rows, :]
    s_scr[...] = st

    @pl.when(j == pl.num_programs(1) - 1)
    def _():
        sout_ref[0] = st


def _gla(gq, gk, gv, lg, gs, nw, s0_t, nb, t):
    cb = GLA_BLOCK if t % GLA_BLOCK == 0 else t
    r = min(t, 256)
    nj = t // r
    m = nb * t

    def rows(n):
        return pl.BlockSpec((r, n), lambda b, j: (b * nj + j, 0))

    return pl.pallas_call(
        functools.partial(_gla_body, cb=cb),
        grid=(nb, nj),
        in_specs=[rows(GLA_QK), rows(GLA_QK), rows(GLA_VW), rows(GLA_QK), rows(GLA_VW),
                  _const_spec(nw.shape),
                  pl.BlockSpec((1, GLA_VW, GLA_QK), lambda b, j: (b, 0, 0))],
        out_specs=[rows(GLA_VW), pl.BlockSpec((1, GLA_VW, GLA_QK), lambda b, j: (b, 0, 0))],
        out_shape=[jax.ShapeDtypeStruct((m, GLA_VW), F32), jax.ShapeDtypeStruct((nb, GLA_VW, GLA_QK), F32)],
        scratch_shapes=[pltpu.VMEM((GLA_VW, GLA_QK), F32)],
        compiler_params=_params("parallel", "arbitrary"),
        name="gla",
    )(gq, gk, gv, lg, gs, nw, s0_t)


def _conv_body(u_ref, buf_ref, cw_ref, cb_ref, g_ref, beta_ref, y_ref, nbuf_ref, seq_scr):
    j = pl.program_id(1)
    r = u_ref.shape[0]
    hist = CONV_WIDTH - 1

    @pl.when(j == 0)
    def _():
        seq_scr[0:CONV_HALO - hist, :] = jnp.zeros((CONV_HALO - hist, CONV_CH), F32)
        seq_scr[CONV_HALO - hist:CONV_HALO, :] = buf_ref[0]

    @pl.when(j > 0)
    def _():
        seq_scr[0:CONV_HALO, :] = seq_scr[r:r + CONV_HALO, :]

    seq_scr[CONV_HALO:CONV_HALO + r, :] = u_ref[...]

    window = seq_scr[...]
    rows = window.shape[0]
    acc = jnp.zeros((r, CONV_CH), F32) + cb_ref[...]
    for rho in range(SUBLANES):
        lo = CONV_HALO - hist + rho
        shifted = pltpu.roll(window, rows - lo, 0)
        for k in range(-(-CONV_WIDTH // SUBLANES)):
            w = SUBLANES * k + rho
            if w < CONV_WIDTH:
                acc = acc + shifted[SUBLANES * k:SUBLANES * k + r, :] * cw_ref[w:w + 1, :]
    y = _layer_norm(acc, g_ref[...], beta_ref[...])
    y_ref[...] = y * _sigmoid(y)

    @pl.when(j == pl.num_programs(1) - 1)
    def _():
        nbuf_ref[0] = seq_scr[r + CONV_HALO - hist:r + CONV_HALO, :]


def _conv(u, buf, cw, cb, g, beta, nb, t):
    r = min(t, 512)
    nj = t // r
    assert nj == 1 or r >= CONV_HALO
    hist = CONV_WIDTH - 1
    return pl.pallas_call(
        _conv_body,
        grid=(nb, nj),
        in_specs=[pl.BlockSpec((r, CONV_CH), lambda b, j: (b * nj + j, 0)),
                  pl.BlockSpec((1, hist, CONV_CH), lambda b, j: (b, 0, 0)),
                  _const_spec(cw.shape), _const_spec(cb.shape), _const_spec(g.shape), _const_spec(beta.shape)],
        out_specs=[pl.BlockSpec((r, CONV_CH), lambda b, j: (b * nj + j, 0)),
                   pl.BlockSpec((1, hist, CONV_CH), lambda b, j: (b, 0, 0))],
        out_shape=[jax.ShapeDtypeStruct((nb * t, CONV_CH), F32), jax.ShapeDtypeStruct((nb, hist, CONV_CH), F32)],
        scratch_shapes=[pltpu.VMEM((r + CONV_HALO, CONV_CH), F32)],
        compiler_params=_params("parallel", "arbitrary"),
        name="conv",
    )(u, buf, cw, cb, g, beta)


def _rel_bucket(dist):
    n = jnp.maximum(dist, 0)
    max_exact = REL_BUCKETS // 2
    nf = jnp.maximum(n, 1).astype(F32)
    large = max_exact + (jnp.log(nf / max_exact) / math.log(REL_MAX_DIST / max_exact)
                         * (REL_BUCKETS - max_exact)).astype(jnp.int32)
    large = jnp.minimum(large, REL_BUCKETS - 1)
    return jnp.where(n < max_exact, n, large)


def _bias_body(rb_ref, idx_ref, o_ref):
    h = pl.program_id(1)
    idx = idx_ref[0]
    acc = jnp.full(idx.shape, -jnp.inf, F32)
    for bucket in range(REL_BUCKETS):
        acc = jnp.where(idx == bucket, rb_ref[bucket, h], acc)
    o_ref[0, 0] = (acc - rb_ref[REL_BUCKETS - 1, h]) * LOG2E


def _bias_tiles(rel_bias, t):
    assert t >= REL_MAX_DIST
    ii = jnp.arange(t, dtype=jnp.int32)[:, None]
    jj = jnp.arange(t, dtype=jnp.int32)[None, :]
    idx = jnp.stack([jnp.where(ii >= jj, _rel_bucket(ii - jj), -1), _rel_bucket(t + ii - jj)])
    nh = rel_bias.shape[1]
    return pl.pallas_call(
        _bias_body,
        grid=(2, nh),
        in_specs=[pl.BlockSpec(memory_space=pltpu.SMEM),
                  pl.BlockSpec((1, t, t), lambda r, h: (r, 0, 0))],
        out_specs=pl.BlockSpec((1, 1, t, t), lambda r, h: (r, h, 0, 0)),
        out_shape=jax.ShapeDtypeStruct((2, nh, t, t), F32),
        compiler_params=_params("arbitrary", "arbitrary"),
        name="rel_bias_tiles",
    )(rel_bias, idx)


def _attn_body(lam_ref, q_ref, kt_ref, v_ref, bias_ref, dnw_ref, *rest, out_scale, nstack):
    stack_in = rest[:2 * nstack]
    o_ref = rest[2 * nstack]
    stack_out = rest[2 * nstack + 1:2 * nstack + 1 + (2 if nstack else 0)]
    q_scr, s_scr, m_scr, l_scr, acc_scr = rest[len(rest) - 5:]
    for layer in range(nstack):
        stack_out[0][layer, 0] = stack_in[layer][0]
        stack_out[1][layer] = stack_in[nstack + layer][...]
    i = pl.program_id(1)
    t = q_ref.shape[0]
    nmaps = 2 * DIFF_HEADS
    lane = lax.broadcasted_iota(jnp.int32, (t, LANES), 1)
    for n in range(nmaps):
        qh = q_ref[:, (n // 2) * LANES:(n // 2 + 1) * LANES]
        keep = (lane < DIFF_DH) if n % 2 == 0 else (lane >= DIFF_DH)
        q_scr[n] = jnp.where(keep, qh, jnp.zeros_like(qh))
    m_scr[...] = jnp.full(m_scr.shape, -jnp.inf, F32)
    l_scr[...] = jnp.zeros(l_scr.shape, F32)
    acc_scr[...] = jnp.zeros(acc_scr.shape, F32)

    def tile(j, which, ntile=1):
        width = ntile * t
        keys = pl.ds(pl.multiple_of(j * t, t), width)
        for n in range(nmaps):
            s = _mm(q_scr[n], kt_ref[0, (n // 2) * LANES:(n // 2 + 1) * LANES, keys])
            if which == "near":
                s = s + jnp.concatenate([bias_ref[1, n], bias_ref[0, n]], axis=1)
            elif which is not None:
                s = s + bias_ref[which, n]
            s_scr[n, :, 0:width] = s
        for n in range(nmaps):
            s = s_scr[n, :, 0:width]
            m_old = m_scr[n]
            row_max = jnp.max(_fold_lanes(s, jnp.maximum), axis=-1, keepdims=True)
            m_new = jnp.maximum(m_old, jnp.broadcast_to(row_max, (t, LANES)))
            alpha = jnp.exp2(m_old - m_new)
            ps = [jnp.exp2(s[:, c * LANES:(c + 1) * LANES] - m_new) for c in range(width // LANES)]
            part = ps[0]
            for pc in ps[1:]:
                part = part + pc
            l_scr[n] = alpha * l_scr[n] + part
            pv = _mm(jnp.concatenate(ps, axis=1).astype(BF16), v_ref[keys, (n // 2) * LANES:(n // 2 + 1) * LANES])
            acc_scr[n] = alpha * acc_scr[n] + pv
            m_scr[n] = m_new

    nfar = jnp.maximum(i - 1, 0)

    def far_pair(c, carry):
        tile(2 * c, None, 2)
        return carry

    lax.fori_loop(0, nfar // 2, far_pair, 0)

    @pl.when(nfar % 2 == 1)
    def _():
        tile(nfar - 1, None)

    @pl.when(i >= 1)
    def _():
        tile(i - 1, "near", 2)

    @pl.when(i == 0)
    def _():
        tile(0, 0)

    lam = lam_ref[0]
    outs = []
    for hp in range(DIFF_HEADS):
        l0 = jnp.sum(l_scr[2 * hp], axis=-1, keepdims=True)
        l1 = jnp.sum(l_scr[2 * hp + 1], axis=-1, keepdims=True)
        o = acc_scr[2 * hp] / l0 - lam * (acc_scr[2 * hp + 1] / l1)
        ms_o = jnp.mean(o * o, axis=-1, keepdims=True)
        outs.append(o * lax.rsqrt(ms_o + EPS) * dnw_ref[...] * out_scale)
    o_ref[...] = jnp.concatenate(outs, axis=1)


def _attn_prompt(q16, kt16, v16, bias, lam, dnw, nb, s, out_scale, stack_kt=(), stack_v=()):
    t = bias.shape[-1]
    nq = s // t
    nmaps = 2 * DIFF_HEADS
    nstack = len(stack_kt)
    assert len(stack_v) == nstack
    in_specs = [pl.BlockSpec(memory_space=pltpu.SMEM),
                pl.BlockSpec((t, DIFF_VW), lambda b, i: (b * nq + i, 0)),
                pl.BlockSpec((1, DIFF_VW, s), lambda b, i: (b, 0, 0)),
                pl.BlockSpec((s, DIFF_VW), lambda b, i: (b, 0)),
                _const_spec(bias.shape), _const_spec(dnw.shape)]
    in_specs += [pl.BlockSpec((1, DIFF_VW, t), lambda b, i: (b, 0, i))] * nstack
    in_specs += [pl.BlockSpec((t * DIFF_HEADS, LANES), lambda b, i: (b * nq + i, 0))] * nstack
    out_specs = [pl.BlockSpec((t, DIFF_VW), lambda b, i: (b * nq + i, 0))]
    out_shape = [jax.ShapeDtypeStruct((nb * s, DIFF_VW), F32)]
    if nstack:
        out_specs += [pl.BlockSpec((nstack, 1, DIFF_VW, t), lambda b, i: (0, b, 0, i)),
                      pl.BlockSpec((nstack, t * DIFF_HEADS, LANES), lambda b, i: (0, b * nq + i, 0))]
        out_shape += [jax.ShapeDtypeStruct((nstack, nb, DIFF_VW, s), F32),
                      jax.ShapeDtypeStruct((nstack, nb * s * DIFF_HEADS, LANES), F32)]
    out = pl.pallas_call(
        functools.partial(_attn_body, out_scale=out_scale, nstack=nstack),
        grid=(nb, nq),
        in_specs=in_specs,
        out_specs=out_specs,
        out_shape=out_shape,
        scratch_shapes=[pltpu.VMEM((nmaps, t, LANES), BF16), pltpu.VMEM((nmaps, t, 2 * t), F32),
                        pltpu.VMEM((nmaps, t, LANES), F32), pltpu.VMEM((nmaps, t, LANES), F32),
                        pltpu.VMEM((nmaps, t, LANES), F32)],
        compiler_params=_params("parallel", "arbitrary"),
        name="diff_attn_prompt",
    )(lam, q16, kt16, v16, bias, dnw, *stack_kt, *stack_v)
    return out if nstack else out[0]


def _sattn_phases(step, nsteps, lam_ref, q_ref, kn_ref, vn_ref, d1_ref, d0_ref, dnw_ref, kp, vp, o_ref, m_scr, l_scr,
                  acc_scr, out_scale):
    npg = len(kp)
    nh = 2 * DIFF_HEADS
    tq = q_ref.shape[0]
    rows_h = 2 * tq

    q = q_ref[...]
    lane_head = lax.broadcasted_iota(jnp.int32, q.shape, 1) // DIFF_DH
    qs = jnp.concatenate([jnp.where(lane_head == hh, q, 0.0) for hh in range(nh)], axis=0).astype(BF16)

    if step == 0:
        m_scr[...] = jnp.full(m_scr.shape, -jnp.inf, F32)
        l_scr[...] = jnp.zeros(l_scr.shape, F32)
        acc_scr[...] = jnp.zeros(acc_scr.shape, F32)

    def update(s, values_of):
        m_old = m_scr[...]
        m_new = jnp.maximum(m_old, jnp.max(s, axis=-1, keepdims=True))
        alpha = jnp.exp2(m_old - m_new)
        p = jnp.exp2(s - m_new)
        l_scr[...] = alpha * l_scr[...] + jnp.sum(p, axis=-1, keepdims=True)
        pvs = [_mm(p[h * rows_h:(h + 1) * rows_h].astype(BF16), values_of(h)) for h in range(DIFF_HEADS)]
        acc_scr[...] = alpha * acc_scr[...] + jnp.concatenate(pvs, axis=0)
        m_scr[...] = m_new

    is_last = step == nsteps - 1

    def logits():
        kt = jnp.concatenate([kp[g][...].astype(BF16) for g in range(npg)], axis=1)
        s = _mm(qs, kt)
        if is_last:
            zeros = jnp.zeros((s.shape[0], s.shape[1] - PAGE_SIZE), F32)
            s = s + jnp.concatenate([zeros, d1_ref[...]], axis=1)
        return s

    def finish(s):
        update(s, lambda h: jnp.concatenate(
            [vp[g][pl.ds(h, PAGE_SIZE, stride=DIFF_HEADS), :].astype(BF16) for g in range(npg)], axis=0))
        if not is_last:
            return
        pad = jnp.zeros((PAGE_SIZE - tq, DIFF_VW), F32)
        kn = jnp.concatenate([kn_ref[...], pad], axis=0).astype(BF16)
        vn = jnp.concatenate([vn_ref[...], pad], axis=0).astype(BF16)
        update(_nt(qs, kn) + d0_ref[...], lambda h: vn[:, h * LANES:(h + 1) * LANES])
        o = acc_scr[...] / l_scr[...]
        lam = lam_ref[0]
        outs = []
        for h in range(DIFF_HEADS):
            oh = o[h * rows_h:h * rows_h + tq] - lam * o[h * rows_h + tq:(h + 1) * rows_h]
            ms = jnp.mean(oh * oh, axis=-1, keepdims=True)
            outs.append(oh * lax.rsqrt(ms + EPS) * dnw_ref[...] * out_scale)
        o_ref[...] = jnp.concatenate(outs, axis=1)

    return logits, finish


_MIX_CONSTS = ("w_out", "ln1_g", "ln1_b", "w_ffn_gate", "w_ffn_up", "w_ffn_down", "w_ple_gate", "w_ple_proj",
               "ln2_g", "ln2_b")


def _mix_head(oa_ref, ob_ref, oc_ref, x_ref, pe_ref, wo_ref, g1_ref, b1_ref, wpg_ref, wpp_ref):
    mix = _mm(oa_ref[...].astype(BF16), wo_ref[0:GLA_VW, :])
    mix = mix + _mm(ob_ref[...].astype(BF16), wo_ref[GLA_VW:GLA_VW + CONV_CH, :])
    mix = mix + _mm(oc_ref[...].astype(BF16), wo_ref[GLA_VW + CONV_CH:, :])
    x = _layer_norm(ALPHA * x_ref[...] + mix, g1_ref[...], b1_ref[...])
    xb = x.astype(BF16)
    ple = _sigmoid(_mm(xb, wpg_ref[...])) * _mm(pe_ref[...].astype(BF16), wpp_ref[...])
    return xb, ALPHA * x + ple


def _ffn_cols(xb, wg_ref, wu_ref, wd_ref, lo, hi):
    acc = None
    c = lo
    while c < hi:
        w = min(FFN_CHUNK, hi - c)
        gate = _mm(xb, wg_ref[:, c:c + w])
        hid = gate * _sigmoid(gate) * _mm(xb, wu_ref[:, c:c + w])
        part = _mm(hid.astype(BF16), wd_ref[c:c + w, :])
        acc = part if acc is None else acc + part
        c += w
    return acc


def _mix_ffn_body(oa_ref, ob_ref, oc_ref, x_ref, pe_ref, wo_ref, g1_ref, b1_ref, wg_ref, wu_ref, wd_ref, wpg_ref,
                  wpp_ref, g2_ref, b2_ref, y_ref):
    xb, base = _mix_head(oa_ref, ob_ref, oc_ref, x_ref, pe_ref, wo_ref, g1_ref, b1_ref, wpg_ref, wpp_ref)
    acc = base + _ffn_cols(xb, wg_ref, wu_ref, wd_ref, 0, wg_ref.shape[1])
    y_ref[...] = _layer_norm(acc, g2_ref[...], b2_ref[...])


def _mix_ffn(oa, ob, oc, x2, pe_all, layer, lw):
    m, d = x2.shape
    tm = min(MIX_ROWS, m)
    nt = m // tm
    consts = [lw[n] for n in _MIX_CONSTS]

    def rows(n):
        return pl.BlockSpec((tm, n), lambda i: (i, 0))

    return pl.pallas_call(
        _mix_ffn_body,
        grid=(nt,),
        in_specs=[rows(GLA_VW), rows(CONV_CH), rows(DIFF_VW), rows(d),
                  pl.BlockSpec((tm, pe_all.shape[1]), lambda i: (layer * nt + i, 0))]
                 + [_const_spec(a.shape) for a in consts],
        out_specs=rows(d),
        out_shape=jax.ShapeDtypeStruct((m, d), F32),
        compiler_params=_params("parallel"),
        name="mix_ffn",
    )(oa, ob, oc, x2, pe_all, *consts)


def _sattn_body(pt_ref, lam_ref, q_ref, kn_ref, vn_ref, d1_ref, d0_ref, dnw_ref, *rest, npg, nsteps, out_scale):
    del pt_ref
    kp = rest[:npg]
    vp = rest[npg:2 * npg]
    o_ref = rest[2 * npg]
    m_scr, l_scr, acc_scr = rest[2 * npg + 1:]
    p_id = pl.program_id(1)
    for step in range(nsteps):
        @pl.when(p_id == step)
        def _(step=step):
            logits, finish = _sattn_phases(step, nsteps, lam_ref, q_ref, kn_ref, vn_ref, d1_ref, d0_ref, dnw_ref,
                                           kp, vp, o_ref, m_scr, l_scr, acc_scr, out_scale)
            finish(logits())


def _attn_sample(dq, dk, dv, cache_kt, cache_v2, page_table, layer, bias, lam, dnw, out_scale):
    nb, n_pages = page_table.shape
    tq = dq.shape[0] // nb
    npg = PAGES_PER_STEP
    nsteps = n_pages // npg
    assert n_pages % npg == 0 and cache_kt.shape[-1] == PAGE_SIZE and tq == SUBLANES
    nh = 2 * DIFF_HEADS
    t = bias.shape[-1]
    d1 = bias[1, :, 0:tq, t - PAGE_SIZE:t].reshape(nh * tq, PAGE_SIZE)
    d0 = jnp.concatenate([bias[0, :, 0:tq, 0:tq], jnp.full((nh, tq, PAGE_SIZE - tq), -jnp.inf, F32)],
                         axis=-1).reshape(nh * tq, PAGE_SIZE)

    def page_spec(g):
        return pl.BlockSpec((None, None, DIFF_VW, PAGE_SIZE),
                            lambda b, p, pt: (layer, pt[b * n_pages + p * npg + g], 0, 0))

    def rows_spec():
        return pl.BlockSpec((tq, DIFF_VW), lambda b, p, pt: (b, 0))

    def full_spec(a):
        nd = a.ndim
        return pl.BlockSpec(a.shape, lambda b, p, pt: (0,) * nd)

    grid_spec = pltpu.PrefetchScalarGridSpec(
        num_scalar_prefetch=1,
        grid=(nb, nsteps),
        in_specs=[pl.BlockSpec(memory_space=pltpu.SMEM), rows_spec(), rows_spec(), rows_spec(),
                  full_spec(d1), full_spec(d0), full_spec(dnw)]
                 + [page_spec(g) for g in range(npg)] + [page_spec(g) for g in range(npg)],
        out_specs=rows_spec(),
        scratch_shapes=[pltpu.VMEM((nh * tq, 1), F32), pltpu.VMEM((nh * tq, 1), F32),
                        pltpu.VMEM((nh * tq, LANES), F32)],
    )
    return pl.pallas_call(
        functools.partial(_sattn_body, npg=npg, nsteps=nsteps, out_scale=out_scale),
        grid_spec=grid_spec,
        out_shape=jax.ShapeDtypeStruct((nb * tq, DIFF_VW), F32),
        compiler_params=_params("parallel", "arbitrary"),
        name="diff_attn_sample",
    )(page_table.reshape(-1), lam, dq, dk, dv, d1, d0, dnw, *([cache_kt] * npg), *([cache_v2] * npg))


def _row(a):
    return a.reshape(1, -1)


def _state_to_t(s):
    nb = s.shape[0]
    eye = jnp.eye(GLA_HEADS, dtype=s.dtype)
    return jnp.einsum("bhde,hg->bhegd", s, eye).reshape(nb, GLA_VW, GLA_QK)


def _state_from_t(s_t):
    nb = s_t.shape[0]
    blocks = s_t.reshape(nb, GLA_HEADS, GLA_DV, GLA_HEADS, GLA_DK)
    diag = jnp.stack([blocks[:, h, :, h, :] for h in range(GLA_HEADS)], axis=1)
    return diag.transpose(0, 1, 3, 2)


def _token_mixers(x2, nb, t, lw, s0, buf, transposed_k):
    gq, gk, gv, gs, lg, u, dq, dv, *kk = _in_proj(x2, lw["w_in"], lw["wk_t"], lw["wg"], lw["bg"], nb, transposed_k)
    o_a, s_t = _gla(gq, gk, gv, lg, gs, lw["gla_nw"], _state_to_t(s0), nb, t)
    o_b, nbuf = _conv(u, buf, lw["conv_w"], lw["conv_b"], lw["conv_g"], lw["conv_beta"], nb, t)
    return o_a, o_b, dq, dv, kk, _state_from_t(s_t), nbuf


def _prep_w_in(w):
    sizes = (GLA_QK, GLA_QK, GLA_VW, GLA_VW, GLA_GATE_RANK, CONV_CH, CONV_CH, DIFF_VW, DIFF_VW, DIFF_VW)
    parts, s = [], 0
    for n in sizes:
        parts.append(w[:, s:s + n])
        s += n
    wk_t = parts[8].T.astype(BF16)
    parts[4] = jnp.pad(parts[4], ((0, 0), (0, LANES - GLA_GATE_RANK)))
    return jnp.concatenate(parts, axis=1).astype(BF16), wk_t


def _layer_weights(i, w_in, gla_w_gate_up, gla_b_gate, gla_norm_w, conv_w, conv_b, conv_ln_g, conv_ln_b, diff_lq1,
                   diff_lk1, diff_lq2, diff_lk2, diff_norm_w, w_out, ln1_g, ln1_b, w_ffn_gate, w_ffn_up, w_ffn_down,
                   w_ple_gate, w_ple_proj, ln2_g, ln2_b):
    lam_init = 0.8 - 0.6 * math.exp(-0.3 * i)
    lam = (jnp.exp(jnp.sum(diff_lq1[i] * diff_lk1[i])) - jnp.exp(jnp.sum(diff_lq2[i] * diff_lk2[i]))
           + lam_init).reshape(1).astype(F32)
    w_in_p, wk_t = _prep_w_in(w_in[i])
    return dict(
        w_in=w_in_p, wk_t=wk_t,
        wg=jnp.pad(gla_w_gate_up[i], ((0, LANES - GLA_GATE_RANK), (0, 0))).astype(BF16),
        bg=_row(gla_b_gate[i]),
        gla_nw=_row(jnp.tile(gla_norm_w[i], GLA_HEADS)),
        conv_w=conv_w[i], conv_b=_row(conv_b[i]), conv_g=_row(conv_ln_g[i]), conv_beta=_row(conv_ln_b[i]),
        lam=lam, dnw=_row(diff_norm_w[i]),
        w_out=w_out[i].astype(BF16), ln1_g=_row(ln1_g[i]), ln1_b=_row(ln1_b[i]),
        w_ffn_gate=w_ffn_gate[i].astype(BF16), w_ffn_up=w_ffn_up[i].astype(BF16),
        w_ffn_down=w_ffn_down[i].astype(BF16), w_ple_gate=w_ple_gate[i].astype(BF16),
        w_ple_proj=w_ple_proj[i].astype(BF16), ln2_g=_row(ln2_g[i]), ln2_b=_row(ln2_b[i]),
    )


def kernel(x_prompt, x_sample, cache_k, cache_v, state_gla, state_conv, page_table, p_prompt, p_sample, w_in, gla_w_gate_up, gla_b_gate, gla_norm_w, conv_w, conv_b, conv_ln_g, conv_ln_b, diff_lq1, diff_lk1, diff_lq2, diff_lk2, diff_norm_w, rel_bias, w_out, ln1_g, ln1_b, w_ffn_gate, w_ffn_up, w_ffn_down, w_ple_gate, w_ple_proj, ln2_g, ln2_b):
    nbp, s, d = x_prompt.shape
    nbs, ts, _ = x_sample.shape
    depth, n_pool = cache_k.shape[:2]
    assert depth == DEPTH and cache_k.shape[2] == PAGE_SIZE
    cache_kt = jnp.transpose(cache_k, (0, 1, 3, 4, 2)).reshape(depth, n_pool, DIFF_VW, PAGE_SIZE)
    cache_v2 = cache_v.reshape(depth, n_pool, PAGE_SIZE * DIFF_HEADS, 2 * DIFF_DH)
    bias = _bias_tiles(rel_bias, ATTN_TILE)

    pe_prompt = p_prompt.reshape(depth * nbp * s, -1)
    pe_sample = p_sample.reshape(depth * nbs * ts, -1)
    yp = x_prompt.reshape(nbp * s, d)
    ys = x_sample.reshape(nbs * ts, d)
    zero_state = jnp.zeros((nbp, GLA_HEADS, GLA_DK, GLA_DV), F32)
    zero_buf = jnp.zeros((nbp, CONV_WIDTH - 1, CONV_CH), F32)
    outs = [[] for _ in range(6)]
    kts, vrs = [], []
    for i in range(depth):
        lw = _layer_weights(i, w_in, gla_w_gate_up, gla_b_gate, gla_norm_w, conv_w, conv_b, conv_ln_g, conv_ln_b,
                            diff_lq1, diff_lk1, diff_lq2, diff_lk2, diff_norm_w, w_out, ln1_g, ln1_b, w_ffn_gate,
                            w_ffn_up, w_ffn_down, w_ple_gate, w_ple_proj, ln2_g, ln2_b)

        out_scale = 1.0 - (0.8 - 0.6 * math.exp(-0.3 * i))
        pa, pb, q16, v_rows, (kt_i, kt16, v16), s_i, c_i = _token_mixers(yp, nbp, s, lw, zero_state, zero_buf, True)
        kts.append(kt_i)
        vrs.append(v_rows)
        if i < depth - 1:
            pc = _attn_prompt(q16, kt16, v16, bias, lw["lam"], lw["dnw"], nbp, s, out_scale)
        else:
            pc, k_stack, v_stack = _attn_prompt(q16, kt16, v16, bias, lw["lam"], lw["dnw"], nbp, s, out_scale,
                                                kts, vrs)
        outs[0].append(s_i)
        outs[1].append(c_i)
        sa, sb, dq, dv, (dk,), s_i, c_i = _token_mixers(ys, nbs, ts, lw, state_gla[i], state_conv[i], False)
        outs[2].append(dk.reshape(nbs, ts, 2 * DIFF_HEADS, DIFF_DH))
        outs[3].append(dv.reshape(nbs, ts, DIFF_HEADS, 2 * DIFF_DH))
        outs[4].append(s_i)
        outs[5].append(c_i)
        sc = _attn_sample(dq, dk, dv, cache_kt, cache_v2, page_table, i, bias, lw["lam"], lw["dnw"], out_scale)
        yp = _mix_ffn(pa, pb, pc, yp, pe_prompt, i, lw)
        ys = _mix_ffn(sa, sb, sc, ys, pe_sample, i, lw)
    k_prompt = k_stack.reshape(depth, nbp, 2 * DIFF_HEADS, DIFF_DH, s).transpose(0, 1, 4, 2, 3)
    v_prompt = v_stack.reshape(depth, nbp, s, DIFF_HEADS, 2 * DIFF_DH)
    return (yp.reshape(nbp, s, d), ys.reshape(nbs, ts, d), k_prompt, v_prompt) + tuple(jnp.stack(o) for o in outs)
```

```python
import functools
import math

import jax
import jax.numpy as jnp
from jax import lax
from jax.experimental import pallas as pl
from jax.experimental.pallas import tpu as pltpu

F32 = jnp.float32
BF16 = jnp.bfloat16

GLA_HEADS = 4
GLA_DK = 32
GLA_DV = 64
GLA_QK = GLA_HEADS * GLA_DK
GLA_VW = GLA_HEADS * GLA_DV
GLA_GATE_RANK = 16
GLA_TAU = 16.0
CONV_CH = 256
CONV_WIDTH = 31
DIFF_HEADS = 4
DIFF_DH = 64
DIFF_VW = 2 * DIFF_HEADS * DIFF_DH
REL_BUCKETS = 32
REL_MAX_DIST = 128
PAGE_SIZE = 128
DEPTH = 2
ALPHA = (2 * DEPTH) ** 0.25
EPS = 1e-5
LOG2E = math.log2(math.e)

LANES = 128
SUBLANES = 8
GLA_BLOCK = 16
GLA_SUBTILE = 128
ATTN_TILE = 256
SEQS_PER_STEP = 8
MIX_ROWS = 512
FFN_CHUNK = 256
PAGES_PER_STEP = 16
CONV_HALO = 32
VMEM_LIMIT = 56 * 1024 * 1024

_IN_OFF = {}
_o = 0
for _name, _n in (("gq", 128), ("gk", 128), ("gv", 256), ("gg", 256), ("glr", 128), ("ca", 256), ("cg", 256),
                  ("dq", 512), ("dk", 512), ("dv", 512)):
    _IN_OFF[_name] = (_o, _n)
    _o += _n
N_IN_PAD = _o


def _params(*sem):
    return pltpu.CompilerParams(dimension_semantics=sem, vmem_limit_bytes=VMEM_LIMIT)


def _const_spec(shape):
    nd = len(shape)
    return pl.BlockSpec(shape, lambda *_: (0,) * nd, pipeline_mode=pl.Buffered(1))


def _nt(a, b):
    return lax.dot_general(a, b, (((1,), (1,)), ((), ())), preferred_element_type=F32)


def _mm(a, b):
    return jnp.dot(a, b, preferred_element_type=F32)


def _mm_split(a, b):
    hi = a.astype(BF16)
    lo = (a - hi.astype(F32)).astype(BF16)
    return _mm(hi, b) + _mm(lo, b)


def _sigmoid(x):
    return 1.0 / (1.0 + jnp.exp(-x))


def _layer_norm(x, g, b):
    mu = jnp.mean(x, axis=-1, keepdims=True)
    xc = x - mu
    var = jnp.mean(xc * xc, axis=-1, keepdims=True)
    return xc * lax.rsqrt(var + EPS) * g + b


def _seqs_per_step(nb, nj):
    if nj > 1:
        return 1
    return math.gcd(nb, SEQS_PER_STEP)


def _fold_lanes(x, op):
    acc = x[:, 0:LANES]
    for c in range(1, x.shape[1] // LANES):
        acc = op(acc, x[:, c * LANES:(c + 1) * LANES])
    return acc


def _inproj_body(x_ref, w_ref, wkt_ref, wg_ref, bg_ref, gq_ref, gk_ref, gv_ref, gs_ref, lg_ref, u_ref, dq_ref,
                 dv_ref, *k_refs, transposed_k):
    x = x_ref[...].astype(BF16)

    def proj(name):
        lo, n = _IN_OFF[name]
        return _mm(x, w_ref[:, lo:lo + n])

    gq_ref[...] = proj("gq") * GLA_DK ** -0.5
    gk_ref[...] = proj("gk")
    gv_ref[...] = proj("gv")
    gg = proj("gg")
    gs_ref[...] = gg * _sigmoid(gg)
    z = _mm(proj("glr").astype(BF16), wg_ref[...]) + bg_ref[...]
    lg_ref[...] = (jnp.minimum(z, 0.0) - jnp.log(1.0 + jnp.exp(-jnp.abs(z)))) * (1.0 / GLA_TAU)
    u_ref[...] = proj("ca") * _sigmoid(proj("cg"))
    dq_ref[...] = (proj("dq") * (DIFF_DH ** -0.5 * LOG2E)).astype(dq_ref.dtype)
    dv = proj("dv")
    if transposed_k:
        kt_ref, kt16_ref, v16_ref = k_refs
        tm = x.shape[0]
        for h in range(DIFF_HEADS):
            dv_ref[pl.ds(h, tm, stride=DIFF_HEADS), :] = dv[:, h * LANES:(h + 1) * LANES]
        kt = _nt(wkt_ref[...], x)
        kt_ref[0] = kt
        kt16_ref[0] = kt.astype(BF16)
        v16_ref[...] = dv.astype(BF16)
    else:
        dv_ref[...] = dv
        k_refs[0][...] = proj("dk")


def _in_proj(x2, w_in_p, wk_t, wg_p, bg, nb, transposed_k):
    m, d = x2.shape
    t = m // nb
    tm = min(512, t if transposed_k else m)
    nj = t // tm
    widths = (GLA_QK, GLA_QK, GLA_VW, GLA_VW, GLA_QK, CONV_CH, DIFF_VW)
    dtypes = [F32] * 6 + [BF16 if transposed_k else F32]
    out_specs = [pl.BlockSpec((tm, n), lambda i: (i, 0)) for n in widths]
    out_shape = [jax.ShapeDtypeStruct((m, n), dt) for n, dt in zip(widths, dtypes)]
    if transposed_k:
        out_specs.append(pl.BlockSpec((tm * DIFF_HEADS, LANES), lambda i: (i, 0)))
        out_shape.append(jax.ShapeDtypeStruct((m * DIFF_HEADS, LANES), F32))
        kt_spec = pl.BlockSpec((1, DIFF_VW, tm), lambda i: (i // nj, 0, i % nj))
        out_specs += [kt_spec, kt_spec, pl.BlockSpec((tm, DIFF_VW), lambda i: (i, 0))]
        out_shape += [jax.ShapeDtypeStruct((nb, DIFF_VW, t), F32), jax.ShapeDtypeStruct((nb, DIFF_VW, t), BF16),
                      jax.ShapeDtypeStruct((m, DIFF_VW), BF16)]
    else:
        out_specs += [pl.BlockSpec((tm, DIFF_VW), lambda i: (i, 0))] * 2
        out_shape += [jax.ShapeDtypeStruct((m, DIFF_VW), F32)] * 2
    return pl.pallas_call(
        functools.partial(_inproj_body, transposed_k=transposed_k),
        grid=(m // tm,),
        in_specs=[pl.BlockSpec((tm, d), lambda i: (i, 0)),
                  _const_spec(w_in_p.shape), _const_spec(wk_t.shape), _const_spec(wg_p.shape), _const_spec(bg.shape)],
        out_specs=out_specs,
        out_shape=out_shape,
        compiler_params=_params("parallel"),
        name="in_proj",
    )(x2, w_in_p, wk_t, wg_p, bg)


def _gla_body(q_ref, k_ref, v_ref, lg_ref, gs_ref, nw_ref, s0_ref, o_ref, sout_ref, s_scr, *, cb, nj):
    nseq = s0_ref.shape[0]
    r = q_ref.shape[0] // nseq
    rs = min(r, GLA_SUBTILE)
    row = lax.broadcasted_iota(jnp.int32, (rs, 1), 0) % cb

    same_head = (lax.broadcasted_iota(jnp.int32, (GLA_QK, GLA_VW), 0) // GLA_DK
                 == lax.broadcasted_iota(jnp.int32, (GLA_QK, GLA_VW), 1) // GLA_DV)
    expand = same_head.astype(BF16)
    mask_t = (lax.broadcasted_iota(jnp.int32, (GLA_VW, GLA_QK), 0) // GLA_DV
              == lax.broadcasted_iota(jnp.int32, (GLA_VW, GLA_QK), 1) // GLA_DK).astype(F32)
    grp = (lax.broadcasted_iota(jnp.int32, (GLA_VW, GLA_VW), 0) // GLA_DV
           == lax.broadcasted_iota(jnp.int32, (GLA_VW, GLA_VW), 1) // GLA_DV)
    head_mean = jnp.where(grp, 1.0 / GLA_DV, 0.0).astype(BF16)

    def run_sequence(first_row, st):
        for sub in range(r // rs):
            rows = pl.ds(first_row + sub * rs, rs)
            q = q_ref[rows, :]
            k = k_ref[rows, :]
            v = v_ref[rows, :]

            b = lg_ref[rows, :]
            s = 1
            while s < cb:
                b = b + jnp.where(row >= s, pltpu.roll(b, s, 0), 0.0)
                s *= 2

            o = _mm((q * k).astype(BF16), expand) * v
            for delta in range(1, cb):
                ks = pltpu.roll(k, delta, 0)
                bs = pltpu.roll(b, delta, 0)
                vs = pltpu.roll(v, delta, 0)
                p = jnp.where(row >= delta, q * ks * jnp.exp(b - bs), 0.0)
                o = o + _mm(p.astype(BF16), expand) * vs

            inter = []
            for t in range(rs // cb):
                sl = slice(t * cb, (t + 1) * cb)
                bt = b[sl]
                bend = bt[cb - 1:cb]
                qe = (q[sl] * jnp.exp(bt)).astype(BF16)
                inter.append(_nt(qe, st.astype(BF16)))
                ke = k[sl] * jnp.exp(bend - bt)
                vt = v[sl]
                if cb < 16:
                    ke = jnp.concatenate([ke, jnp.zeros((16 - cb, GLA_QK), F32)], axis=0)
                    vt = jnp.concatenate([vt, jnp.zeros((16 - cb, GLA_VW), F32)], axis=0)
                kv = lax.dot_general(vt.astype(BF16), ke.astype(BF16), (((0,), (0,)), ((), ())),
                                     preferred_element_type=F32)
                st = st * jnp.exp(bend) + kv * mask_t
            o = o + (inter[0] if len(inter) == 1 else jnp.concatenate(inter, axis=0))

            mean_sq = _mm_split(o * o, head_mean)
            o_ref[rows, :] = o * lax.rsqrt(mean_sq + EPS) * nw_ref[...] * gs_ref[rows, :]
        return st

    def load_state(seq):
        z = s0_ref[seq]
        z = jnp.concatenate([z] * GLA_HEADS, axis=1)
        return jnp.where(same_head, z, 0.0).T

    def store_state(seq, st):
        z = st.T
        out = z[:, 0:GLA_DV]
        for h in range(1, GLA_HEADS):
            out = out + z[:, h * GLA_DV:(h + 1) * GLA_DV]
        sout_ref[seq] = out

    if nj == 1:
        for seq in range(nseq):
            store_state(seq, run_sequence(seq * r, load_state(seq)))
    else:
        j = pl.program_id(1)

        @pl.when(j == 0)
        def _():
            s_scr[...] = load_state(0)

        st = run_sequence(0, s_scr[...])
        s_scr[...] = st

        @pl.when(j == nj - 1)
        def _():
            store_state(0, st)


def _gla(gq, gk, gv, lg, gs, nw, s0, nb, t):
    cb = GLA_BLOCK if t % GLA_BLOCK == 0 else t
    r = min(t, 256)
    nj = t // r
    m = nb * t
    nseq = _seqs_per_step(nb, nj)

    def rows(n):
        return pl.BlockSpec((nseq * r, n), lambda b, j: (b * nj + j, 0))

    return pl.pallas_call(
        functools.partial(_gla_body, cb=cb, nj=nj),
        grid=(nb // nseq, nj),
        in_specs=[rows(GLA_QK), rows(GLA_QK), rows(GLA_VW), rows(GLA_QK), rows(GLA_VW),
                  _const_spec(nw.shape),
                  pl.BlockSpec((nseq, GLA_QK, GLA_DV), lambda b, j: (b, 0, 0))],
        out_specs=[rows(GLA_VW), pl.BlockSpec((nseq, GLA_QK, GLA_DV), lambda b, j: (b, 0, 0))],
        out_shape=[jax.ShapeDtypeStruct((m, GLA_VW), F32), jax.ShapeDtypeStruct((nb, GLA_QK, GLA_DV), F32)],
        scratch_shapes=[pltpu.VMEM((GLA_VW, GLA_QK), F32)],
        compiler_params=_params("parallel", "arbitrary"),
        name="gla",
    )(gq, gk, gv, lg, gs, nw, s0)


def _conv_body(u_ref, buf_ref, cw_ref, cb_ref, g_ref, beta_ref, y_ref, nbuf_ref, seq_scr, *, nj):
    nseq = buf_ref.shape[0]
    r = u_ref.shape[0] // nseq
    hist = CONV_WIDTH - 1

    def start_window(scr, seq):
        scr[0:CONV_HALO - hist, :] = jnp.zeros((CONV_HALO - hist, CONV_CH), F32)
        scr[CONV_HALO - hist:CONV_HALO, :] = buf_ref[seq]

    def run_tile(scr, first_row):
        scr[CONV_HALO:CONV_HALO + r, :] = u_ref[pl.ds(first_row, r), :]
        window = scr[...]
        rows = window.shape[0]
        acc = jnp.zeros((r, CONV_CH), F32) + cb_ref[...]
        for rho in range(SUBLANES):
            lo = CONV_HALO - hist + rho
            shifted = pltpu.roll(window, rows - lo, 0)
            for k in range(-(-CONV_WIDTH // SUBLANES)):
                w = SUBLANES * k + rho
                if w < CONV_WIDTH:
                    acc = acc + shifted[SUBLANES * k:SUBLANES * k + r, :] * cw_ref[w:w + 1, :]
        y = _layer_norm(acc, g_ref[...], beta_ref[...])
        y_ref[pl.ds(first_row, r), :] = y * _sigmoid(y)

    def last_rows(scr):
        return scr[r + CONV_HALO - hist:r + CONV_HALO, :]

    if nj == 1:
        for seq in range(nseq):
            scr = seq_scr.at[seq]
            start_window(scr, seq)
            run_tile(scr, seq * r)
            nbuf_ref[seq] = last_rows(scr)
    else:
        j = pl.program_id(1)
        scr = seq_scr.at[0]

        @pl.when(j == 0)
        def _():
            start_window(scr, 0)

        @pl.when(j > 0)
        def _():
            scr[0:CONV_HALO, :] = scr[r:r + CONV_HALO, :]

        run_tile(scr, 0)

        @pl.when(j == nj - 1)
        def _():
            nbuf_ref[0] = last_rows(scr)


def _conv(u, buf, cw, cb, g, beta, nb, t):
    r = min(t, 512)
    nj = t // r
    assert nj == 1 or r >= CONV_HALO
    hist = CONV_WIDTH - 1
    nseq = _seqs_per_step(nb, nj)
    return pl.pallas_call(
        functools.partial(_conv_body, nj=nj),
        grid=(nb // nseq, nj),
        in_specs=[pl.BlockSpec((nseq * r, CONV_CH), lambda b, j: (b * nj + j, 0)),
                  pl.BlockSpec((nseq, hist, CONV_CH), lambda b, j: (b, 0, 0)),
                  _const_spec(cw.shape), _const_spec(cb.shape), _const_spec(g.shape), _const_spec(beta.shape)],
        out_specs=[pl.BlockSpec((nseq * r, CONV_CH), lambda b, j: (b * nj + j, 0)),
                   pl.BlockSpec((nseq, hist, CONV_CH), lambda b, j: (b, 0, 0))],
        out_shape=[jax.ShapeDtypeStruct((nb * t, CONV_CH), F32), jax.ShapeDtypeStruct((nb, hist, CONV_CH), F32)],
        scratch_shapes=[pltpu.VMEM((nseq, r + CONV_HALO, CONV_CH), F32)],
        compiler_params=_params("parallel", "arbitrary"),
        name="conv",
    )(u, buf, cw, cb, g, beta)


def _rel_bucket(dist):
    n = jnp.maximum(dist, 0)
    max_exact = REL_BUCKETS // 2
    nf = jnp.maximum(n, 1).astype(F32)
    large = max_exact + (jnp.log(nf / max_exact) / math.log(REL_MAX_DIST / max_exact)
                         * (REL_BUCKETS - max_exact)).astype(jnp.int32)
    large = jnp.minimum(large, REL_BUCKETS - 1)
    return jnp.where(n < max_exact, n, large)


def _bias_body(rb_ref, idx_ref, o_ref):
    h = pl.program_id(1)
    idx = idx_ref[0]
    acc = jnp.full(idx.shape, -jnp.inf, F32)
    for bucket in range(REL_BUCKETS):
        acc = jnp.where(idx == bucket, rb_ref[bucket, h], acc)
    o_ref[0, 0] = (acc - rb_ref[REL_BUCKETS - 1, h]) * LOG2E


def _bias_tiles(rel_bias, t):
    assert t >= REL_MAX_DIST
    ii = jnp.arange(t, dtype=jnp.int32)[:, None]
    jj = jnp.arange(t, dtype=jnp.int32)[None, :]
    idx = jnp.stack([jnp.where(ii >= jj, _rel_bucket(ii - jj), -1), _rel_bucket(t + ii - jj)])
    nh = rel_bias.shape[1]
    return pl.pallas_call(
        _bias_body,
        grid=(2, nh),
        in_specs=[pl.BlockSpec(memory_space=pltpu.SMEM),
                  pl.BlockSpec((1, t, t), lambda r, h: (r, 0, 0))],
        out_specs=pl.BlockSpec((1, 1, t, t), lambda r, h: (r, h, 0, 0)),
        out_shape=jax.ShapeDtypeStruct((2, nh, t, t), F32),
        compiler_params=_params("arbitrary", "arbitrary"),
        name="rel_bias_tiles",
    )(rel_bias, idx)


def _attn_body(lam_ref, q_ref, kt_ref, v_ref, bias_ref, dnw_ref, *rest, out_scale, nstack):
    stack_in = rest[:2 * nstack]
    o_ref = rest[2 * nstack]
    stack_out = rest[2 * nstack + 1:2 * nstack + 1 + (2 if nstack else 0)]
    q_scr, s_scr, m_scr, l_scr, acc_scr = rest[len(rest) - 5:]
    for layer in range(nstack):
        stack_out[0][layer, 0] = stack_in[layer][0]
        stack_out[1][layer] = stack_in[nstack + layer][...]
    i = pl.program_id(1)
    t = q_ref.shape[0]
    nmaps = 2 * DIFF_HEADS
    lane = lax.broadcasted_iota(jnp.int32, (t, LANES), 1)
    for n in range(nmaps):
        qh = q_ref[:, (n // 2) * LANES:(n // 2 + 1) * LANES]
        keep = (lane < DIFF_DH) if n % 2 == 0 else (lane >= DIFF_DH)
        q_scr[n] = jnp.where(keep, qh, jnp.zeros_like(qh))
    m_scr[...] = jnp.full(m_scr.shape, -jnp.inf, F32)
    l_scr[...] = jnp.zeros(l_scr.shape, F32)
    acc_scr[...] = jnp.zeros(acc_scr.shape, F32)

    def tile(j, which, ntile=1):
        width = ntile * t
        keys = pl.ds(pl.multiple_of(j * t, t), width)
        for n in range(nmaps):
            s = _mm(q_scr[n], kt_ref[0, (n // 2) * LANES:(n // 2 + 1) * LANES, keys])
            if which == "near":
                s = s + jnp.concatenate([bias_ref[1, n], bias_ref[0, n]], axis=1)
            elif which is not None:
                s = s + bias_ref[which, n]
            s_scr[n, :, 0:width] = s
        for n in range(nmaps):
            s = s_scr[n, :, 0:width]
            m_old = m_scr[n]
            row_max = jnp.max(_fold_lanes(s, jnp.maximum), axis=-1, keepdims=True)
            m_new = jnp.maximum(m_old, jnp.broadcast_to(row_max, (t, LANES)))
            alpha = jnp.exp2(m_old - m_new)
            ps = [jnp.exp2(s[:, c * LANES:(c + 1) * LANES] - m_new) for c in range(width // LANES)]
            part = ps[0]
            for pc in ps[1:]:
                part = part + pc
            l_scr[n] = alpha * l_scr[n] + part
            pv = _mm(jnp.concatenate(ps, axis=1).astype(BF16), v_ref[keys, (n // 2) * LANES:(n // 2 + 1) * LANES])
            acc_scr[n] = alpha * acc_scr[n] + pv
            m_scr[n] = m_new

    nfar = jnp.maximum(i - 1, 0)

    def far_pair(c, carry):
        tile(2 * c, None, 2)
        return carry

    lax.fori_loop(0, nfar // 2, far_pair, 0)

    @pl.when(nfar % 2 == 1)
    def _():
        tile(nfar - 1, None)

    @pl.when(i >= 1)
    def _():
        tile(i - 1, "near", 2)

    @pl.when(i == 0)
    def _():
        tile(0, 0)

    lam = lam_ref[0]
    outs = []
    for hp in range(DIFF_HEADS):
        l0 = jnp.sum(l_scr[2 * hp], axis=-1, keepdims=True)
        l1 = jnp.sum(l_scr[2 * hp + 1], axis=-1, keepdims=True)
        o = acc_scr[2 * hp] / l0 - lam * (acc_scr[2 * hp + 1] / l1)
        ms_o = jnp.mean(o * o, axis=-1, keepdims=True)
        outs.append(o * lax.rsqrt(ms_o + EPS) * dnw_ref[...] * out_scale)
    o_ref[...] = jnp.concatenate(outs, axis=1)


def _attn_prompt(q16, kt16, v16, bias, lam, dnw, nb, s, out_scale, stack_kt=(), stack_v=()):
    t = bias.shape[-1]
    nq = s // t
    nmaps = 2 * DIFF_HEADS
    nstack = len(stack_kt)
    assert len(stack_v) == nstack
    in_specs = [pl.BlockSpec(memory_space=pltpu.SMEM),
                pl.BlockSpec((t, DIFF_VW), lambda b, i: (b * nq + i, 0)),
                pl.BlockSpec((1, DIFF_VW, s), lambda b, i: (b, 0, 0)),
                pl.BlockSpec((s, DIFF_VW), lambda b, i: (b, 0)),
                _const_spec(bias.shape), _const_spec(dnw.shape)]
    in_specs += [pl.BlockSpec((1, DIFF_VW, t), lambda b, i: (b, 0, i))] * nstack
    in_specs += [pl.BlockSpec((t * DIFF_HEADS, LANES), lambda b, i: (b * nq + i, 0))] * nstack
    out_specs = [pl.BlockSpec((t, DIFF_VW), lambda b, i: (b * nq + i, 0))]
    out_shape = [jax.ShapeDtypeStruct((nb * s, DIFF_VW), F32)]
    if nstack:
        out_specs += [pl.BlockSpec((nstack, 1, DIFF_VW, t), lambda b, i: (0, b, 0, i)),
                      pl.BlockSpec((nstack, t * DIFF_HEADS, LANES), lambda b, i: (0, b * nq + i, 0))]
        out_shape += [jax.ShapeDtypeStruct((nstack, nb, DIFF_VW, s), F32),
                      jax.ShapeDtypeStruct((nstack, nb * s * DIFF_HEADS, LANES), F32)]
    out = pl.pallas_call(
        functools.partial(_attn_body, out_scale=out_scale, nstack=nstack),
        grid=(nb, nq),
        in_specs=in_specs,
        out_specs=out_specs,
        out_shape=out_shape,
        scratch_shapes=[pltpu.VMEM((nmaps, t, LANES), BF16), pltpu.VMEM((nmaps, t, 2 * t), F32),
                        pltpu.VMEM((nmaps, t, LANES), F32), pltpu.VMEM((nmaps, t, LANES), F32),
                        pltpu.VMEM((nmaps, t, LANES), F32)],
        compiler_params=_params("parallel", "arbitrary"),
        name="diff_attn_prompt",
    )(lam, q16, kt16, v16, bias, dnw, *stack_kt, *stack_v)
    return out if nstack else out[0]


def _sattn_phases(step, nsteps, lam_ref, q_ref, kn_ref, vn_ref, d1_ref, d0_ref, dnw_ref, kp, vp, o_ref, m_scr, l_scr,
                  acc_scr, out_scale):
    npg = len(kp)
    nh = 2 * DIFF_HEADS
    tq = q_ref.shape[0]
    rows_h = 2 * tq

    q = q_ref[...]
    lane_head = lax.broadcasted_iota(jnp.int32, q.shape, 1) // DIFF_DH
    qs = jnp.concatenate([jnp.where(lane_head == hh, q, 0.0) for hh in range(nh)], axis=0).astype(BF16)

    if step == 0:
        m_scr[...] = jnp.full(m_scr.shape, -jnp.inf, F32)
        l_scr[...] = jnp.zeros(l_scr.shape, F32)
        acc_scr[...] = jnp.zeros(acc_scr.shape, F32)

    def update(s, values_of):
        m_old = m_scr[...]
        m_new = jnp.maximum(m_old, jnp.max(s, axis=-1, keepdims=True))
        alpha = jnp.exp2(m_old - m_new)
        p = jnp.exp2(s - m_new)
        l_scr[...] = alpha * l_scr[...] + jnp.sum(p, axis=-1, keepdims=True)
        pvs = [_mm(p[h * rows_h:(h + 1) * rows_h].astype(BF16), values_of(h)) for h in range(DIFF_HEADS)]
        acc_scr[...] = alpha * acc_scr[...] + jnp.concatenate(pvs, axis=0)
        m_scr[...] = m_new

    is_last = step == nsteps - 1

    def logits():
        kt = jnp.concatenate([kp[g][...].astype(BF16) for g in range(npg)], axis=1)
        s = _mm(qs, kt)
        if is_last:
            zeros = jnp.zeros((s.shape[0], s.shape[1] - PAGE_SIZE), F32)
            s = s + jnp.concatenate([zeros, d1_ref[...]], axis=1)
        return s

    def finish(s):
        update(s, lambda h: jnp.concatenate(
            [vp[g][pl.ds(h, PAGE_SIZE, stride=DIFF_HEADS), :].astype(BF16) for g in range(npg)], axis=0))
        if not is_last:
            return
        pad = jnp.zeros((PAGE_SIZE - tq, DIFF_VW), F32)
        kn = jnp.concatenate([kn_ref[...], pad], axis=0).astype(BF16)
        vn = jnp.concatenate([vn_ref[...], pad], axis=0).astype(BF16)
        update(_nt(qs, kn) + d0_ref[...], lambda h: vn[:, h * LANES:(h + 1) * LANES])
        o = acc_scr[...] / l_scr[...]
        lam = lam_ref[0]
        outs = []
        for h in range(DIFF_HEADS):
            oh = o[h * rows_h:h * rows_h + tq] - lam * o[h * rows_h + tq:(h + 1) * rows_h]
            ms = jnp.mean(oh * oh, axis=-1, keepdims=True)
            outs.append(oh * lax.rsqrt(ms + EPS) * dnw_ref[...] * out_scale)
        o_ref[...] = jnp.concatenate(outs, axis=1)

    return logits, finish


_MIX_CONSTS = ("w_out", "ln1_g", "ln1_b", "w_ffn_gate", "w_ffn_up", "w_ffn_down", "w_ple_gate", "w_ple_proj",
               "ln2_g", "ln2_b")


def _mix_head(oa_ref, ob_ref, oc_ref, x_ref, pe_ref, wo_ref, g1_ref, b1_ref, wpg_ref, wpp_ref):
    mix = _mm(oa_ref[...].astype(BF16), wo_ref[0:GLA_VW, :])
    mix = mix + _mm(ob_ref[...].astype(BF16), wo_ref[GLA_VW:GLA_VW + CONV_CH, :])
    mix = mix + _mm(oc_ref[...].astype(BF16), wo_ref[GLA_VW + CONV_CH:, :])
    x = _layer_norm(ALPHA * x_ref[...] + mix, g1_ref[...], b1_ref[...])
    xb = x.astype(BF16)
    ple = _sigmoid(_mm(xb, wpg_ref[...])) * _mm(pe_ref[...].astype(BF16), wpp_ref[...])
    return xb, ALPHA * x + ple


def _ffn_cols(xb, wg_ref, wu_ref, wd_ref, lo, hi):
    acc = None
    c = lo
    while c < hi:
        w = min(FFN_CHUNK, hi - c)
        gate = _mm(xb, wg_ref[:, c:c + w])
        hid = gate * _sigmoid(gate) * _mm(xb, wu_ref[:, c:c + w])
        part = _mm(hid.astype(BF16), wd_ref[c:c + w, :])
        acc = part if acc is None else acc + part
        c += w
    return acc


def _mix_ffn_body(oa_ref, ob_ref, oc_ref, x_ref, pe_ref, wo_ref, g1_ref, b1_ref, wg_ref, wu_ref, wd_ref, wpg_ref,
                  wpp_ref, g2_ref, b2_ref, y_ref):
    xb, base = _mix_head(oa_ref, ob_ref, oc_ref, x_ref, pe_ref, wo_ref, g1_ref, b1_ref, wpg_ref, wpp_ref)
    acc = base + _ffn_cols(xb, wg_ref, wu_ref, wd_ref, 0, wg_ref.shape[1])
    y_ref[...] = _layer_norm(acc, g2_ref[...], b2_ref[...])


def _mix_ffn(oa, ob, oc, x2, pe_all, layer, lw):
    m, d = x2.shape
    tm = min(MIX_ROWS, m)
    nt = m // tm
    consts = [lw[n] for n in _MIX_CONSTS]

    def rows(n):
        return pl.BlockSpec((tm, n), lambda i: (i, 0))

    return pl.pallas_call(
        _mix_ffn_body,
        grid=(nt,),
        in_specs=[rows(GLA_VW), rows(CONV_CH), rows(DIFF_VW), rows(d),
                  pl.BlockSpec((tm, pe_all.shape[1]), lambda i: (layer * nt + i, 0))]
                 + [_const_spec(a.shape) for a in consts],
        out_specs=rows(d),
        out_shape=jax.ShapeDtypeStruct((m, d), F32),
        compiler_params=_params("parallel"),
        name="mix_ffn",
    )(oa, ob, oc, x2, pe_all, *consts)


def _sattn_body(pt_ref, lam_ref, q_ref, kn_ref, vn_ref, d1_ref, d0_ref, dnw_ref, *rest, npg, nsteps, out_scale):
    del pt_ref
    kp = rest[:npg]
    vp = rest[npg:2 * npg]
    o_ref = rest[2 * npg]
    m_scr, l_scr, acc_scr = rest[2 * npg + 1:]
    p_id = pl.program_id(1)
    for step in range(nsteps):
        @pl.when(p_id == step)
        def _(step=step):
            logits, finish = _sattn_phases(step, nsteps, lam_ref, q_ref, kn_ref, vn_ref, d1_ref, d0_ref, dnw_ref,
                                           kp, vp, o_ref, m_scr, l_scr, acc_scr, out_scale)
            finish(logits())


def _attn_sample(dq, dk, dv, cache_kt, cache_v2, page_table, layer, bias, lam, dnw, out_scale):
    nb, n_pages = page_table.shape
    tq = dq.shape[0] // nb
    npg = PAGES_PER_STEP
    nsteps = n_pages // npg
    assert n_pages % npg == 0 and cache_kt.shape[-1] == PAGE_SIZE and tq == SUBLANES
    nh = 2 * DIFF_HEADS
    t = bias.shape[-1]
    d1 = bias[1, :, 0:tq, t - PAGE_SIZE:t].reshape(nh * tq, PAGE_SIZE)
    d0 = jnp.concatenate([bias[0, :, 0:tq, 0:tq], jnp.full((nh, tq, PAGE_SIZE - tq), -jnp.inf, F32)],
                         axis=-1).reshape(nh * tq, PAGE_SIZE)

    def page_spec(g):
        return pl.BlockSpec((None, None, DIFF_VW, PAGE_SIZE),
                            lambda b, p, pt: (layer, pt[b * n_pages + p * npg + g], 0, 0))

    def rows_spec():
        return pl.BlockSpec((tq, DIFF_VW), lambda b, p, pt: (b, 0))

    def full_spec(a):
        nd = a.ndim
        return pl.BlockSpec(a.shape, lambda b, p, pt: (0,) * nd)

    grid_spec = pltpu.PrefetchScalarGridSpec(
        num_scalar_prefetch=1,
        grid=(nb, nsteps),
        in_specs=[pl.BlockSpec(memory_space=pltpu.SMEM), rows_spec(), rows_spec(), rows_spec(),
                  full_spec(d1), full_spec(d0), full_spec(dnw)]
                 + [page_spec(g) for g in range(npg)] + [page_spec(g) for g in range(npg)],
        out_specs=rows_spec(),
        scratch_shapes=[pltpu.VMEM((nh * tq, 1), F32), pltpu.VMEM((nh * tq, 1), F32),
                        pltpu.VMEM((nh * tq, LANES), F32)],
    )
    return pl.pallas_call(
        functools.partial(_sattn_body, npg=npg, nsteps=nsteps, out_scale=out_scale),
        grid_spec=grid_spec,
        out_shape=jax.ShapeDtypeStruct((nb * tq, DIFF_VW), F32),
        compiler_params=_params("parallel", "arbitrary"),
        name="diff_attn_sample",
    )(page_table.reshape(-1), lam, dq, dk, dv, d1, d0, dnw, *([cache_kt] * npg), *([cache_v2] * npg))


def _row(a):
    return a.reshape(1, -1)


def _token_mixers(x2, nb, t, lw, s0, buf, transposed_k):
    gq, gk, gv, gs, lg, u, dq, dv, *kk = _in_proj(x2, lw["w_in"], lw["wk_t"], lw["wg"], lw["bg"], nb, transposed_k)
    o_a, s_new = _gla(gq, gk, gv, lg, gs, lw["gla_nw"], s0.reshape(nb, GLA_QK, GLA_DV), nb, t)
    o_b, nbuf = _conv(u, buf, lw["conv_w"], lw["conv_b"], lw["conv_g"], lw["conv_beta"], nb, t)
    return o_a, o_b, dq, dv, kk, s_new.reshape(nb, GLA_HEADS, GLA_DK, GLA_DV), nbuf


def _prep_w_in(w):
    sizes = (GLA_QK, GLA_QK, GLA_VW, GLA_VW, GLA_GATE_RANK, CONV_CH, CONV_CH, DIFF_VW, DIFF_VW, DIFF_VW)
    parts, s = [], 0
    for n in sizes:
        parts.append(w[:, s:s + n])
        s += n
    wk_t = parts[8].T.astype(BF16)
    parts[4] = jnp.pad(parts[4], ((0, 0), (0, LANES - GLA_GATE_RANK)))
    return jnp.concatenate(parts, axis=1).astype(BF16), wk_t


def _layer_weights(i, w_in, gla_w_gate_up, gla_b_gate, gla_norm_w, conv_w, conv_b, conv_ln_g, conv_ln_b, diff_lq1,
                   diff_lk1, diff_lq2, diff_lk2, diff_norm_w, w_out, ln1_g, ln1_b, w_ffn_gate, w_ffn_up, w_ffn_down,
                   w_ple_gate, w_ple_proj, ln2_g, ln2_b):
    lam_init = 0.8 - 0.6 * math.exp(-0.3 * i)
    lam = (jnp.exp(jnp.sum(diff_lq1[i] * diff_lk1[i])) - jnp.exp(jnp.sum(diff_lq2[i] * diff_lk2[i]))
           + lam_init).reshape(1).astype(F32)
    w_in_p, wk_t = _prep_w_in(w_in[i])
    return dict(
        w_in=w_in_p, wk_t=wk_t,
        wg=jnp.pad(gla_w_gate_up[i], ((0, LANES - GLA_GATE_RANK), (0, 0))).astype(BF16),
        bg=_row(gla_b_gate[i]),
        gla_nw=_row(jnp.tile(gla_norm_w[i], GLA_HEADS)),
        conv_w=conv_w[i], conv_b=_row(conv_b[i]), conv_g=_row(conv_ln_g[i]), conv_beta=_row(conv_ln_b[i]),
        lam=lam, dnw=_row(diff_norm_w[i]),
        w_out=w_out[i].astype(BF16), ln1_g=_row(ln1_g[i]), ln1_b=_row(ln1_b[i]),
        w_ffn_gate=w_ffn_gate[i].astype(BF16), w_ffn_up=w_ffn_up[i].astype(BF16),
        w_ffn_down=w_ffn_down[i].astype(BF16), w_ple_gate=w_ple_gate[i].astype(BF16),
        w_ple_proj=w_ple_proj[i].astype(BF16), ln2_g=_row(ln2_g[i]), ln2_b=_row(ln2_b[i]),
    )


def kernel(x_prompt, x_sample, cache_k, cache_v, state_gla, state_conv, page_table, p_prompt, p_sample, w_in, gla_w_gate_up, gla_b_gate, gla_norm_w, conv_w, conv_b, conv_ln_g, conv_ln_b, diff_lq1, diff_lk1, diff_lq2, diff_lk2, diff_norm_w, rel_bias, w_out, ln1_g, ln1_b, w_ffn_gate, w_ffn_up, w_ffn_down, w_ple_gate, w_ple_proj, ln2_g, ln2_b):
    nbp, s, d = x_prompt.shape
    nbs, ts, _ = x_sample.shape
    depth, n_pool = cache_k.shape[:2]
    assert depth == DEPTH and cache_k.shape[2] == PAGE_SIZE
    cache_kt = jnp.transpose(cache_k, (0, 1, 3, 4, 2)).reshape(depth, n_pool, DIFF_VW, PAGE_SIZE)
    cache_v2 = cache_v.reshape(depth, n_pool, PAGE_SIZE * DIFF_HEADS, 2 * DIFF_DH)
    bias = _bias_tiles(rel_bias, ATTN_TILE)

    pe_prompt = p_prompt.reshape(depth * nbp * s, -1)
    pe_sample = p_sample.reshape(depth * nbs * ts, -1)
    yp = x_prompt.reshape(nbp * s, d)
    ys = x_sample.reshape(nbs * ts, d)
    zero_state = jnp.zeros((nbp, GLA_HEADS, GLA_DK, GLA_DV), F32)
    zero_buf = jnp.zeros((nbp, CONV_WIDTH - 1, CONV_CH), F32)
    outs = [[] for _ in range(6)]
    kts, vrs = [], []
    for i in range(depth):
        lw = _layer_weights(i, w_in, gla_w_gate_up, gla_b_gate, gla_norm_w, conv_w, conv_b, conv_ln_g, conv_ln_b,
                            diff_lq1, diff_lk1, diff_lq2, diff_lk2, diff_norm_w, w_out, ln1_g, ln1_b, w_ffn_gate,
                            w_ffn_up, w_ffn_down, w_ple_gate, w_ple_proj, ln2_g, ln2_b)

        out_scale = 1.0 - (0.8 - 0.6 * math.exp(-0.3 * i))
        pa, pb, q16, v_rows, (kt_i, kt16, v16), s_i, c_i = _token_mixers(yp, nbp, s, lw, zero_state, zero_buf, True)
        kts.append(kt_i)
        vrs.append(v_rows)
        if i < depth - 1:
            pc = _attn_prompt(q16, kt16, v16, bias, lw["lam"], lw["dnw"], nbp, s, out_scale)
        else:
            pc, k_stack, v_stack = _attn_prompt(q16, kt16, v16, bias, lw["lam"], lw["dnw"], nbp, s, out_scale,
                                                kts, vrs)
        outs[0].append(s_i)
        outs[1].append(c_i)
        sa, sb, dq, dv, (dk,), s_i, c_i = _token_mixers(ys, nbs, ts, lw, state_gla[i], state_conv[i], False)
        outs[2].append(dk.reshape(nbs, ts, 2 * DIFF_HEADS, DIFF_DH))
        outs[3].append(dv.reshape(nbs, ts, DIFF_HEADS, 2 * DIFF_DH))
        outs[4].append(s_i)
        outs[5].append(c_i)
        sc = _attn_sample(dq, dk, dv, cache_kt, cache_v2, page_table, i, bias, lw["lam"], lw["dnw"], out_scale)
        yp = _mix_ffn(pa, pb, pc, yp, pe_prompt, i, lw)
        ys = _mix_ffn(sa, sb, sc, ys, pe_sample, i, lw)
    k_prompt = k_stack.reshape(depth, nbp, 2 * DIFF_HEADS, DIFF_DH, s).transpose(0, 1, 4, 2, 3)
    v_prompt = v_stack.reshape(depth, nbp, s, DIFF_HEADS, 2 * DIFF_DH)
    return (yp.reshape(nbp, s, d), ys.reshape(nbs, ts, d), k_prompt, v_prompt) + tuple(jnp.stack(o) for o in outs)
```

```python
import functools
import math

import jax
import jax.numpy as jnp
from jax import lax
from jax.experimental import pallas as pl
from jax.experimental.pallas import tpu as pltpu

F32 = jnp.float32
BF16 = jnp.bfloat16

GLA_HEADS = 4
GLA_DK = 32
GLA_DV = 64
GLA_QK = GLA_HEADS * GLA_DK
GLA_VW = GLA_HEADS * GLA_DV
GLA_GATE_RANK = 16
GLA_TAU = 16.0
CONV_CH = 256
CONV_WIDTH = 31
DIFF_HEADS = 4
DIFF_DH = 64
DIFF_VW = 2 * DIFF_HEADS * DIFF_DH
REL_BUCKETS = 32
REL_MAX_DIST = 128
PAGE_SIZE = 128
DEPTH = 2
ALPHA = (2 * DEPTH) ** 0.25
EPS = 1e-5
LOG2E = math.log2(math.e)

LANES = 128
SUBLANES = 8
GLA_BLOCK = 16
GLA_SUBTILE = 128
ATTN_TILE = 256
SEQS_PER_STEP = 8
MIX_ROWS = 512
FFN_CHUNK = 256
PAGES_PER_STEP = 16
CONV_HALO = 32
VMEM_LIMIT = 56 * 1024 * 1024

_IN_OFF = {}
_o = 0
for _name, _n in (("gq", 128), ("gk", 128), ("gv", 256), ("gg", 256), ("glr", 128), ("ca", 256), ("cg", 256),
                  ("dq", 512), ("dk", 512), ("dv", 512)):
    _IN_OFF[_name] = (_o, _n)
    _o += _n
N_IN_PAD = _o


def _params(*sem):
    return pltpu.CompilerParams(dimension_semantics=sem, vmem_limit_bytes=VMEM_LIMIT)


def _const_spec(shape, layer=None):
    if layer is None:
        nd = len(shape)
        return pl.BlockSpec(shape, lambda *_: (0,) * nd, pipeline_mode=pl.Buffered(1))
    nd = len(shape) - 1
    return pl.BlockSpec((None,) + tuple(shape[1:]), lambda *_: (layer,) + (0,) * nd, pipeline_mode=pl.Buffered(1))


def _nt(a, b):
    return lax.dot_general(a, b, (((1,), (1,)), ((), ())), preferred_element_type=F32)


def _mm(a, b):
    return jnp.dot(a, b, preferred_element_type=F32)


def _mm_split(a, b):
    hi = a.astype(BF16)
    lo = (a - hi.astype(F32)).astype(BF16)
    return _mm(hi, b) + _mm(lo, b)


def _sigmoid(x):
    return 1.0 / (1.0 + jnp.exp(-x))


def _layer_norm(x, g, b):
    mu = jnp.mean(x, axis=-1, keepdims=True)
    xc = x - mu
    var = jnp.mean(xc * xc, axis=-1, keepdims=True)
    return xc * lax.rsqrt(var + EPS) * g + b


def _seqs_per_step(nb, nj):
    if nj > 1:
        return 1
    return math.gcd(nb, SEQS_PER_STEP)


def _fold_lanes(x, op):
    acc = x[:, 0:LANES]
    for c in range(1, x.shape[1] // LANES):
        acc = op(acc, x[:, c * LANES:(c + 1) * LANES])
    return acc


def _inproj_body(x_ref, w_ref, wkt_ref, wg_ref, bg_ref, gq_ref, gk_ref, gv_ref, gs_ref, lg_ref, u_ref, dq_ref,
                 dv_ref, *k_refs, transposed_k):
    x = x_ref[...].astype(BF16)

    def proj(name):
        lo, n = _IN_OFF[name]
        return _mm(x, w_ref[:, lo:lo + n])

    gq_ref[...] = proj("gq") * GLA_DK ** -0.5
    gk_ref[...] = proj("gk")
    gv_ref[...] = proj("gv")
    gg = proj("gg")
    gs_ref[...] = gg * _sigmoid(gg)
    z = _mm(proj("glr").astype(BF16), wg_ref[...]) + bg_ref[...]
    lg_ref[...] = (jnp.minimum(z, 0.0) - jnp.log(1.0 + jnp.exp(-jnp.abs(z)))) * (1.0 / GLA_TAU)
    u_ref[...] = proj("ca") * _sigmoid(proj("cg"))
    dq_ref[...] = (proj("dq") * (DIFF_DH ** -0.5 * LOG2E)).astype(dq_ref.dtype)
    dv = proj("dv")
    if transposed_k:
        kt_ref, kt16_ref, v16_ref = k_refs
        tm = x.shape[0]
        for h in range(DIFF_HEADS):
            dv_ref[pl.ds(h, tm, stride=DIFF_HEADS), :] = dv[:, h * LANES:(h + 1) * LANES]
        kt = _nt(wkt_ref[...], x)
        kt_ref[0] = kt
        kt16_ref[0] = kt.astype(BF16)
        v16_ref[...] = dv.astype(BF16)
    else:
        dv_ref[...] = dv
        k_refs[0][...] = proj("dk")


def _in_proj(x2, w_in_p, wk_t, wg_p, bg, nb, transposed_k, layer):
    m, d = x2.shape
    t = m // nb
    tm = min(512, t if transposed_k else m)
    nj = t // tm
    widths = (GLA_QK, GLA_QK, GLA_VW, GLA_VW, GLA_QK, CONV_CH, DIFF_VW)
    dtypes = [F32] * 6 + [BF16 if transposed_k else F32]
    out_specs = [pl.BlockSpec((tm, n), lambda i: (i, 0)) for n in widths]
    out_shape = [jax.ShapeDtypeStruct((m, n), dt) for n, dt in zip(widths, dtypes)]
    if transposed_k:
        out_specs.append(pl.BlockSpec((tm * DIFF_HEADS, LANES), lambda i: (i, 0)))
        out_shape.append(jax.ShapeDtypeStruct((m * DIFF_HEADS, LANES), F32))
        kt_spec = pl.BlockSpec((1, DIFF_VW, tm), lambda i: (i // nj, 0, i % nj))
        out_specs += [kt_spec, kt_spec, pl.BlockSpec((tm, DIFF_VW), lambda i: (i, 0))]
        out_shape += [jax.ShapeDtypeStruct((nb, DIFF_VW, t), F32), jax.ShapeDtypeStruct((nb, DIFF_VW, t), BF16),
                      jax.ShapeDtypeStruct((m, DIFF_VW), BF16)]
    else:
        out_specs += [pl.BlockSpec((tm, DIFF_VW), lambda i: (i, 0))] * 2
        out_shape += [jax.ShapeDtypeStruct((m, DIFF_VW), F32)] * 2
    return pl.pallas_call(
        functools.partial(_inproj_body, transposed_k=transposed_k),
        grid=(m // tm,),
        in_specs=[pl.BlockSpec((tm, d), lambda i: (i, 0)),
                  _const_spec(w_in_p.shape, layer), _const_spec(wk_t.shape, layer), _const_spec(wg_p.shape, layer),
                  _const_spec(bg.shape)],
        out_specs=out_specs,
        out_shape=out_shape,
        compiler_params=_params("parallel"),
        name="in_proj",
    )(x2, w_in_p, wk_t, wg_p, bg)


def _gla_body(q_ref, k_ref, v_ref, lg_ref, gs_ref, nw_ref, s0_ref, o_ref, sout_ref, s_scr, *, cb, nj):
    nseq = s0_ref.shape[0]
    r = q_ref.shape[0] // nseq
    rs = min(r, GLA_SUBTILE)
    row = lax.broadcasted_iota(jnp.int32, (rs, 1), 0) % cb

    same_head = (lax.broadcasted_iota(jnp.int32, (GLA_QK, GLA_VW), 0) // GLA_DK
                 == lax.broadcasted_iota(jnp.int32, (GLA_QK, GLA_VW), 1) // GLA_DV)
    expand = same_head.astype(BF16)
    mask_t = (lax.broadcasted_iota(jnp.int32, (GLA_VW, GLA_QK), 0) // GLA_DV
              == lax.broadcasted_iota(jnp.int32, (GLA_VW, GLA_QK), 1) // GLA_DK).astype(F32)
    grp = (lax.broadcasted_iota(jnp.int32, (GLA_VW, GLA_VW), 0) // GLA_DV
           == lax.broadcasted_iota(jnp.int32, (GLA_VW, GLA_VW), 1) // GLA_DV)
    head_mean = jnp.where(grp, 1.0 / GLA_DV, 0.0).astype(BF16)

    def run_sequence(first_row, st):
        for sub in range(r // rs):
            rows = pl.ds(first_row + sub * rs, rs)
            q = q_ref[rows, :]
            k = k_ref[rows, :]
            v = v_ref[rows, :]

            b = lg_ref[rows, :]
            s = 1
            while s < cb:
                b = b + jnp.where(row >= s, pltpu.roll(b, s, 0), 0.0)
                s *= 2

            o = _mm((q * k).astype(BF16), expand) * v
            for delta in range(1, cb):
                ks = pltpu.roll(k, delta, 0)
                bs = pltpu.roll(b, delta, 0)
                vs = pltpu.roll(v, delta, 0)
                p = jnp.where(row >= delta, q * ks * jnp.exp(b - bs), 0.0)
                o = o + _mm(p.astype(BF16), expand) * vs

            inter = []
            for t in range(rs // cb):
                sl = slice(t * cb, (t + 1) * cb)
                bt = b[sl]
                bend = bt[cb - 1:cb]
                qe = (q[sl] * jnp.exp(bt)).astype(BF16)
                inter.append(_nt(qe, st.astype(BF16)))
                ke = k[sl] * jnp.exp(bend - bt)
                vt = v[sl]
                if cb < 16:
                    ke = jnp.concatenate([ke, jnp.zeros((16 - cb, GLA_QK), F32)], axis=0)
                    vt = jnp.concatenate([vt, jnp.zeros((16 - cb, GLA_VW), F32)], axis=0)
                kv = lax.dot_general(vt.astype(BF16), ke.astype(BF16), (((0,), (0,)), ((), ())),
                                     preferred_element_type=F32)
                st = st * jnp.exp(bend) + kv * mask_t
            o = o + (inter[0] if len(inter) == 1 else jnp.concatenate(inter, axis=0))

            mean_sq = _mm_split(o * o, head_mean)
            o_ref[rows, :] = o * lax.rsqrt(mean_sq + EPS) * nw_ref[...] * gs_ref[rows, :]
        return st

    def load_state(seq):
        z = s0_ref[seq]
        z = jnp.concatenate([z] * GLA_HEADS, axis=1)
        return jnp.where(same_head, z, 0.0).T

    def store_state(seq, st):
        z = st.T
        out = z[:, 0:GLA_DV]
        for h in range(1, GLA_HEADS):
            out = out + z[:, h * GLA_DV:(h + 1) * GLA_DV]
        sout_ref[seq] = out

    if nj == 1:
        for seq in range(nseq):
            store_state(seq, run_sequence(seq * r, load_state(seq)))
    else:
        j = pl.program_id(1)

        @pl.when(j == 0)
        def _():
            s_scr[...] = load_state(0)

        st = run_sequence(0, s_scr[...])
        s_scr[...] = st

        @pl.when(j == nj - 1)
        def _():
            store_state(0, st)


def _gla(gq, gk, gv, lg, gs, nw, s0, nb, t):
    cb = GLA_BLOCK if t % GLA_BLOCK == 0 else t
    r = min(t, 256)
    nj = t // r
    m = nb * t
    nseq = _seqs_per_step(nb, nj)

    def rows(n):
        return pl.BlockSpec((nseq * r, n), lambda b, j: (b * nj + j, 0))

    return pl.pallas_call(
        functools.partial(_gla_body, cb=cb, nj=nj),
        grid=(nb // nseq, nj),
        in_specs=[rows(GLA_QK), rows(GLA_QK), rows(GLA_VW), rows(GLA_QK), rows(GLA_VW),
                  _const_spec(nw.shape),
                  pl.BlockSpec((nseq, GLA_QK, GLA_DV), lambda b, j: (b, 0, 0))],
        out_specs=[rows(GLA_VW), pl.BlockSpec((nseq, GLA_QK, GLA_DV), lambda b, j: (b, 0, 0))],
        out_shape=[jax.ShapeDtypeStruct((m, GLA_VW), F32), jax.ShapeDtypeStruct((nb, GLA_QK, GLA_DV), F32)],
        scratch_shapes=[pltpu.VMEM((GLA_VW, GLA_QK), F32)],
        compiler_params=_params("parallel", "arbitrary"),
        name="gla",
    )(gq, gk, gv, lg, gs, nw, s0)


def _conv_body(u_ref, buf_ref, cw_ref, cb_ref, g_ref, beta_ref, y_ref, nbuf_ref, seq_scr, *, nj):
    nseq = buf_ref.shape[0]
    r = u_ref.shape[0] // nseq
    hist = CONV_WIDTH - 1

    def start_window(scr, seq):
        scr[0:CONV_HALO - hist, :] = jnp.zeros((CONV_HALO - hist, CONV_CH), F32)
        scr[CONV_HALO - hist:CONV_HALO, :] = buf_ref[seq]

    def run_tile(scr, first_row):
        scr[CONV_HALO:CONV_HALO + r, :] = u_ref[pl.ds(first_row, r), :]
        window = scr[...]
        rows = window.shape[0]
        acc = jnp.zeros((r, CONV_CH), F32) + cb_ref[...]
        for rho in range(SUBLANES):
            lo = CONV_HALO - hist + rho
            shifted = pltpu.roll(window, rows - lo, 0)
            for k in range(-(-CONV_WIDTH // SUBLANES)):
                w = SUBLANES * k + rho
                if w < CONV_WIDTH:
                    acc = acc + shifted[SUBLANES * k:SUBLANES * k + r, :] * cw_ref[w:w + 1, :]
        y = _layer_norm(acc, g_ref[...], beta_ref[...])
        y_ref[pl.ds(first_row, r), :] = y * _sigmoid(y)

    def last_rows(scr):
        return scr[r + CONV_HALO - hist:r + CONV_HALO, :]

    if nj == 1:
        for seq in range(nseq):
            scr = seq_scr.at[seq]
            start_window(scr, seq)
            run_tile(scr, seq * r)
            nbuf_ref[seq] = last_rows(scr)
    else:
        j = pl.program_id(1)
        scr = seq_scr.at[0]

        @pl.when(j == 0)
        def _():
            start_window(scr, 0)

        @pl.when(j > 0)
        def _():
            scr[0:CONV_HALO, :] = scr[r:r + CONV_HALO, :]

        run_tile(scr, 0)

        @pl.when(j == nj - 1)
        def _():
            nbuf_ref[0] = last_rows(scr)


def _conv(u, buf, cw, cb, g, beta, nb, t):
    r = min(t, 512)
    nj = t // r
    assert nj == 1 or r >= CONV_HALO
    hist = CONV_WIDTH - 1
    nseq = _seqs_per_step(nb, nj)
    return pl.pallas_call(
        functools.partial(_conv_body, nj=nj),
        grid=(nb // nseq, nj),
        in_specs=[pl.BlockSpec((nseq * r, CONV_CH), lambda b, j: (b * nj + j, 0)),
                  pl.BlockSpec((nseq, hist, CONV_CH), lambda b, j: (b, 0, 0)),
                  _const_spec(cw.shape), _const_spec(cb.shape), _const_spec(g.shape), _const_spec(beta.shape)],
        out_specs=[pl.BlockSpec((nseq * r, CONV_CH), lambda b, j: (b * nj + j, 0)),
                   pl.BlockSpec((nseq, hist, CONV_CH), lambda b, j: (b, 0, 0))],
        out_shape=[jax.ShapeDtypeStruct((nb * t, CONV_CH), F32), jax.ShapeDtypeStruct((nb, hist, CONV_CH), F32)],
        scratch_shapes=[pltpu.VMEM((nseq, r + CONV_HALO, CONV_CH), F32)],
        compiler_params=_params("parallel", "arbitrary"),
        name="conv",
    )(u, buf, cw, cb, g, beta)


def _rel_bucket(dist):
    n = jnp.maximum(dist, 0)
    max_exact = REL_BUCKETS // 2
    nf = jnp.maximum(n, 1).astype(F32)
    large = max_exact + (jnp.log(nf / max_exact) / math.log(REL_MAX_DIST / max_exact)
                         * (REL_BUCKETS - max_exact)).astype(jnp.int32)
    large = jnp.minimum(large, REL_BUCKETS - 1)
    return jnp.where(n < max_exact, n, large)


def _bias_body(rb_ref, idx_ref, o_ref):
    h = pl.program_id(1)
    idx = idx_ref[0]
    acc = jnp.full(idx.shape, -jnp.inf, F32)
    for bucket in range(REL_BUCKETS):
        acc = jnp.where(idx == bucket, rb_ref[bucket, h], acc)
    o_ref[0, 0] = (acc - rb_ref[REL_BUCKETS - 1, h]) * LOG2E


def _bias_tiles(rel_bias, t):
    assert t >= REL_MAX_DIST
    ii = jnp.arange(t, dtype=jnp.int32)[:, None]
    jj = jnp.arange(t, dtype=jnp.int32)[None, :]
    idx = jnp.stack([jnp.where(ii >= jj, _rel_bucket(ii - jj), -1), _rel_bucket(t + ii - jj)])
    nh = rel_bias.shape[1]
    return pl.pallas_call(
        _bias_body,
        grid=(2, nh),
        in_specs=[pl.BlockSpec(memory_space=pltpu.SMEM),
                  pl.BlockSpec((1, t, t), lambda r, h: (r, 0, 0))],
        out_specs=pl.BlockSpec((1, 1, t, t), lambda r, h: (r, h, 0, 0)),
        out_shape=jax.ShapeDtypeStruct((2, nh, t, t), F32),
        compiler_params=_params("arbitrary", "arbitrary"),
        name="rel_bias_tiles",
    )(rel_bias, idx)


def _attn_body(lam_ref, q_ref, kt_ref, v_ref, bias_ref, dnw_ref, *rest, out_scale, nstack):
    stack_in = rest[:2 * nstack]
    o_ref = rest[2 * nstack]
    stack_out = rest[2 * nstack + 1:2 * nstack + 1 + (2 if nstack else 0)]
    q_scr, s_scr, m_scr, l_scr, acc_scr = rest[len(rest) - 5:]
    for layer in range(nstack):
        stack_out[0][layer, 0] = stack_in[layer][0]
        stack_out[1][layer] = stack_in[nstack + layer][...]
    i = pl.program_id(1)
    t = q_ref.shape[0]
    nmaps = 2 * DIFF_HEADS
    lane = lax.broadcasted_iota(jnp.int32, (t, LANES), 1)
    for n in range(nmaps):
        qh = q_ref[:, (n // 2) * LANES:(n // 2 + 1) * LANES]
        keep = (lane < DIFF_DH) if n % 2 == 0 else (lane >= DIFF_DH)
        q_scr[n] = jnp.where(keep, qh, jnp.zeros_like(qh))
    m_scr[...] = jnp.full(m_scr.shape, -jnp.inf, F32)
    l_scr[...] = jnp.zeros(l_scr.shape, F32)
    acc_scr[...] = jnp.zeros(acc_scr.shape, F32)

    def tile(j, which, ntile=1):
        width = ntile * t
        keys = pl.ds(pl.multiple_of(j * t, t), width)
        for n in range(nmaps):
            s = _mm(q_scr[n], kt_ref[0, (n // 2) * LANES:(n // 2 + 1) * LANES, keys])
            if which == "near":
                s = s + jnp.concatenate([bias_ref[1, n], bias_ref[0, n]], axis=1)
            elif which is not None:
                s = s + bias_ref[which, n]
            s_scr[n, :, 0:width] = s
        for n in range(nmaps):
            s = s_scr[n, :, 0:width]
            m_old = m_scr[n]
            row_max = jnp.max(_fold_lanes(s, jnp.maximum), axis=-1, keepdims=True)
            m_new = jnp.maximum(m_old, jnp.broadcast_to(row_max, (t, LANES)))
            alpha = jnp.exp2(m_old - m_new)
            ps = [jnp.exp2(s[:, c * LANES:(c + 1) * LANES] - m_new) for c in range(width // LANES)]
            part = ps[0]
            for pc in ps[1:]:
                part = part + pc
            l_scr[n] = alpha * l_scr[n] + part
            pv = _mm(jnp.concatenate(ps, axis=1).astype(BF16), v_ref[keys, (n // 2) * LANES:(n // 2 + 1) * LANES])
            acc_scr[n] = alpha * acc_scr[n] + pv
            m_scr[n] = m_new

    nfar = jnp.maximum(i - 1, 0)

    def far_pair(c, carry):
        tile(2 * c, None, 2)
        return carry

    lax.fori_loop(0, nfar // 2, far_pair, 0)

    @pl.when(nfar % 2 == 1)
    def _():
        tile(nfar - 1, None)

    @pl.when(i >= 1)
    def _():
        tile(i - 1, "near", 2)

    @pl.when(i == 0)
    def _():
        tile(0, 0)

    lam = lam_ref[0]
    outs = []
    for hp in range(DIFF_HEADS):
        l0 = jnp.sum(l_scr[2 * hp], axis=-1, keepdims=True)
        l1 = jnp.sum(l_scr[2 * hp + 1], axis=-1, keepdims=True)
        o = acc_scr[2 * hp] / l0 - lam * (acc_scr[2 * hp + 1] / l1)
        ms_o = jnp.mean(o * o, axis=-1, keepdims=True)
        outs.append(o * lax.rsqrt(ms_o + EPS) * dnw_ref[...] * out_scale)
    o_ref[...] = jnp.concatenate(outs, axis=1)


def _attn_prompt(q16, kt16, v16, bias, lam, dnw, nb, s, out_scale, stack_kt=(), stack_v=()):
    t = bias.shape[-1]
    nq = s // t
    nmaps = 2 * DIFF_HEADS
    nstack = len(stack_kt)
    assert len(stack_v) == nstack
    in_specs = [pl.BlockSpec(memory_space=pltpu.SMEM),
                pl.BlockSpec((t, DIFF_VW), lambda b, i: (b * nq + i, 0)),
                pl.BlockSpec((1, DIFF_VW, s), lambda b, i: (b, 0, 0)),
                pl.BlockSpec((s, DIFF_VW), lambda b, i: (b, 0)),
                _const_spec(bias.shape), _const_spec(dnw.shape)]
    in_specs += [pl.BlockSpec((1, DIFF_VW, t), lambda b, i: (b, 0, i))] * nstack
    in_specs += [pl.BlockSpec((t * DIFF_HEADS, LANES), lambda b, i: (b * nq + i, 0))] * nstack
    out_specs = [pl.BlockSpec((t, DIFF_VW), lambda b, i: (b * nq + i, 0))]
    out_shape = [jax.ShapeDtypeStruct((nb * s, DIFF_VW), F32)]
    if nstack:
        out_specs += [pl.BlockSpec((nstack, 1, DIFF_VW, t), lambda b, i: (0, b, 0, i)),
                      pl.BlockSpec((nstack, t * DIFF_HEADS, LANES), lambda b, i: (0, b * nq + i, 0))]
        out_shape += [jax.ShapeDtypeStruct((nstack, nb, DIFF_VW, s), F32),
                      jax.ShapeDtypeStruct((nstack, nb * s * DIFF_HEADS, LANES), F32)]
    out = pl.pallas_call(
        functools.partial(_attn_body, out_scale=out_scale, nstack=nstack),
        grid=(nb, nq),
        in_specs=in_specs,
        out_specs=out_specs,
        out_shape=out_shape,
        scratch_shapes=[pltpu.VMEM((nmaps, t, LANES), BF16), pltpu.VMEM((nmaps, t, 2 * t), F32),
                        pltpu.VMEM((nmaps, t, LANES), F32), pltpu.VMEM((nmaps, t, LANES), F32),
                        pltpu.VMEM((nmaps, t, LANES), F32)],
        compiler_params=_params("parallel", "arbitrary"),
        name="diff_attn_prompt",
    )(lam, q16, kt16, v16, bias, dnw, *stack_kt, *stack_v)
    return out if nstack else out[0]


def _sattn_phases(step, nsteps, lam_ref, q_ref, kn_ref, vn_ref, d1_ref, d0_ref, dnw_ref, kp, vp, o_ref, m_scr, l_scr,
                  acc_scr, out_scale):
    npg = len(kp)
    nh = 2 * DIFF_HEADS
    tq = q_ref.shape[0]
    rows_h = 2 * tq

    q = q_ref[...]
    lane_head = lax.broadcasted_iota(jnp.int32, q.shape, 1) // DIFF_DH
    qs = jnp.concatenate([jnp.where(lane_head == hh, q, 0.0) for hh in range(nh)], axis=0).astype(BF16)

    if step == 0:
        m_scr[...] = jnp.full(m_scr.shape, -jnp.inf, F32)
        l_scr[...] = jnp.zeros(l_scr.shape, F32)
        acc_scr[...] = jnp.zeros(acc_scr.shape, F32)

    def update(s, values_of):
        m_old = m_scr[...]
        m_new = jnp.maximum(m_old, jnp.max(s, axis=-1, keepdims=True))
        alpha = jnp.exp2(m_old - m_new)
        p = jnp.exp2(s - m_new)
        l_scr[...] = alpha * l_scr[...] + jnp.sum(p, axis=-1, keepdims=True)
        pvs = [_mm(p[h * rows_h:(h + 1) * rows_h].astype(BF16), values_of(h)) for h in range(DIFF_HEADS)]
        acc_scr[...] = alpha * acc_scr[...] + jnp.concatenate(pvs, axis=0)
        m_scr[...] = m_new

    is_last = step == nsteps - 1

    def logits():
        kt = jnp.concatenate([kp[g][...].astype(BF16) for g in range(npg)], axis=1)
        s = _mm(qs, kt)
        if is_last:
            zeros = jnp.zeros((s.shape[0], s.shape[1] - PAGE_SIZE), F32)
            s = s + jnp.concatenate([zeros, d1_ref[...]], axis=1)
        return s

    def finish(s):
        update(s, lambda h: jnp.concatenate(
            [vp[g][pl.ds(h, PAGE_SIZE, stride=DIFF_HEADS), :].astype(BF16) for g in range(npg)], axis=0))
        if not is_last:
            return
        pad = jnp.zeros((PAGE_SIZE - tq, DIFF_VW), F32)
        kn = jnp.concatenate([kn_ref[...], pad], axis=0).astype(BF16)
        vn = jnp.concatenate([vn_ref[...], pad], axis=0).astype(BF16)
        update(_nt(qs, kn) + d0_ref[...], lambda h: vn[:, h * LANES:(h + 1) * LANES])
        o = acc_scr[...] / l_scr[...]
        lam = lam_ref[0]
        outs = []
        for h in range(DIFF_HEADS):
            oh = o[h * rows_h:h * rows_h + tq] - lam * o[h * rows_h + tq:(h + 1) * rows_h]
            ms = jnp.mean(oh * oh, axis=-1, keepdims=True)
            outs.append(oh * lax.rsqrt(ms + EPS) * dnw_ref[...] * out_scale)
        o_ref[...] = jnp.concatenate(outs, axis=1)

    return logits, finish


_MIX_CONSTS = ("w_out", "ln1_g", "ln1_b", "w_ffn_gate", "w_ffn_up", "w_ffn_down", "w_ple_gate", "w_ple_proj",
               "ln2_g", "ln2_b")


def _mix_head(oa_ref, ob_ref, oc_ref, x_ref, pe_ref, wo_ref, g1_ref, b1_ref, wpg_ref, wpp_ref):
    mix = _mm(oa_ref[...].astype(BF16), wo_ref[0:GLA_VW, :])
    mix = mix + _mm(ob_ref[...].astype(BF16), wo_ref[GLA_VW:GLA_VW + CONV_CH, :])
    mix = mix + _mm(oc_ref[...].astype(BF16), wo_ref[GLA_VW + CONV_CH:, :])
    x = _layer_norm(ALPHA * x_ref[...] + mix, g1_ref[...], b1_ref[...])
    xb = x.astype(BF16)
    ple = _sigmoid(_mm(xb, wpg_ref[...])) * _mm(pe_ref[...].astype(BF16), wpp_ref[...])
    return xb, ALPHA * x + ple


def _ffn_cols(xb, wg_ref, wu_ref, wd_ref, lo, hi):
    acc = None
    c = lo
    while c < hi:
        w = min(FFN_CHUNK, hi - c)
        gate = _mm(xb, wg_ref[:, c:c + w])
        hid = gate * _sigmoid(gate) * _mm(xb, wu_ref[:, c:c + w])
        part = _mm(hid.astype(BF16), wd_ref[c:c + w, :])
        acc = part if acc is None else acc + part
        c += w
    return acc


def _mix_ffn_body(oa_ref, ob_ref, oc_ref, x_ref, pe_ref, wo_ref, g1_ref, b1_ref, wg_ref, wu_ref, wd_ref, wpg_ref,
                  wpp_ref, g2_ref, b2_ref, y_ref):
    xb, base = _mix_head(oa_ref, ob_ref, oc_ref, x_ref, pe_ref, wo_ref, g1_ref, b1_ref, wpg_ref, wpp_ref)
    acc = base + _ffn_cols(xb, wg_ref, wu_ref, wd_ref, 0, wg_ref.shape[1])
    y_ref[...] = _layer_norm(acc, g2_ref[...], b2_ref[...])


def _mix_ffn(oa, ob, oc, x2, pe_all, layer, lw):
    m, d = x2.shape
    tm = min(MIX_ROWS, m)
    nt = m // tm
    consts = [lw[n] for n in _MIX_CONSTS]

    def rows(n):
        return pl.BlockSpec((tm, n), lambda i: (i, 0))

    return pl.pallas_call(
        _mix_ffn_body,
        grid=(nt,),
        in_specs=[rows(GLA_VW), rows(CONV_CH), rows(DIFF_VW), rows(d),
                  pl.BlockSpec((tm, pe_all.shape[1]), lambda i: (layer * nt + i, 0))]
                 + [_const_spec(a.shape, layer if a.ndim == 3 else None) for a in consts],
        out_specs=rows(d),
        out_shape=jax.ShapeDtypeStruct((m, d), F32),
        compiler_params=_params("parallel"),
        name="mix_ffn",
    )(oa, ob, oc, x2, pe_all, *consts)


def _sattn_body(pt_ref, lam_ref, q_ref, kn_ref, vn_ref, d1_ref, d0_ref, dnw_ref, *rest, npg, nsteps, out_scale):
    del pt_ref
    kp = rest[:npg]
    vp = rest[npg:2 * npg]
    o_ref = rest[2 * npg]
    m_scr, l_scr, acc_scr = rest[2 * npg + 1:]
    p_id = pl.program_id(1)
    for step in range(nsteps):
        @pl.when(p_id == step)
        def _(step=step):
            logits, finish = _sattn_phases(step, nsteps, lam_ref, q_ref, kn_ref, vn_ref, d1_ref, d0_ref, dnw_ref,
                                           kp, vp, o_ref, m_scr, l_scr, acc_scr, out_scale)
            finish(logits())


def _attn_sample(dq, dk, dv, cache_kt, cache_v2, page_table, layer, bias, lam, dnw, out_scale):
    nb, n_pages = page_table.shape
    tq = dq.shape[0] // nb
    npg = PAGES_PER_STEP
    nsteps = n_pages // npg
    assert n_pages % npg == 0 and cache_kt.shape[-1] == PAGE_SIZE and tq == SUBLANES
    nh = 2 * DIFF_HEADS
    t = bias.shape[-1]
    d1 = bias[1, :, 0:tq, t - PAGE_SIZE:t].reshape(nh * tq, PAGE_SIZE)
    d0 = jnp.concatenate([bias[0, :, 0:tq, 0:tq], jnp.full((nh, tq, PAGE_SIZE - tq), -jnp.inf, F32)],
                         axis=-1).reshape(nh * tq, PAGE_SIZE)

    def page_spec(g):
        return pl.BlockSpec((None, None, DIFF_VW, PAGE_SIZE),
                            lambda b, p, pt: (layer, pt[b * n_pages + p * npg + g], 0, 0))

    def rows_spec():
        return pl.BlockSpec((tq, DIFF_VW), lambda b, p, pt: (b, 0))

    def full_spec(a):
        nd = a.ndim
        return pl.BlockSpec(a.shape, lambda b, p, pt: (0,) * nd)

    grid_spec = pltpu.PrefetchScalarGridSpec(
        num_scalar_prefetch=1,
        grid=(nb, nsteps),
        in_specs=[pl.BlockSpec(memory_space=pltpu.SMEM), rows_spec(), rows_spec(), rows_spec(),
                  full_spec(d1), full_spec(d0), full_spec(dnw)]
                 + [page_spec(g) for g in range(npg)] + [page_spec(g) for g in range(npg)],
        out_specs=rows_spec(),
        scratch_shapes=[pltpu.VMEM((nh * tq, 1), F32), pltpu.VMEM((nh * tq, 1), F32),
                        pltpu.VMEM((nh * tq, LANES), F32)],
    )
    return pl.pallas_call(
        functools.partial(_sattn_body, npg=npg, nsteps=nsteps, out_scale=out_scale),
        grid_spec=grid_spec,
        out_shape=jax.ShapeDtypeStruct((nb * tq, DIFF_VW), F32),
        compiler_params=_params("parallel", "arbitrary"),
        name="diff_attn_sample",
    )(page_table.reshape(-1), lam, dq, dk, dv, d1, d0, dnw, *([cache_kt] * npg), *([cache_v2] * npg))


def _row(a):
    return a.reshape(1, -1)


def _token_mixers(x2, nb, t, lw, s0, buf, transposed_k):
    gq, gk, gv, gs, lg, u, dq, dv, *kk = _in_proj(x2, lw["w_in"], lw["wk_t"], lw["wg"], lw["bg"], nb, transposed_k,
                                                  lw["layer"])
    o_a, s_new = _gla(gq, gk, gv, lg, gs, lw["gla_nw"], s0.reshape(nb, GLA_QK, GLA_DV), nb, t)
    o_b, nbuf = _conv(u, buf, lw["conv_w"], lw["conv_b"], lw["conv_g"], lw["conv_beta"], nb, t)
    return o_a, o_b, dq, dv, kk, s_new.reshape(nb, GLA_HEADS, GLA_DK, GLA_DV), nbuf


def _prep_w_in(w):
    sizes = (GLA_QK, GLA_QK, GLA_VW, GLA_VW, GLA_GATE_RANK, CONV_CH, CONV_CH, DIFF_VW, DIFF_VW, DIFF_VW)
    parts, s = [], 0
    for n in sizes:
        parts.append(w[:, :, s:s + n])
        s += n
    wk_t = jnp.swapaxes(parts[8], 1, 2).astype(BF16)
    parts[4] = jnp.pad(parts[4], ((0, 0), (0, 0), (0, LANES - GLA_GATE_RANK)))
    return jnp.concatenate(parts, axis=2).astype(BF16), wk_t


def _stacked_weights(w_in, gla_w_gate_up, w_out, w_ffn_gate, w_ffn_up, w_ffn_down, w_ple_gate, w_ple_proj):
    w_in_p, wk_t = _prep_w_in(w_in)
    return dict(
        w_in=w_in_p, wk_t=wk_t,
        wg=jnp.pad(gla_w_gate_up, ((0, 0), (0, LANES - GLA_GATE_RANK), (0, 0))).astype(BF16),
        w_out=w_out.astype(BF16), w_ffn_gate=w_ffn_gate.astype(BF16), w_ffn_up=w_ffn_up.astype(BF16),
        w_ffn_down=w_ffn_down.astype(BF16), w_ple_gate=w_ple_gate.astype(BF16), w_ple_proj=w_ple_proj.astype(BF16),
    )


def _layer_vectors(i, gla_b_gate, gla_norm_w, conv_w, conv_b, conv_ln_g, conv_ln_b, diff_lq1, diff_lk1, diff_lq2,
                   diff_lk2, diff_norm_w, ln1_g, ln1_b, ln2_g, ln2_b):
    lam_init = 0.8 - 0.6 * math.exp(-0.3 * i)
    lam = (jnp.exp(jnp.sum(diff_lq1[i] * diff_lk1[i])) - jnp.exp(jnp.sum(diff_lq2[i] * diff_lk2[i]))
           + lam_init).reshape(1).astype(F32)
    return dict(
        layer=i,
        bg=_row(gla_b_gate[i]),
        gla_nw=_row(jnp.tile(gla_norm_w[i], GLA_HEADS)),
        conv_w=conv_w[i], conv_b=_row(conv_b[i]), conv_g=_row(conv_ln_g[i]), conv_beta=_row(conv_ln_b[i]),
        lam=lam, dnw=_row(diff_norm_w[i]),
        ln1_g=_row(ln1_g[i]), ln1_b=_row(ln1_b[i]), ln2_g=_row(ln2_g[i]), ln2_b=_row(ln2_b[i]),
    )


def kernel(x_prompt, x_sample, cache_k, cache_v, state_gla, state_conv, page_table, p_prompt, p_sample, w_in, gla_w_gate_up, gla_b_gate, gla_norm_w, conv_w, conv_b, conv_ln_g, conv_ln_b, diff_lq1, diff_lk1, diff_lq2, diff_lk2, diff_norm_w, rel_bias, w_out, ln1_g, ln1_b, w_ffn_gate, w_ffn_up, w_ffn_down, w_ple_gate, w_ple_proj, ln2_g, ln2_b):
    nbp, s, d = x_prompt.shape
    nbs, ts, _ = x_sample.shape
    depth, n_pool = cache_k.shape[:2]
    assert depth == DEPTH and cache_k.shape[2] == PAGE_SIZE
    cache_kt = jnp.transpose(cache_k, (0, 1, 3, 4, 2)).reshape(depth, n_pool, DIFF_VW, PAGE_SIZE)
    cache_v2 = cache_v.reshape(depth, n_pool, PAGE_SIZE * DIFF_HEADS, 2 * DIFF_DH)
    bias = _bias_tiles(rel_bias, ATTN_TILE)

    pe_prompt = p_prompt.reshape(depth * nbp * s, -1)
    pe_sample = p_sample.reshape(depth * nbs * ts, -1)
    yp = x_prompt.reshape(nbp * s, d)
    ys = x_sample.reshape(nbs * ts, d)
    zero_state = jnp.zeros((nbp, GLA_HEADS, GLA_DK, GLA_DV), F32)
    zero_buf = jnp.zeros((nbp, CONV_WIDTH - 1, CONV_CH), F32)
    outs = [[] for _ in range(6)]
    kts, vrs = [], []
    stacked = _stacked_weights(w_in, gla_w_gate_up, w_out, w_ffn_gate, w_ffn_up, w_ffn_down, w_ple_gate, w_ple_proj)
    for i in range(depth):
        lw = dict(stacked, **_layer_vectors(i, gla_b_gate, gla_norm_w, conv_w, conv_b, conv_ln_g, conv_ln_b, diff_lq1,
                                            diff_lk1, diff_lq2, diff_lk2, diff_norm_w, ln1_g, ln1_b, ln2_g, ln2_b))

        out_scale = 1.0 - (0.8 - 0.6 * math.exp(-0.3 * i))
        pa, pb, q16, v_rows, (kt_i, kt16, v16), s_i, c_i = _token_mixers(yp, nbp, s, lw, zero_state, zero_buf, True)
        kts.append(kt_i)
        vrs.append(v_rows)
        if i < depth - 1:
            pc = _attn_prompt(q16, kt16, v16, bias, lw["lam"], lw["dnw"], nbp, s, out_scale)
        else:
            pc, k_stack, v_stack = _attn_prompt(q16, kt16, v16, bias, lw["lam"], lw["dnw"], nbp, s, out_scale,
                                                kts, vrs)
        outs[0].append(s_i)
        outs[1].append(c_i)
        sa, sb, dq, dv, (dk,), s_i, c_i = _token_mixers(ys, nbs, ts, lw, state_gla[i], state_conv[i], False)
        outs[2].append(dk.reshape(nbs, ts, 2 * DIFF_HEADS, DIFF_DH))
        outs[3].append(dv.reshape(nbs, ts, DIFF_HEADS, 2 * DIFF_DH))
        outs[4].append(s_i)
        outs[5].append(c_i)
        sc = _attn_sample(dq, dk, dv, cache_kt, cache_v2, page_table, i, bias, lw["lam"], lw["dnw"], out_scale)
        yp = _mix_ffn(pa, pb, pc, yp, pe_prompt, i, lw)
        ys = _mix_ffn(sa, sb, sc, ys, pe_sample, i, lw)
    k_prompt = k_stack.reshape(depth, nbp, 2 * DIFF_HEADS, DIFF_DH, s).transpose(0, 1, 4, 2, 3)
    v_prompt = v_stack.reshape(depth, nbp, s, DIFF_HEADS, 2 * DIFF_DH)
    return (yp.reshape(nbp, s, d), ys.reshape(nbs, ts, d), k_prompt, v_prompt) + tuple(jnp.stack(o) for o in outs)
```

```python
import functools
import math

import jax
import jax.numpy as jnp
from jax import lax
from jax.experimental import pallas as pl
from jax.experimental.pallas import tpu as pltpu

F32 = jnp.float32
BF16 = jnp.bfloat16

GLA_HEADS = 4
GLA_DK = 32
GLA_DV = 64
GLA_QK = GLA_HEADS * GLA_DK
GLA_VW = GLA_HEADS * GLA_DV
GLA_GATE_RANK = 16
GLA_TAU = 16.0
CONV_CH = 256
CONV_WIDTH = 31
DIFF_HEADS = 4
DIFF_DH = 64
DIFF_VW = 2 * DIFF_HEADS * DIFF_DH
REL_BUCKETS = 32
REL_MAX_DIST = 128
PAGE_SIZE = 128
DEPTH = 2
ALPHA = (2 * DEPTH) ** 0.25
EPS = 1e-5
LOG2E = math.log2(math.e)

LANES = 128
SUBLANES = 8
PROJ_ROWS = 512
GLA_TILE = 256
GLA_SUBTILE = 128
GLA_BLOCK = 16
CONV_TILE = 512
CONV_HALO = 32
SEQS_PER_STEP = 8
ATTN_TILE = 256
PAGES_PER_STEP = 16
MIX_ROWS = 512
FFN_CHUNK = 256
VMEM_LIMIT = 56 * 1024 * 1024

_IN_OFF = {}
_o = 0
for _name, _n in (("gq", GLA_QK), ("gk", GLA_QK), ("gv", GLA_VW), ("gg", GLA_VW), ("glr", LANES), ("ca", CONV_CH),
                  ("cg", CONV_CH), ("dq", DIFF_VW), ("dk", DIFF_VW), ("dv", DIFF_VW)):
    _IN_OFF[_name] = (_o, _n)
    _o += _n
N_IN_PAD = _o


def _params(*sem):
    return pltpu.CompilerParams(dimension_semantics=sem, vmem_limit_bytes=VMEM_LIMIT)


def _const_spec(shape, layer=None):
    if layer is None:
        nd = len(shape)
        return pl.BlockSpec(shape, lambda *_: (0,) * nd, pipeline_mode=pl.Buffered(1))
    nd = len(shape) - 1
    return pl.BlockSpec((None,) + tuple(shape[1:]), lambda *_: (layer,) + (0,) * nd, pipeline_mode=pl.Buffered(1))


def _nt(a, b):
    return lax.dot_general(a, b, (((1,), (1,)), ((), ())), preferred_element_type=F32)


def _mm(a, b):
    return jnp.dot(a, b, preferred_element_type=F32)


def _mm_split(a, b):
    hi = a.astype(BF16)
    lo = (a - hi.astype(F32)).astype(BF16)
    return _mm(hi, b) + _mm(lo, b)


def _sigmoid(x):
    return 1.0 / (1.0 + jnp.exp(-x))


def _layer_norm(x, g, b):
    mu = jnp.mean(x, axis=-1, keepdims=True)
    xc = x - mu
    var = jnp.mean(xc * xc, axis=-1, keepdims=True)
    return xc * lax.rsqrt(var + EPS) * g + b


def _seqs_per_step(nb, nj):
    if nj > 1:
        return 1
    return math.gcd(nb, SEQS_PER_STEP)


def _fold_lanes(x, op):
    acc = x[:, 0:LANES]
    for c in range(1, x.shape[1] // LANES):
        acc = op(acc, x[:, c * LANES:(c + 1) * LANES])
    return acc


def _inproj_body(x_ref, w_ref, wkt_ref, wg_ref, bg_ref, gq_ref, gk_ref, gv_ref, gs_ref, lg_ref, u_ref, dq_ref,
                 dv_ref, *k_refs, transposed_k):
    x = x_ref[...].astype(BF16)

    def proj(name):
        lo, n = _IN_OFF[name]
        return _mm(x, w_ref[:, lo:lo + n])

    gq_ref[...] = proj("gq") * GLA_DK ** -0.5
    gk_ref[...] = proj("gk")
    gv_ref[...] = proj("gv")
    gg = proj("gg")
    gs_ref[...] = gg * _sigmoid(gg)
    z = _mm(proj("glr").astype(BF16), wg_ref[...]) + bg_ref[...]
    lg_ref[...] = (jnp.minimum(z, 0.0) - jnp.log(1.0 + jnp.exp(-jnp.abs(z)))) * (1.0 / GLA_TAU)
    u_ref[...] = proj("ca") * _sigmoid(proj("cg"))
    dq_ref[...] = (proj("dq") * (DIFF_DH ** -0.5 * LOG2E)).astype(dq_ref.dtype)
    dv = proj("dv")
    if transposed_k:
        kt_ref, kt16_ref, v16_ref = k_refs
        tm = x.shape[0]
        for h in range(DIFF_HEADS):
            dv_ref[pl.ds(h, tm, stride=DIFF_HEADS), :] = dv[:, h * LANES:(h + 1) * LANES]
        kt = _nt(wkt_ref[...], x)
        kt_ref[0] = kt
        kt16_ref[0] = kt.astype(BF16)
        v16_ref[...] = dv.astype(BF16)
    else:
        dv_ref[...] = dv
        k_refs[0][...] = proj("dk")


def _in_proj(x2, w_in_p, wk_t, wg_p, bg, nb, transposed_k, layer):
    m, d = x2.shape
    t = m // nb
    tm = min(PROJ_ROWS, t if transposed_k else m)
    nj = t // tm
    widths = (GLA_QK, GLA_QK, GLA_VW, GLA_VW, GLA_QK, CONV_CH, DIFF_VW)
    dtypes = [F32] * 6 + [BF16 if transposed_k else F32]
    out_specs = [pl.BlockSpec((tm, n), lambda i: (i, 0)) for n in widths]
    out_shape = [jax.ShapeDtypeStruct((m, n), dt) for n, dt in zip(widths, dtypes)]
    if transposed_k:
        out_specs.append(pl.BlockSpec((tm * DIFF_HEADS, LANES), lambda i: (i, 0)))
        out_shape.append(jax.ShapeDtypeStruct((m * DIFF_HEADS, LANES), F32))
        kt_spec = pl.BlockSpec((1, DIFF_VW, tm), lambda i: (i // nj, 0, i % nj))
        out_specs += [kt_spec, kt_spec, pl.BlockSpec((tm, DIFF_VW), lambda i: (i, 0))]
        out_shape += [jax.ShapeDtypeStruct((nb, DIFF_VW, t), F32), jax.ShapeDtypeStruct((nb, DIFF_VW, t), BF16),
                      jax.ShapeDtypeStruct((m, DIFF_VW), BF16)]
    else:
        out_specs += [pl.BlockSpec((tm, DIFF_VW), lambda i: (i, 0))] * 2
        out_shape += [jax.ShapeDtypeStruct((m, DIFF_VW), F32)] * 2
    return pl.pallas_call(
        functools.partial(_inproj_body, transposed_k=transposed_k),
        grid=(m // tm,),
        in_specs=[pl.BlockSpec((tm, d), lambda i: (i, 0)),
                  _const_spec(w_in_p.shape, layer), _const_spec(wk_t.shape, layer), _const_spec(wg_p.shape, layer),
                  _const_spec(bg.shape)],
        out_specs=out_specs,
        out_shape=out_shape,
        compiler_params=_params("parallel"),
        name="in_proj",
    )(x2, w_in_p, wk_t, wg_p, bg)


def _gla_body(q_ref, k_ref, v_ref, lg_ref, gs_ref, nw_ref, s0_ref, o_ref, sout_ref, s_scr, *, cb, nj):
    nseq = s0_ref.shape[0]
    r = q_ref.shape[0] // nseq
    rs = min(r, GLA_SUBTILE)
    row = lax.broadcasted_iota(jnp.int32, (rs, 1), 0) % cb

    same_head = (lax.broadcasted_iota(jnp.int32, (GLA_QK, GLA_VW), 0) // GLA_DK
                 == lax.broadcasted_iota(jnp.int32, (GLA_QK, GLA_VW), 1) // GLA_DV)
    expand = same_head.astype(BF16)
    mask_t = (lax.broadcasted_iota(jnp.int32, (GLA_VW, GLA_QK), 0) // GLA_DV
              == lax.broadcasted_iota(jnp.int32, (GLA_VW, GLA_QK), 1) // GLA_DK).astype(F32)
    grp = (lax.broadcasted_iota(jnp.int32, (GLA_VW, GLA_VW), 0) // GLA_DV
           == lax.broadcasted_iota(jnp.int32, (GLA_VW, GLA_VW), 1) // GLA_DV)
    head_mean = jnp.where(grp, 1.0 / GLA_DV, 0.0).astype(BF16)

    def run_sequence(first_row, st):
        for sub in range(r // rs):
            rows = pl.ds(first_row + sub * rs, rs)
            q = q_ref[rows, :]
            k = k_ref[rows, :]
            v = v_ref[rows, :]

            b = lg_ref[rows, :]
            s = 1
            while s < cb:
                b = b + jnp.where(row >= s, pltpu.roll(b, s, 0), 0.0)
                s *= 2

            o = _mm((q * k).astype(BF16), expand) * v
            for delta in range(1, cb):
                ks = pltpu.roll(k, delta, 0)
                bs = pltpu.roll(b, delta, 0)
                vs = pltpu.roll(v, delta, 0)
                p = jnp.where(row >= delta, q * ks * jnp.exp(b - bs), 0.0)
                o = o + _mm(p.astype(BF16), expand) * vs

            inter = []
            for t in range(rs // cb):
                sl = slice(t * cb, (t + 1) * cb)
                bt = b[sl]
                bend = bt[cb - 1:cb]
                qe = (q[sl] * jnp.exp(bt)).astype(BF16)
                inter.append(_nt(qe, st.astype(BF16)))
                ke = k[sl] * jnp.exp(bend - bt)
                vt = v[sl]
                if cb < 16:
                    ke = jnp.concatenate([ke, jnp.zeros((16 - cb, GLA_QK), F32)], axis=0)
                    vt = jnp.concatenate([vt, jnp.zeros((16 - cb, GLA_VW), F32)], axis=0)
                kv = lax.dot_general(vt.astype(BF16), ke.astype(BF16), (((0,), (0,)), ((), ())),
                                     preferred_element_type=F32)
                st = st * jnp.exp(bend) + kv * mask_t
            o = o + (inter[0] if len(inter) == 1 else jnp.concatenate(inter, axis=0))

            mean_sq = _mm_split(o * o, head_mean)
            o_ref[rows, :] = o * lax.rsqrt(mean_sq + EPS) * nw_ref[...] * gs_ref[rows, :]
        return st

    def load_state(seq):
        z = s0_ref[seq]
        z = jnp.concatenate([z] * GLA_HEADS, axis=1)
        return jnp.where(same_head, z, 0.0).T

    def store_state(seq, st):
        z = st.T
        out = z[:, 0:GLA_DV]
        for h in range(1, GLA_HEADS):
            out = out + z[:, h * GLA_DV:(h + 1) * GLA_DV]
        sout_ref[seq] = out

    if nj == 1:
        for seq in range(nseq):
            store_state(seq, run_sequence(seq * r, load_state(seq)))
    else:
        j = pl.program_id(1)

        @pl.when(j == 0)
        def _():
            s_scr[...] = load_state(0)

        st = run_sequence(0, s_scr[...])
        s_scr[...] = st

        @pl.when(j == nj - 1)
        def _():
            store_state(0, st)


def _gla(gq, gk, gv, lg, gs, nw, s0, nb, t):
    cb = GLA_BLOCK if t % GLA_BLOCK == 0 else t
    r = min(t, GLA_TILE)
    nj = t // r
    m = nb * t
    nseq = _seqs_per_step(nb, nj)

    def rows(n):
        return pl.BlockSpec((nseq * r, n), lambda b, j: (b * nj + j, 0))

    return pl.pallas_call(
        functools.partial(_gla_body, cb=cb, nj=nj),
        grid=(nb // nseq, nj),
        in_specs=[rows(GLA_QK), rows(GLA_QK), rows(GLA_VW), rows(GLA_QK), rows(GLA_VW),
                  _const_spec(nw.shape),
                  pl.BlockSpec((nseq, GLA_QK, GLA_DV), lambda b, j: (b, 0, 0))],
        out_specs=[rows(GLA_VW), pl.BlockSpec((nseq, GLA_QK, GLA_DV), lambda b, j: (b, 0, 0))],
        out_shape=[jax.ShapeDtypeStruct((m, GLA_VW), F32), jax.ShapeDtypeStruct((nb, GLA_QK, GLA_DV), F32)],
        scratch_shapes=[pltpu.VMEM((GLA_VW, GLA_QK), F32)],
        compiler_params=_params("parallel", "arbitrary"),
        name="gla",
    )(gq, gk, gv, lg, gs, nw, s0)


def _conv_body(u_ref, buf_ref, cw_ref, cb_ref, g_ref, beta_ref, y_ref, nbuf_ref, seq_scr, *, nj):
    nseq = buf_ref.shape[0]
    r = u_ref.shape[0] // nseq
    hist = CONV_WIDTH - 1

    def start_window(scr, seq):
        scr[0:CONV_HALO - hist, :] = jnp.zeros((CONV_HALO - hist, CONV_CH), F32)
        scr[CONV_HALO - hist:CONV_HALO, :] = buf_ref[seq]

    def run_tile(scr, first_row):
        scr[CONV_HALO:CONV_HALO + r, :] = u_ref[pl.ds(first_row, r), :]
        window = scr[...]
        rows = window.shape[0]
        acc = jnp.zeros((r, CONV_CH), F32) + cb_ref[...]
        for rho in range(SUBLANES):
            lo = CONV_HALO - hist + rho
            shifted = pltpu.roll(window, rows - lo, 0)
            for k in range(-(-CONV_WIDTH // SUBLANES)):
                w = SUBLANES * k + rho
                if w < CONV_WIDTH:
                    acc = acc + shifted[SUBLANES * k:SUBLANES * k + r, :] * cw_ref[w:w + 1, :]
        y = _layer_norm(acc, g_ref[...], beta_ref[...])
        y_ref[pl.ds(first_row, r), :] = y * _sigmoid(y)

    def last_rows(scr):
        return scr[r + CONV_HALO - hist:r + CONV_HALO, :]

    if nj == 1:
        for seq in range(nseq):
            scr = seq_scr.at[seq]
            start_window(scr, seq)
            run_tile(scr, seq * r)
            nbuf_ref[seq] = last_rows(scr)
    else:
        j = pl.program_id(1)
        scr = seq_scr.at[0]

        @pl.when(j == 0)
        def _():
            start_window(scr, 0)

        @pl.when(j > 0)
        def _():
            scr[0:CONV_HALO, :] = scr[r:r + CONV_HALO, :]

        run_tile(scr, 0)

        @pl.when(j == nj - 1)
        def _():
            nbuf_ref[0] = last_rows(scr)


def _conv(u, buf, cw, cb, g, beta, nb, t):
    r = min(t, CONV_TILE)
    nj = t // r
    assert nj == 1 or r >= CONV_HALO
    hist = CONV_WIDTH - 1
    nseq = _seqs_per_step(nb, nj)
    return pl.pallas_call(
        functools.partial(_conv_body, nj=nj),
        grid=(nb // nseq, nj),
        in_specs=[pl.BlockSpec((nseq * r, CONV_CH), lambda b, j: (b * nj + j, 0)),
                  pl.BlockSpec((nseq, hist, CONV_CH), lambda b, j: (b, 0, 0)),
                  _const_spec(cw.shape), _const_spec(cb.shape), _const_spec(g.shape), _const_spec(beta.shape)],
        out_specs=[pl.BlockSpec((nseq * r, CONV_CH), lambda b, j: (b * nj + j, 0)),
                   pl.BlockSpec((nseq, hist, CONV_CH), lambda b, j: (b, 0, 0))],
        out_shape=[jax.ShapeDtypeStruct((nb * t, CONV_CH), F32), jax.ShapeDtypeStruct((nb, hist, CONV_CH), F32)],
        scratch_shapes=[pltpu.VMEM((nseq, r + CONV_HALO, CONV_CH), F32)],
        compiler_params=_params("parallel", "arbitrary"),
        name="conv",
    )(u, buf, cw, cb, g, beta)


def _rel_bucket(dist):
    n = jnp.maximum(dist, 0)
    max_exact = REL_BUCKETS // 2
    nf = jnp.maximum(n, 1).astype(F32)
    large = max_exact + (jnp.log(nf / max_exact) / math.log(REL_MAX_DIST / max_exact)
                         * (REL_BUCKETS - max_exact)).astype(jnp.int32)
    large = jnp.minimum(large, REL_BUCKETS - 1)
    return jnp.where(n < max_exact, n, large)


def _bias_body(rb_ref, idx_ref, o_ref):
    h = pl.program_id(1)
    idx = idx_ref[0]
    acc = jnp.full(idx.shape, -jnp.inf, F32)
    for bucket in range(REL_BUCKETS):
        acc = jnp.where(idx == bucket, rb_ref[bucket, h], acc)
    o_ref[0, 0] = (acc - rb_ref[REL_BUCKETS - 1, h]) * LOG2E


def _bias_tiles(rel_bias, t):
    assert t >= REL_MAX_DIST
    ii = jnp.arange(t, dtype=jnp.int32)[:, None]
    jj = jnp.arange(t, dtype=jnp.int32)[None, :]
    idx = jnp.stack([jnp.where(ii >= jj, _rel_bucket(ii - jj), -1), _rel_bucket(t + ii - jj)])
    nh = rel_bias.shape[1]
    return pl.pallas_call(
        _bias_body,
        grid=(2, nh),
        in_specs=[pl.BlockSpec(memory_space=pltpu.SMEM),
                  pl.BlockSpec((1, t, t), lambda r, h: (r, 0, 0))],
        out_specs=pl.BlockSpec((1, 1, t, t), lambda r, h: (r, h, 0, 0)),
        out_shape=jax.ShapeDtypeStruct((2, nh, t, t), F32),
        compiler_params=_params("arbitrary", "arbitrary"),
        name="rel_bias_tiles",
    )(rel_bias, idx)


def _attn_body(lam_ref, q_ref, kt_ref, v_ref, bias_ref, dnw_ref, *rest, out_scale, nstack):
    stack_in = rest[:2 * nstack]
    o_ref = rest[2 * nstack]
    stack_out = rest[2 * nstack + 1:2 * nstack + 1 + (2 if nstack else 0)]
    q_scr, s_scr, m_scr, l_scr, acc_scr = rest[len(rest) - 5:]
    for layer in range(nstack):
        stack_out[0][layer, 0] = stack_in[layer][0]
        stack_out[1][layer] = stack_in[nstack + layer][...]
    i = pl.program_id(1)
    t = q_ref.shape[0]
    nmaps = 2 * DIFF_HEADS
    lane = lax.broadcasted_iota(jnp.int32, (t, LANES), 1)
    for n in range(nmaps):
        qh = q_ref[:, (n // 2) * LANES:(n // 2 + 1) * LANES]
        keep = (lane < DIFF_DH) if n % 2 == 0 else (lane >= DIFF_DH)
        q_scr[n] = jnp.where(keep, qh, jnp.zeros_like(qh))
    m_scr[...] = jnp.full(m_scr.shape, -jnp.inf, F32)
    l_scr[...] = jnp.zeros(l_scr.shape, F32)
    acc_scr[...] = jnp.zeros(acc_scr.shape, F32)

    def tile(j, which, ntile=1):
        width = ntile * t
        keys = pl.ds(pl.multiple_of(j * t, t), width)
        for n in range(nmaps):
            s = _mm(q_scr[n], kt_ref[0, (n // 2) * LANES:(n // 2 + 1) * LANES, keys])
            if which == "near":
                s = s + jnp.concatenate([bias_ref[1, n], bias_ref[0, n]], axis=1)
            elif which is not None:
                s = s + bias_ref[which, n]
            s_scr[n, :, 0:width] = s
        for n in range(nmaps):
            s = s_scr[n, :, 0:width]
            m_old = m_scr[n]
            row_max = jnp.max(_fold_lanes(s, jnp.maximum), axis=-1, keepdims=True)
            m_new = jnp.maximum(m_old, jnp.broadcast_to(row_max, (t, LANES)))
            alpha = jnp.exp2(m_old - m_new)
            ps = [jnp.exp2(s[:, c * LANES:(c + 1) * LANES] - m_new) for c in range(width // LANES)]
            part = ps[0]
            for pc in ps[1:]:
                part = part + pc
            l_scr[n] = alpha * l_scr[n] + part
            pv = _mm(jnp.concatenate(ps, axis=1).astype(BF16), v_ref[keys, (n // 2) * LANES:(n // 2 + 1) * LANES])
            acc_scr[n] = alpha * acc_scr[n] + pv
            m_scr[n] = m_new

    nfar = jnp.maximum(i - 1, 0)

    def far_pair(c, carry):
        tile(2 * c, None, 2)
        return carry

    lax.fori_loop(0, nfar // 2, far_pair, 0)

    @pl.when(nfar % 2 == 1)
    def _():
        tile(nfar - 1, None)

    @pl.when(i >= 1)
    def _():
        tile(i - 1, "near", 2)

    @pl.when(i == 0)
    def _():
        tile(0, 0)

    lam = lam_ref[0]
    outs = []
    for hp in range(DIFF_HEADS):
        l0 = jnp.sum(l_scr[2 * hp], axis=-1, keepdims=True)
        l1 = jnp.sum(l_scr[2 * hp + 1], axis=-1, keepdims=True)
        o = acc_scr[2 * hp] / l0 - lam * (acc_scr[2 * hp + 1] / l1)
        ms_o = jnp.mean(o * o, axis=-1, keepdims=True)
        outs.append(o * lax.rsqrt(ms_o + EPS) * dnw_ref[...] * out_scale)
    o_ref[...] = jnp.concatenate(outs, axis=1)


def _attn_prompt(q16, kt16, v16, bias, lam, dnw, nb, s, out_scale, stack_kt=(), stack_v=()):
    t = bias.shape[-1]
    nq = s // t
    nmaps = 2 * DIFF_HEADS
    nstack = len(stack_kt)
    assert len(stack_v) == nstack
    in_specs = [pl.BlockSpec(memory_space=pltpu.SMEM),
                pl.BlockSpec((t, DIFF_VW), lambda b, i: (b * nq + i, 0)),
                pl.BlockSpec((1, DIFF_VW, s), lambda b, i: (b, 0, 0)),
                pl.BlockSpec((s, DIFF_VW), lambda b, i: (b, 0)),
                _const_spec(bias.shape), _const_spec(dnw.shape)]
    in_specs += [pl.BlockSpec((1, DIFF_VW, t), lambda b, i: (b, 0, i))] * nstack
    in_specs += [pl.BlockSpec((t * DIFF_HEADS, LANES), lambda b, i: (b * nq + i, 0))] * nstack
    out_specs = [pl.BlockSpec((t, DIFF_VW), lambda b, i: (b * nq + i, 0))]
    out_shape = [jax.ShapeDtypeStruct((nb * s, DIFF_VW), F32)]
    if nstack:
        out_specs += [pl.BlockSpec((nstack, 1, DIFF_VW, t), lambda b, i: (0, b, 0, i)),
                      pl.BlockSpec((nstack, t * DIFF_HEADS, LANES), lambda b, i: (0, b * nq + i, 0))]
        out_shape += [jax.ShapeDtypeStruct((nstack, nb, DIFF_VW, s), F32),
                      jax.ShapeDtypeStruct((nstack, nb * s * DIFF_HEADS, LANES), F32)]
    out = pl.pallas_call(
        functools.partial(_attn_body, out_scale=out_scale, nstack=nstack),
        grid=(nb, nq),
        in_specs=in_specs,
        out_specs=out_specs,
        out_shape=out_shape,
        scratch_shapes=[pltpu.VMEM((nmaps, t, LANES), BF16), pltpu.VMEM((nmaps, t, 2 * t), F32),
                        pltpu.VMEM((nmaps, t, LANES), F32), pltpu.VMEM((nmaps, t, LANES), F32),
                        pltpu.VMEM((nmaps, t, LANES), F32)],
        compiler_params=_params("parallel", "arbitrary"),
        name="diff_attn_prompt",
    )(lam, q16, kt16, v16, bias, dnw, *stack_kt, *stack_v)
    return out if nstack else out[0]


def _sattn_phases(step, nsteps, lam_ref, q_ref, kn_ref, vn_ref, d1_ref, d0_ref, dnw_ref, kp, vp, o_ref, m_scr, l_scr,
                  acc_scr, out_scale):
    npg = len(kp)
    nh = 2 * DIFF_HEADS
    tq = q_ref.shape[0]
    rows_h = 2 * tq

    q = q_ref[...]
    lane_head = lax.broadcasted_iota(jnp.int32, q.shape, 1) // DIFF_DH
    qs = jnp.concatenate([jnp.where(lane_head == hh, q, 0.0) for hh in range(nh)], axis=0).astype(BF16)

    if step == 0:
        m_scr[...] = jnp.full(m_scr.shape, -jnp.inf, F32)
        l_scr[...] = jnp.zeros(l_scr.shape, F32)
        acc_scr[...] = jnp.zeros(acc_scr.shape, F32)

    def update(s, values_of):
        m_old = m_scr[...]
        m_new = jnp.maximum(m_old, jnp.max(s, axis=-1, keepdims=True))
        alpha = jnp.exp2(m_old - m_new)
        p = jnp.exp2(s - m_new)
        l_scr[...] = alpha * l_scr[...] + jnp.sum(p, axis=-1, keepdims=True)
        pvs = [_mm(p[h * rows_h:(h + 1) * rows_h].astype(BF16), values_of(h)) for h in range(DIFF_HEADS)]
        acc_scr[...] = alpha * acc_scr[...] + jnp.concatenate(pvs, axis=0)
        m_scr[...] = m_new

    is_last = step == nsteps - 1

    def logits():
        kt = jnp.concatenate([kp[g][...].astype(BF16) for g in range(npg)], axis=1)
        s = _mm(qs, kt)
        if is_last:
            zeros = jnp.zeros((s.shape[0], s.shape[1] - PAGE_SIZE), F32)
            s = s + jnp.concatenate([zeros, d1_ref[...]], axis=1)
        return s

    def finish(s):
        update(s, lambda h: jnp.concatenate(
            [vp[g][pl.ds(h, PAGE_SIZE, stride=DIFF_HEADS), :].astype(BF16) for g in range(npg)], axis=0))
        if not is_last:
            return
        pad = jnp.zeros((PAGE_SIZE - tq, DIFF_VW), F32)
        kn = jnp.concatenate([kn_ref[...], pad], axis=0).astype(BF16)
        vn = jnp.concatenate([vn_ref[...], pad], axis=0).astype(BF16)
        update(_nt(qs, kn) + d0_ref[...], lambda h: vn[:, h * LANES:(h + 1) * LANES])
        o = acc_scr[...] / l_scr[...]
        lam = lam_ref[0]
        outs = []
        for h in range(DIFF_HEADS):
            oh = o[h * rows_h:h * rows_h + tq] - lam * o[h * rows_h + tq:(h + 1) * rows_h]
            ms = jnp.mean(oh * oh, axis=-1, keepdims=True)
            outs.append(oh * lax.rsqrt(ms + EPS) * dnw_ref[...] * out_scale)
        o_ref[...] = jnp.concatenate(outs, axis=1)

    return logits, finish


_MIX_CONSTS = ("w_out", "ln1_g", "ln1_b", "w_ffn_gate", "w_ffn_up", "w_ffn_down", "w_ple_gate", "w_ple_proj",
               "ln2_g", "ln2_b")


def _mix_head(oa_ref, ob_ref, oc_ref, x_ref, pe_ref, wo_ref, g1_ref, b1_ref, wpg_ref, wpp_ref):
    mix = _mm(oa_ref[...].astype(BF16), wo_ref[0:GLA_VW, :])
    mix = mix + _mm(ob_ref[...].astype(BF16), wo_ref[GLA_VW:GLA_VW + CONV_CH, :])
    mix = mix + _mm(oc_ref[...].astype(BF16), wo_ref[GLA_VW + CONV_CH:, :])
    x = _layer_norm(ALPHA * x_ref[...] + mix, g1_ref[...], b1_ref[...])
    xb = x.astype(BF16)
    ple = _sigmoid(_mm(xb, wpg_ref[...])) * _mm(pe_ref[...].astype(BF16), wpp_ref[...])
    return xb, ALPHA * x + ple


def _ffn_cols(xb, wg_ref, wu_ref, wd_ref, lo, hi):
    acc = None
    c = lo
    while c < hi:
        w = min(FFN_CHUNK, hi - c)
        gate = _mm(xb, wg_ref[:, c:c + w])
        hid = gate * _sigmoid(gate) * _mm(xb, wu_ref[:, c:c + w])
        part = _mm(hid.astype(BF16), wd_ref[c:c + w, :])
        acc = part if acc is None else acc + part
        c += w
    return acc


def _mix_ffn_body(oa_ref, ob_ref, oc_ref, x_ref, pe_ref, wo_ref, g1_ref, b1_ref, wg_ref, wu_ref, wd_ref, wpg_ref,
                  wpp_ref, g2_ref, b2_ref, y_ref):
    xb, base = _mix_head(oa_ref, ob_ref, oc_ref, x_ref, pe_ref, wo_ref, g1_ref, b1_ref, wpg_ref, wpp_ref)
    acc = base + _ffn_cols(xb, wg_ref, wu_ref, wd_ref, 0, wg_ref.shape[1])
    y_ref[...] = _layer_norm(acc, g2_ref[...], b2_ref[...])


def _mix_ffn(oa, ob, oc, x2, pe_all, layer, lw):
    m, d = x2.shape
    tm = min(MIX_ROWS, m)
    nt = m // tm
    consts = [lw[n] for n in _MIX_CONSTS]

    def rows(n):
        return pl.BlockSpec((tm, n), lambda i: (i, 0))

    return pl.pallas_call(
        _mix_ffn_body,
        grid=(nt,),
        in_specs=[rows(GLA_VW), rows(CONV_CH), rows(DIFF_VW), rows(d),
                  pl.BlockSpec((tm, pe_all.shape[1]), lambda i: (layer * nt + i, 0))]
                 + [_const_spec(a.shape, layer if a.ndim == 3 else None) for a in consts],
        out_specs=rows(d),
        out_shape=jax.ShapeDtypeStruct((m, d), F32),
        compiler_params=_params("parallel"),
        name="mix_ffn",
    )(oa, ob, oc, x2, pe_all, *consts)


def _sattn_body(pt_ref, lam_ref, q_ref, kn_ref, vn_ref, d1_ref, d0_ref, dnw_ref, *rest, npg, nsteps, out_scale):
    del pt_ref
    kp = rest[:npg]
    vp = rest[npg:2 * npg]
    o_ref = rest[2 * npg]
    m_scr, l_scr, acc_scr = rest[2 * npg + 1:]
    p_id = pl.program_id(1)
    for step in range(nsteps):
        @pl.when(p_id == step)
        def _(step=step):
            logits, finish = _sattn_phases(step, nsteps, lam_ref, q_ref, kn_ref, vn_ref, d1_ref, d0_ref, dnw_ref,
                                           kp, vp, o_ref, m_scr, l_scr, acc_scr, out_scale)
            finish(logits())


def _attn_sample(dq, dk, dv, cache_kt, cache_v2, page_table, layer, bias, lam, dnw, out_scale):
    nb, n_pages = page_table.shape
    tq = dq.shape[0] // nb
    npg = PAGES_PER_STEP
    nsteps = n_pages // npg
    assert n_pages % npg == 0 and cache_kt.shape[-1] == PAGE_SIZE and tq == SUBLANES
    nh = 2 * DIFF_HEADS
    t = bias.shape[-1]
    d1 = bias[1, :, 0:tq, t - PAGE_SIZE:t].reshape(nh * tq, PAGE_SIZE)
    d0 = jnp.concatenate([bias[0, :, 0:tq, 0:tq], jnp.full((nh, tq, PAGE_SIZE - tq), -jnp.inf, F32)],
                         axis=-1).reshape(nh * tq, PAGE_SIZE)

    def page_spec(g):
        return pl.BlockSpec((None, None, DIFF_VW, PAGE_SIZE),
                            lambda b, p, pt: (layer, pt[b * n_pages + p * npg + g], 0, 0))

    def rows_spec():
        return pl.BlockSpec((tq, DIFF_VW), lambda b, p, pt: (b, 0))

    def full_spec(a):
        nd = a.ndim
        return pl.BlockSpec(a.shape, lambda b, p, pt: (0,) * nd)

    grid_spec = pltpu.PrefetchScalarGridSpec(
        num_scalar_prefetch=1,
        grid=(nb, nsteps),
        in_specs=[pl.BlockSpec(memory_space=pltpu.SMEM), rows_spec(), rows_spec(), rows_spec(),
                  full_spec(d1), full_spec(d0), full_spec(dnw)]
                 + [page_spec(g) for g in range(npg)] + [page_spec(g) for g in range(npg)],
        out_specs=rows_spec(),
        scratch_shapes=[pltpu.VMEM((nh * tq, 1), F32), pltpu.VMEM((nh * tq, 1), F32),
                        pltpu.VMEM((nh * tq, LANES), F32)],
    )
    return pl.pallas_call(
        functools.partial(_sattn_body, npg=npg, nsteps=nsteps, out_scale=out_scale),
        grid_spec=grid_spec,
        out_shape=jax.ShapeDtypeStruct((nb * tq, DIFF_VW), F32),
        compiler_params=_params("parallel", "arbitrary"),
        name="diff_attn_sample",
    )(page_table.reshape(-1), lam, dq, dk, dv, d1, d0, dnw, *([cache_kt] * npg), *([cache_v2] * npg))


def _row(a):
    return a.reshape(1, -1)


def _token_mixers(x2, nb, t, lw, s0, buf, transposed_k):
    gq, gk, gv, gs, lg, u, dq, dv, *kk = _in_proj(x2, lw["w_in"], lw["wk_t"], lw["wg"], lw["bg"], nb, transposed_k,
                                                  lw["layer"])
    o_a, s_new = _gla(gq, gk, gv, lg, gs, lw["gla_nw"], s0.reshape(nb, GLA_QK, GLA_DV), nb, t)
    o_b, nbuf = _conv(u, buf, lw["conv_w"], lw["conv_b"], lw["conv_g"], lw["conv_beta"], nb, t)
    return o_a, o_b, dq, dv, kk, s_new.reshape(nb, GLA_HEADS, GLA_DK, GLA_DV), nbuf


def _prep_w_in(w):
    sizes = (GLA_QK, GLA_QK, GLA_VW, GLA_VW, GLA_GATE_RANK, CONV_CH, CONV_CH, DIFF_VW, DIFF_VW, DIFF_VW)
    parts, s = [], 0
    for n in sizes:
        parts.append(w[:, :, s:s + n])
        s += n
    wk_t = jnp.swapaxes(parts[8], 1, 2).astype(BF16)
    parts[4] = jnp.pad(parts[4], ((0, 0), (0, 0), (0, LANES - GLA_GATE_RANK)))
    return jnp.concatenate(parts, axis=2).astype(BF16), wk_t


def _stacked_weights(w_in, gla_w_gate_up, w_out, w_ffn_gate, w_ffn_up, w_ffn_down, w_ple_gate, w_ple_proj):
    w_in_p, wk_t = _prep_w_in(w_in)
    return dict(
        w_in=w_in_p, wk_t=wk_t,
        wg=jnp.pad(gla_w_gate_up, ((0, 0), (0, LANES - GLA_GATE_RANK), (0, 0))).astype(BF16),
        w_out=w_out.astype(BF16), w_ffn_gate=w_ffn_gate.astype(BF16), w_ffn_up=w_ffn_up.astype(BF16),
        w_ffn_down=w_ffn_down.astype(BF16), w_ple_gate=w_ple_gate.astype(BF16), w_ple_proj=w_ple_proj.astype(BF16),
    )


def _layer_vectors(i, gla_b_gate, gla_norm_w, conv_w, conv_b, conv_ln_g, conv_ln_b, diff_lq1, diff_lk1, diff_lq2,
                   diff_lk2, diff_norm_w, ln1_g, ln1_b, ln2_g, ln2_b):
    lam_init = 0.8 - 0.6 * math.exp(-0.3 * i)
    lam = (jnp.exp(jnp.sum(diff_lq1[i] * diff_lk1[i])) - jnp.exp(jnp.sum(diff_lq2[i] * diff_lk2[i]))
           + lam_init).reshape(1).astype(F32)
    return dict(
        layer=i,
        bg=_row(gla_b_gate[i]),
        gla_nw=_row(jnp.tile(gla_norm_w[i], GLA_HEADS)),
        conv_w=conv_w[i], conv_b=_row(conv_b[i]), conv_g=_row(conv_ln_g[i]), conv_beta=_row(conv_ln_b[i]),
        lam=lam, dnw=_row(diff_norm_w[i]),
        ln1_g=_row(ln1_g[i]), ln1_b=_row(ln1_b[i]), ln2_g=_row(ln2_g[i]), ln2_b=_row(ln2_b[i]),
    )


def kernel(x_prompt, x_sample, cache_k, cache_v, state_gla, state_conv, page_table, p_prompt, p_sample, w_in, gla_w_gate_up, gla_b_gate, gla_norm_w, conv_w, conv_b, conv_ln_g, conv_ln_b, diff_lq1, diff_lk1, diff_lq2, diff_lk2, diff_norm_w, rel_bias, w_out, ln1_g, ln1_b, w_ffn_gate, w_ffn_up, w_ffn_down, w_ple_gate, w_ple_proj, ln2_g, ln2_b):
    nbp, s, d = x_prompt.shape
    nbs, ts, _ = x_sample.shape
    depth, n_pool = cache_k.shape[:2]
    assert depth == DEPTH and cache_k.shape[2] == PAGE_SIZE
    cache_kt = jnp.transpose(cache_k, (0, 1, 3, 4, 2)).reshape(depth, n_pool, DIFF_VW, PAGE_SIZE)
    cache_v2 = cache_v.reshape(depth, n_pool, PAGE_SIZE * DIFF_HEADS, 2 * DIFF_DH)
    bias = _bias_tiles(rel_bias, ATTN_TILE)

    pe_prompt = p_prompt.reshape(depth * nbp * s, -1)
    pe_sample = p_sample.reshape(depth * nbs * ts, -1)
    yp = x_prompt.reshape(nbp * s, d)
    ys = x_sample.reshape(nbs * ts, d)
    zero_state = jnp.zeros((nbp, GLA_HEADS, GLA_DK, GLA_DV), F32)
    zero_buf = jnp.zeros((nbp, CONV_WIDTH - 1, CONV_CH), F32)
    outs = [[] for _ in range(6)]
    kts, vrs = [], []
    stacked = _stacked_weights(w_in, gla_w_gate_up, w_out, w_ffn_gate, w_ffn_up, w_ffn_down, w_ple_gate, w_ple_proj)
    for i in range(depth):
        lw = dict(stacked, **_layer_vectors(i, gla_b_gate, gla_norm_w, conv_w, conv_b, conv_ln_g, conv_ln_b, diff_lq1,
                                            diff_lk1, diff_lq2, diff_lk2, diff_norm_w, ln1_g, ln1_b, ln2_g, ln2_b))

        out_scale = 1.0 - (0.8 - 0.6 * math.exp(-0.3 * i))
        pa, pb, q16, v_rows, (kt_i, kt16, v16), s_i, c_i = _token_mixers(yp, nbp, s, lw, zero_state, zero_buf, True)
        kts.append(kt_i)
        vrs.append(v_rows)
        if i < depth - 1:
            pc = _attn_prompt(q16, kt16, v16, bias, lw["lam"], lw["dnw"], nbp, s, out_scale)
        else:
            pc, k_stack, v_stack = _attn_prompt(q16, kt16, v16, bias, lw["lam"], lw["dnw"], nbp, s, out_scale,
                                                kts, vrs)
        outs[0].append(s_i)
        outs[1].append(c_i)
        sa, sb, dq, dv, (dk,), s_i, c_i = _token_mixers(ys, nbs, ts, lw, state_gla[i], state_conv[i], False)
        outs[2].append(dk.reshape(nbs, ts, 2 * DIFF_HEADS, DIFF_DH))
        outs[3].append(dv.reshape(nbs, ts, DIFF_HEADS, 2 * DIFF_DH))
        outs[4].append(s_i)
        outs[5].append(c_i)
        sc = _attn_sample(dq, dk, dv, cache_kt, cache_v2, page_table, i, bias, lw["lam"], lw["dnw"], out_scale)
        yp = _mix_ffn(pa, pb, pc, yp, pe_prompt, i, lw)
        ys = _mix_ffn(sa, sb, sc, ys, pe_sample, i, lw)
    k_prompt = k_stack.reshape(depth, nbp, 2 * DIFF_HEADS, DIFF_DH, s).transpose(0, 1, 4, 2, 3)
    v_prompt = v_stack.reshape(depth, nbp, s, DIFF_HEADS, 2 * DIFF_DH)
    return (yp.reshape(nbp, s, d), ys.reshape(nbs, ts, d), k_prompt, v_prompt) + tuple(jnp.stack(o) for o in outs)
```

```python
import functools
import math

import jax
import jax.numpy as jnp
from jax import lax
from jax.experimental import pallas as pl
from jax.experimental.pallas import tpu as pltpu

F32 = jnp.float32
BF16 = jnp.bfloat16

GLA_HEADS = 4
GLA_DK = 32
GLA_DV = 64
GLA_QK = GLA_HEADS * GLA_DK
GLA_VW = GLA_HEADS * GLA_DV
GLA_GATE_RANK = 16
GLA_TAU = 16.0
CONV_CH = 256
CONV_WIDTH = 31
DIFF_HEADS = 4
DIFF_DH = 64
DIFF_VW = 2 * DIFF_HEADS * DIFF_DH
REL_BUCKETS = 32
REL_MAX_DIST = 128
PAGE_SIZE = 128
DEPTH = 2
ALPHA = (2 * DEPTH) ** 0.25
EPS = 1e-5
LOG2E = math.log2(math.e)

LANES = 128
SUBLANES = 8
PROJ_ROWS = 512
GLA_TILE = 1024
GLA_SUBTILE = 128
GLA_BLOCK = 16
CONV_TILE = 1024
CONV_HALO = 32
SEQS_PER_STEP = 8
SHORT_ROWS = 512
ATTN_TILE = 256
PAGES_PER_STEP = 16
MIX_ROWS = 512
FFN_CHUNK = 256
VMEM_LIMIT = 56 * 1024 * 1024

_IN_OFF = {}
_o = 0
for _name, _n in (("gq", GLA_QK), ("gk", GLA_QK), ("gv", GLA_VW), ("gg", GLA_VW), ("glr", LANES), ("ca", CONV_CH),
                  ("cg", CONV_CH), ("dq", DIFF_VW), ("dk", DIFF_VW), ("dv", DIFF_VW)):
    _IN_OFF[_name] = (_o, _n)
    _o += _n
N_IN_PAD = _o


def _params(*sem):
    return pltpu.CompilerParams(dimension_semantics=sem, vmem_limit_bytes=VMEM_LIMIT)


def _const_spec(shape, layer=None):
    if layer is None:
        nd = len(shape)
        return pl.BlockSpec(shape, lambda *_: (0,) * nd, pipeline_mode=pl.Buffered(1))
    nd = len(shape) - 1
    return pl.BlockSpec((None,) + tuple(shape[1:]), lambda *_: (layer,) + (0,) * nd, pipeline_mode=pl.Buffered(1))


def _nt(a, b):
    return lax.dot_general(a, b, (((1,), (1,)), ((), ())), preferred_element_type=F32)


def _mm(a, b):
    return jnp.dot(a, b, preferred_element_type=F32)


def _mm_split(a, b):
    hi = a.astype(BF16)
    lo = (a - hi.astype(F32)).astype(BF16)
    return _mm(hi, b) + _mm(lo, b)


def _sigmoid(x):
    return 1.0 / (1.0 + jnp.exp(-x))


def _layer_norm(x, g, b):
    mu = jnp.mean(x, axis=-1, keepdims=True)
    xc = x - mu
    var = jnp.mean(xc * xc, axis=-1, keepdims=True)
    return xc * lax.rsqrt(var + EPS) * g + b


def _seqs_per_step(nb, nj, r):
    if nj > 1:
        return 1
    n = math.gcd(nb, SEQS_PER_STEP)
    while n > 1 and n * r > SHORT_ROWS:
        n //= 2
    return n


def _fold_lanes(x, op):
    acc = x[:, 0:LANES]
    for c in range(1, x.shape[1] // LANES):
        acc = op(acc, x[:, c * LANES:(c + 1) * LANES])
    return acc


def _inproj_body(x_ref, w_ref, wkt_ref, wg_ref, bg_ref, gq_ref, gk_ref, gv_ref, gs_ref, lg_ref, u_ref, dq_ref,
                 dv_ref, *k_refs, transposed_k):
    x = x_ref[...].astype(BF16)

    def proj(name):
        lo, n = _IN_OFF[name]
        return _mm(x, w_ref[:, lo:lo + n])

    gq_ref[...] = proj("gq") * GLA_DK ** -0.5
    gk_ref[...] = proj("gk")
    gv_ref[...] = proj("gv")
    gg = proj("gg")
    gs_ref[...] = gg * _sigmoid(gg)
    z = _mm(proj("glr").astype(BF16), wg_ref[...]) + bg_ref[...]
    lg_ref[...] = (jnp.minimum(z, 0.0) - jnp.log(1.0 + jnp.exp(-jnp.abs(z)))) * (1.0 / GLA_TAU)
    u_ref[...] = proj("ca") * _sigmoid(proj("cg"))
    dq_ref[...] = (proj("dq") * (DIFF_DH ** -0.5 * LOG2E)).astype(dq_ref.dtype)
    dv = proj("dv")
    if transposed_k:
        kt_ref, kt16_ref, v16_ref = k_refs
        tm = x.shape[0]
        for h in range(DIFF_HEADS):
            dv_ref[pl.ds(h, tm, stride=DIFF_HEADS), :] = dv[:, h * LANES:(h + 1) * LANES]
        kt = _nt(wkt_ref[...], x)
        kt_ref[0] = kt
        kt16_ref[0] = kt.astype(BF16)
        v16_ref[...] = dv.astype(BF16)
    else:
        dv_ref[...] = dv
        k_refs[0][...] = proj("dk")


def _in_proj(x2, w_in_p, wk_t, wg_p, bg, nb, transposed_k, layer):
    m, d = x2.shape
    t = m // nb
    tm = min(PROJ_ROWS, t if transposed_k else m)
    nj = t // tm
    widths = (GLA_QK, GLA_QK, GLA_VW, GLA_VW, GLA_QK, CONV_CH, DIFF_VW)
    dtypes = [F32] * 6 + [BF16 if transposed_k else F32]
    out_specs = [pl.BlockSpec((tm, n), lambda i: (i, 0)) for n in widths]
    out_shape = [jax.ShapeDtypeStruct((m, n), dt) for n, dt in zip(widths, dtypes)]
    if transposed_k:
        out_specs.append(pl.BlockSpec((tm * DIFF_HEADS, LANES), lambda i: (i, 0)))
        out_shape.append(jax.ShapeDtypeStruct((m * DIFF_HEADS, LANES), F32))
        kt_spec = pl.BlockSpec((1, DIFF_VW, tm), lambda i: (i // nj, 0, i % nj))
        out_specs += [kt_spec, kt_spec, pl.BlockSpec((tm, DIFF_VW), lambda i: (i, 0))]
        out_shape += [jax.ShapeDtypeStruct((nb, DIFF_VW, t), F32), jax.ShapeDtypeStruct((nb, DIFF_VW, t), BF16),
                      jax.ShapeDtypeStruct((m, DIFF_VW), BF16)]
    else:
        out_specs += [pl.BlockSpec((tm, DIFF_VW), lambda i: (i, 0))] * 2
        out_shape += [jax.ShapeDtypeStruct((m, DIFF_VW), F32)] * 2
    return pl.pallas_call(
        functools.partial(_inproj_body, transposed_k=transposed_k),
        grid=(m // tm,),
        in_specs=[pl.BlockSpec((tm, d), lambda i: (i, 0)),
                  _const_spec(w_in_p.shape, layer), _const_spec(wk_t.shape, layer), _const_spec(wg_p.shape, layer),
                  _const_spec(bg.shape)],
        out_specs=out_specs,
        out_shape=out_shape,
        compiler_params=_params("parallel"),
        name="in_proj",
    )(x2, w_in_p, wk_t, wg_p, bg)


def _gla_body(q_ref, k_ref, v_ref, lg_ref, gs_ref, nw_ref, s0_ref, o_ref, sout_ref, s_scr, *, cb, nj):
    nseq = s0_ref.shape[0]
    r = q_ref.shape[0] // nseq
    rs = min(r, GLA_SUBTILE)
    row = lax.broadcasted_iota(jnp.int32, (rs, 1), 0) % cb

    same_head = (lax.broadcasted_iota(jnp.int32, (GLA_QK, GLA_VW), 0) // GLA_DK
                 == lax.broadcasted_iota(jnp.int32, (GLA_QK, GLA_VW), 1) // GLA_DV)
    expand = same_head.astype(BF16)
    mask_t = (lax.broadcasted_iota(jnp.int32, (GLA_VW, GLA_QK), 0) // GLA_DV
              == lax.broadcasted_iota(jnp.int32, (GLA_VW, GLA_QK), 1) // GLA_DK).astype(F32)
    grp = (lax.broadcasted_iota(jnp.int32, (GLA_VW, GLA_VW), 0) // GLA_DV
           == lax.broadcasted_iota(jnp.int32, (GLA_VW, GLA_VW), 1) // GLA_DV)
    head_mean = jnp.where(grp, 1.0 / GLA_DV, 0.0).astype(BF16)

    def run_sequence(first_row, st):
        for sub in range(r // rs):
            rows = pl.ds(first_row + sub * rs, rs)
            q = q_ref[rows, :]
            k = k_ref[rows, :]
            v = v_ref[rows, :]

            b = lg_ref[rows, :]
            s = 1
            while s < cb:
                b = b + jnp.where(row >= s, pltpu.roll(b, s, 0), 0.0)
                s *= 2

            o = _mm((q * k).astype(BF16), expand) * v
            for delta in range(1, cb):
                ks = pltpu.roll(k, delta, 0)
                bs = pltpu.roll(b, delta, 0)
                vs = pltpu.roll(v, delta, 0)
                p = jnp.where(row >= delta, q * ks * jnp.exp(b - bs), 0.0)
                o = o + _mm(p.astype(BF16), expand) * vs

            inter = []
            for t in range(rs // cb):
                sl = slice(t * cb, (t + 1) * cb)
                bt = b[sl]
                bend = bt[cb - 1:cb]
                qe = (q[sl] * jnp.exp(bt)).astype(BF16)
                inter.append(_nt(qe, st.astype(BF16)))
                ke = k[sl] * jnp.exp(bend - bt)
                vt = v[sl]
                if cb < 16:
                    ke = jnp.concatenate([ke, jnp.zeros((16 - cb, GLA_QK), F32)], axis=0)
                    vt = jnp.concatenate([vt, jnp.zeros((16 - cb, GLA_VW), F32)], axis=0)
                kv = lax.dot_general(vt.astype(BF16), ke.astype(BF16), (((0,), (0,)), ((), ())),
                                     preferred_element_type=F32)
                st = st * jnp.exp(bend) + kv * mask_t
            o = o + (inter[0] if len(inter) == 1 else jnp.concatenate(inter, axis=0))

            mean_sq = _mm_split(o * o, head_mean)
            o_ref[rows, :] = o * lax.rsqrt(mean_sq + EPS) * nw_ref[...] * gs_ref[rows, :]
        return st

    def load_state(seq):
        z = s0_ref[seq]
        z = jnp.concatenate([z] * GLA_HEADS, axis=1)
        return jnp.where(same_head, z, 0.0).T

    def store_state(seq, st):
        z = st.T
        out = z[:, 0:GLA_DV]
        for h in range(1, GLA_HEADS):
            out = out + z[:, h * GLA_DV:(h + 1) * GLA_DV]
        sout_ref[seq] = out

    if nj == 1:
        for seq in range(nseq):
            store_state(seq, run_sequence(seq * r, load_state(seq)))
    else:
        j = pl.program_id(1)

        @pl.when(j == 0)
        def _():
            s_scr[...] = load_state(0)

        st = run_sequence(0, s_scr[...])
        s_scr[...] = st

        @pl.when(j == nj - 1)
        def _():
            store_state(0, st)


def _gla(gq, gk, gv, lg, gs, nw, s0, nb, t):
    cb = GLA_BLOCK if t % GLA_BLOCK == 0 else t
    r = min(t, GLA_TILE)
    nj = t // r
    m = nb * t
    nseq = _seqs_per_step(nb, nj, r)

    def rows(n):
        return pl.BlockSpec((nseq * r, n), lambda b, j: (b * nj + j, 0))

    return pl.pallas_call(
        functools.partial(_gla_body, cb=cb, nj=nj),
        grid=(nb // nseq, nj),
        in_specs=[rows(GLA_QK), rows(GLA_QK), rows(GLA_VW), rows(GLA_QK), rows(GLA_VW),
                  _const_spec(nw.shape),
                  pl.BlockSpec((nseq, GLA_QK, GLA_DV), lambda b, j: (b, 0, 0))],
        out_specs=[rows(GLA_VW), pl.BlockSpec((nseq, GLA_QK, GLA_DV), lambda b, j: (b, 0, 0))],
        out_shape=[jax.ShapeDtypeStruct((m, GLA_VW), F32), jax.ShapeDtypeStruct((nb, GLA_QK, GLA_DV), F32)],
        scratch_shapes=[pltpu.VMEM((GLA_VW, GLA_QK), F32)],
        compiler_params=_params("parallel", "arbitrary"),
        name="gla",
    )(gq, gk, gv, lg, gs, nw, s0)


def _conv_body(u_ref, buf_ref, cw_ref, cb_ref, g_ref, beta_ref, y_ref, nbuf_ref, seq_scr, *, nj):
    nseq = buf_ref.shape[0]
    r = u_ref.shape[0] // nseq
    hist = CONV_WIDTH - 1

    def start_window(scr, seq):
        scr[0:CONV_HALO - hist, :] = jnp.zeros((CONV_HALO - hist, CONV_CH), F32)
        scr[CONV_HALO - hist:CONV_HALO, :] = buf_ref[seq]

    def run_tile(scr, first_row):
        scr[CONV_HALO:CONV_HALO + r, :] = u_ref[pl.ds(first_row, r), :]
        window = scr[...]
        rows = window.shape[0]
        acc = jnp.zeros((r, CONV_CH), F32) + cb_ref[...]
        for rho in range(SUBLANES):
            lo = CONV_HALO - hist + rho
            shifted = pltpu.roll(window, rows - lo, 0)
            for k in range(-(-CONV_WIDTH // SUBLANES)):
                w = SUBLANES * k + rho
                if w < CONV_WIDTH:
                    acc = acc + shifted[SUBLANES * k:SUBLANES * k + r, :] * cw_ref[w:w + 1, :]
        y = _layer_norm(acc, g_ref[...], beta_ref[...])
        y_ref[pl.ds(first_row, r), :] = y * _sigmoid(y)

    def last_rows(scr):
        return scr[r + CONV_HALO - hist:r + CONV_HALO, :]

    if nj == 1:
        for seq in range(nseq):
            scr = seq_scr.at[seq]
            start_window(scr, seq)
            run_tile(scr, seq * r)
            nbuf_ref[seq] = last_rows(scr)
    else:
        j = pl.program_id(1)
        scr = seq_scr.at[0]

        @pl.when(j == 0)
        def _():
            start_window(scr, 0)

        @pl.when(j > 0)
        def _():
            scr[0:CONV_HALO, :] = scr[r:r + CONV_HALO, :]

        run_tile(scr, 0)

        @pl.when(j == nj - 1)
        def _():
            nbuf_ref[0] = last_rows(scr)


def _conv(u, buf, cw, cb, g, beta, nb, t):
    r = min(t, CONV_TILE)
    nj = t // r
    assert nj == 1 or r >= CONV_HALO
    hist = CONV_WIDTH - 1
    nseq = _seqs_per_step(nb, nj, r)
    return pl.pallas_call(
        functools.partial(_conv_body, nj=nj),
        grid=(nb // nseq, nj),
        in_specs=[pl.BlockSpec((nseq * r, CONV_CH), lambda b, j: (b * nj + j, 0)),
                  pl.BlockSpec((nseq, hist, CONV_CH), lambda b, j: (b, 0, 0)),
                  _const_spec(cw.shape), _const_spec(cb.shape), _const_spec(g.shape), _const_spec(beta.shape)],
        out_specs=[pl.BlockSpec((nseq * r, CONV_CH), lambda b, j: (b * nj + j, 0)),
                   pl.BlockSpec((nseq, hist, CONV_CH), lambda b, j: (b, 0, 0))],
        out_shape=[jax.ShapeDtypeStruct((nb * t, CONV_CH), F32), jax.ShapeDtypeStruct((nb, hist, CONV_CH), F32)],
        scratch_shapes=[pltpu.VMEM((nseq, r + CONV_HALO, CONV_CH), F32)],
        compiler_params=_params("parallel", "arbitrary"),
        name="conv",
    )(u, buf, cw, cb, g, beta)


def _rel_bucket(dist):
    n = jnp.maximum(dist, 0)
    max_exact = REL_BUCKETS // 2
    nf = jnp.maximum(n, 1).astype(F32)
    large = max_exact + (jnp.log(nf / max_exact) / math.log(REL_MAX_DIST / max_exact)
                         * (REL_BUCKETS - max_exact)).astype(jnp.int32)
    large = jnp.minimum(large, REL_BUCKETS - 1)
    return jnp.where(n < max_exact, n, large)


def _bias_body(rb_ref, idx_ref, o_ref):
    h = pl.program_id(1)
    idx = idx_ref[0]
    acc = jnp.full(idx.shape, -jnp.inf, F32)
    for bucket in range(REL_BUCKETS):
        acc = jnp.where(idx == bucket, rb_ref[bucket, h], acc)
    o_ref[0, 0] = (acc - rb_ref[REL_BUCKETS - 1, h]) * LOG2E


def _bias_tiles(rel_bias, t):
    assert t >= REL_MAX_DIST
    ii = jnp.arange(t, dtype=jnp.int32)[:, None]
    jj = jnp.arange(t, dtype=jnp.int32)[None, :]
    idx = jnp.stack([jnp.where(ii >= jj, _rel_bucket(ii - jj), -1), _rel_bucket(t + ii - jj)])
    nh = rel_bias.shape[1]
    return pl.pallas_call(
        _bias_body,
        grid=(2, nh),
        in_specs=[pl.BlockSpec(memory_space=pltpu.SMEM),
                  pl.BlockSpec((1, t, t), lambda r, h: (r, 0, 0))],
        out_specs=pl.BlockSpec((1, 1, t, t), lambda r, h: (r, h, 0, 0)),
        out_shape=jax.ShapeDtypeStruct((2, nh, t, t), F32),
        compiler_params=_params("arbitrary", "arbitrary"),
        name="rel_bias_tiles",
    )(rel_bias, idx)


def _attn_body(lam_ref, q_ref, kt_ref, v_ref, bias_ref, dnw_ref, *rest, out_scale, nstack):
    stack_in = rest[:2 * nstack]
    o_ref = rest[2 * nstack]
    stack_out = rest[2 * nstack + 1:2 * nstack + 1 + (2 if nstack else 0)]
    q_scr, s_scr, m_scr, l_scr, acc_scr = rest[len(rest) - 5:]
    for layer in range(nstack):
        stack_out[0][layer, 0] = stack_in[layer][0]
        stack_out[1][layer] = stack_in[nstack + layer][...]
    i = pl.program_id(1)
    t = q_ref.shape[0]
    nmaps = 2 * DIFF_HEADS
    lane = lax.broadcasted_iota(jnp.int32, (t, LANES), 1)
    for n in range(nmaps):
        qh = q_ref[:, (n // 2) * LANES:(n // 2 + 1) * LANES]
        keep = (lane < DIFF_DH) if n % 2 == 0 else (lane >= DIFF_DH)
        q_scr[n] = jnp.where(keep, qh, jnp.zeros_like(qh))
    m_scr[...] = jnp.full(m_scr.shape, -jnp.inf, F32)
    l_scr[...] = jnp.zeros(l_scr.shape, F32)
    acc_scr[...] = jnp.zeros(acc_scr.shape, F32)

    def tile(j, which, ntile=1):
        width = ntile * t
        keys = pl.ds(pl.multiple_of(j * t, t), width)
        for n in range(nmaps):
            s = _mm(q_scr[n], kt_ref[0, (n // 2) * LANES:(n // 2 + 1) * LANES, keys])
            if which == "near":
                s = s + jnp.concatenate([bias_ref[1, n], bias_ref[0, n]], axis=1)
            elif which is not None:
                s = s + bias_ref[which, n]
            s_scr[n, :, 0:width] = s
        for n in range(nmaps):
            s = s_scr[n, :, 0:width]
            m_old = m_scr[n]
            row_max = jnp.max(_fold_lanes(s, jnp.maximum), axis=-1, keepdims=True)
            m_new = jnp.maximum(m_old, jnp.broadcast_to(row_max, (t, LANES)))
            alpha = jnp.exp2(m_old - m_new)
            ps = [jnp.exp2(s[:, c * LANES:(c + 1) * LANES] - m_new) for c in range(width // LANES)]
            part = ps[0]
            for pc in ps[1:]:
                part = part + pc
            l_scr[n] = alpha * l_scr[n] + part
            pv = _mm(jnp.concatenate(ps, axis=1).astype(BF16), v_ref[keys, (n // 2) * LANES:(n // 2 + 1) * LANES])
            acc_scr[n] = alpha * acc_scr[n] + pv
            m_scr[n] = m_new

    nfar = jnp.maximum(i - 1, 0)

    def far_pair(c, carry):
        tile(2 * c, None, 2)
        return carry

    lax.fori_loop(0, nfar // 2, far_pair, 0)

    @pl.when(nfar % 2 == 1)
    def _():
        tile(nfar - 1, None)

    @pl.when(i >= 1)
    def _():
        tile(i - 1, "near", 2)

    @pl.when(i == 0)
    def _():
        tile(0, 0)

    lam = lam_ref[0]
    outs = []
    for hp in range(DIFF_HEADS):
        l0 = jnp.sum(l_scr[2 * hp], axis=-1, keepdims=True)
        l1 = jnp.sum(l_scr[2 * hp + 1], axis=-1, keepdims=True)
        o = acc_scr[2 * hp] / l0 - lam * (acc_scr[2 * hp + 1] / l1)
        ms_o = jnp.mean(o * o, axis=-1, keepdims=True)
        outs.append(o * lax.rsqrt(ms_o + EPS) * dnw_ref[...] * out_scale)
    o_ref[...] = jnp.concatenate(outs, axis=1)


def _attn_prompt(q16, kt16, v16, bias, lam, dnw, nb, s, out_scale, stack_kt=(), stack_v=()):
    t = bias.shape[-1]
    nq = s // t
    nmaps = 2 * DIFF_HEADS
    nstack = len(stack_kt)
    assert len(stack_v) == nstack
    in_specs = [pl.BlockSpec(memory_space=pltpu.SMEM),
                pl.BlockSpec((t, DIFF_VW), lambda b, i: (b * nq + i, 0)),
                pl.BlockSpec((1, DIFF_VW, s), lambda b, i: (b, 0, 0)),
                pl.BlockSpec((s, DIFF_VW), lambda b, i: (b, 0)),
                _const_spec(bias.shape), _const_spec(dnw.shape)]
    in_specs += [pl.BlockSpec((1, DIFF_VW, t), lambda b, i: (b, 0, i))] * nstack
    in_specs += [pl.BlockSpec((t * DIFF_HEADS, LANES), lambda b, i: (b * nq + i, 0))] * nstack
    out_specs = [pl.BlockSpec((t, DIFF_VW), lambda b, i: (b * nq + i, 0))]
    out_shape = [jax.ShapeDtypeStruct((nb * s, DIFF_VW), F32)]
    if nstack:
        out_specs += [pl.BlockSpec((nstack, 1, DIFF_VW, t), lambda b, i: (0, b, 0, i)),
                      pl.BlockSpec((nstack, t * DIFF_HEADS, LANES), lambda b, i: (0, b * nq + i, 0))]
        out_shape += [jax.ShapeDtypeStruct((nstack, nb, DIFF_VW, s), F32),
                      jax.ShapeDtypeStruct((nstack, nb * s * DIFF_HEADS, LANES), F32)]
    out = pl.pallas_call(
        functools.partial(_attn_body, out_scale=out_scale, nstack=nstack),
        grid=(nb, nq),
        in_specs=in_specs,
        out_specs=out_specs,
        out_shape=out_shape,
        scratch_shapes=[pltpu.VMEM((nmaps, t, LANES), BF16), pltpu.VMEM((nmaps, t, 2 * t), F32),
                        pltpu.VMEM((nmaps, t, LANES), F32), pltpu.VMEM((nmaps, t, LANES), F32),
                        pltpu.VMEM((nmaps, t, LANES), F32)],
        compiler_params=_params("parallel", "arbitrary"),
        name="diff_attn_prompt",
    )(lam, q16, kt16, v16, bias, dnw, *stack_kt, *stack_v)
    return out if nstack else out[0]


def _sattn_phases(step, nsteps, lam_ref, q_ref, kn_ref, vn_ref, d1_ref, d0_ref, dnw_ref, kp, vp, o_ref, m_scr, l_scr,
                  acc_scr, out_scale):
    npg = len(kp)
    nh = 2 * DIFF_HEADS
    tq = q_ref.shape[0]
    rows_h = 2 * tq

    q = q_ref[...]
    lane_head = lax.broadcasted_iota(jnp.int32, q.shape, 1) // DIFF_DH
    qs = jnp.concatenate([jnp.where(lane_head == hh, q, 0.0) for hh in range(nh)], axis=0).astype(BF16)

    if step == 0:
        m_scr[...] = jnp.full(m_scr.shape, -jnp.inf, F32)
        l_scr[...] = jnp.zeros(l_scr.shape, F32)
        acc_scr[...] = jnp.zeros(acc_scr.shape, F32)

    def update(s, values_of):
        m_old = m_scr[...]
        m_new = jnp.maximum(m_old, jnp.max(s, axis=-1, keepdims=True))
        alpha = jnp.exp2(m_old - m_new)
        p = jnp.exp2(s - m_new)
        l_scr[...] = alpha * l_scr[...] + jnp.sum(p, axis=-1, keepdims=True)
        pvs = [_mm(p[h * rows_h:(h + 1) * rows_h].astype(BF16), values_of(h)) for h in range(DIFF_HEADS)]
        acc_scr[...] = alpha * acc_scr[...] + jnp.concatenate(pvs, axis=0)
        m_scr[...] = m_new

    is_last = step == nsteps - 1

    def logits():
        kt = jnp.concatenate([kp[g][...].astype(BF16) for g in range(npg)], axis=1)
        s = _mm(qs, kt)
        if is_last:
            zeros = jnp.zeros((s.shape[0], s.shape[1] - PAGE_SIZE), F32)
            s = s + jnp.concatenate([zeros, d1_ref[...]], axis=1)
        return s

    def finish(s):
        update(s, lambda h: jnp.concatenate(
            [vp[g][pl.ds(h, PAGE_SIZE, stride=DIFF_HEADS), :].astype(BF16) for g in range(npg)], axis=0))
        if not is_last:
            return
        pad = jnp.zeros((PAGE_SIZE - tq, DIFF_VW), F32)
        kn = jnp.concatenate([kn_ref[...], pad], axis=0).astype(BF16)
        vn = jnp.concatenate([vn_ref[...], pad], axis=0).astype(BF16)
        update(_nt(qs, kn) + d0_ref[...], lambda h: vn[:, h * LANES:(h + 1) * LANES])
        o = acc_scr[...] / l_scr[...]
        lam = lam_ref[0]
        outs = []
        for h in range(DIFF_HEADS):
            oh = o[h * rows_h:h * rows_h + tq] - lam * o[h * rows_h + tq:(h + 1) * rows_h]
            ms = jnp.mean(oh * oh, axis=-1, keepdims=True)
            outs.append(oh * lax.rsqrt(ms + EPS) * dnw_ref[...] * out_scale)
        o_ref[...] = jnp.concatenate(outs, axis=1)

    return logits, finish


_MIX_CONSTS = ("w_out", "ln1_g", "ln1_b", "w_ffn_gate", "w_ffn_up", "w_ffn_down", "w_ple_gate", "w_ple_proj",
               "ln2_g", "ln2_b")


def _mix_head(oa_ref, ob_ref, oc_ref, x_ref, pe_ref, wo_ref, g1_ref, b1_ref, wpg_ref, wpp_ref):
    mix = _mm(oa_ref[...].astype(BF16), wo_ref[0:GLA_VW, :])
    mix = mix + _mm(ob_ref[...].astype(BF16), wo_ref[GLA_VW:GLA_VW + CONV_CH, :])
    mix = mix + _mm(oc_ref[...].astype(BF16), wo_ref[GLA_VW + CONV_CH:, :])
    x = _layer_norm(ALPHA * x_ref[...] + mix, g1_ref[...], b1_ref[...])
    xb = x.astype(BF16)
    ple = _sigmoid(_mm(xb, wpg_ref[...])) * _mm(pe_ref[...].astype(BF16), wpp_ref[...])
    return xb, ALPHA * x + ple


def _ffn_cols(xb, wg_ref, wu_ref, wd_ref, lo, hi):
    acc = None
    c = lo
    while c < hi:
        w = min(FFN_CHUNK, hi - c)
        gate = _mm(xb, wg_ref[:, c:c + w])
        hid = gate * _sigmoid(gate) * _mm(xb, wu_ref[:, c:c + w])
        part = _mm(hid.astype(BF16), wd_ref[c:c + w, :])
        acc = part if acc is None else acc + part
        c += w
    return acc


def _mix_ffn_body(oa_ref, ob_ref, oc_ref, x_ref, pe_ref, wo_ref, g1_ref, b1_ref, wg_ref, wu_ref, wd_ref, wpg_ref,
                  wpp_ref, g2_ref, b2_ref, y_ref):
    xb, base = _mix_head(oa_ref, ob_ref, oc_ref, x_ref, pe_ref, wo_ref, g1_ref, b1_ref, wpg_ref, wpp_ref)
    acc = base + _ffn_cols(xb, wg_ref, wu_ref, wd_ref, 0, wg_ref.shape[1])
    y_ref[...] = _layer_norm(acc, g2_ref[...], b2_ref[...])


def _mix_ffn(oa, ob, oc, x2, pe_all, layer, lw):
    m, d = x2.shape
    tm = min(MIX_ROWS, m)
    nt = m // tm
    consts = [lw[n] for n in _MIX_CONSTS]

    def rows(n):
        return pl.BlockSpec((tm, n), lambda i: (i, 0))

    return pl.pallas_call(
        _mix_ffn_body,
        grid=(nt,),
        in_specs=[rows(GLA_VW), rows(CONV_CH), rows(DIFF_VW), rows(d),
                  pl.BlockSpec((tm, pe_all.shape[1]), lambda i: (layer * nt + i, 0))]
                 + [_const_spec(a.shape, layer if a.ndim == 3 else None) for a in consts],
        out_specs=rows(d),
        out_shape=jax.ShapeDtypeStruct((m, d), F32),
        compiler_params=_params("parallel"),
        name="mix_ffn",
    )(oa, ob, oc, x2, pe_all, *consts)


def _sattn_body(pt_ref, lam_ref, q_ref, kn_ref, vn_ref, d1_ref, d0_ref, dnw_ref, *rest, npg, nsteps, out_scale):
    del pt_ref
    kp = rest[:npg]
    vp = rest[npg:2 * npg]
    o_ref = rest[2 * npg]
    m_scr, l_scr, acc_scr = rest[2 * npg + 1:]
    p_id = pl.program_id(1)
    for step in range(nsteps):
        @pl.when(p_id == step)
        def _(step=step):
            logits, finish = _sattn_phases(step, nsteps, lam_ref, q_ref, kn_ref, vn_ref, d1_ref, d0_ref, dnw_ref,
                                           kp, vp, o_ref, m_scr, l_scr, acc_scr, out_scale)
            finish(logits())


def _attn_sample(dq, dk, dv, cache_kt, cache_v2, page_table, layer, bias, lam, dnw, out_scale):
    nb, n_pages = page_table.shape
    tq = dq.shape[0] // nb
    npg = PAGES_PER_STEP
    nsteps = n_pages // npg
    assert n_pages % npg == 0 and cache_kt.shape[-1] == PAGE_SIZE and tq == SUBLANES
    nh = 2 * DIFF_HEADS
    t = bias.shape[-1]
    d1 = bias[1, :, 0:tq, t - PAGE_SIZE:t].reshape(nh * tq, PAGE_SIZE)
    d0 = jnp.concatenate([bias[0, :, 0:tq, 0:tq], jnp.full((nh, tq, PAGE_SIZE - tq), -jnp.inf, F32)],
                         axis=-1).reshape(nh * tq, PAGE_SIZE)

    def page_spec(g):
        return pl.BlockSpec((None, None, DIFF_VW, PAGE_SIZE),
                            lambda b, p, pt: (layer, pt[b * n_pages + p * npg + g], 0, 0))

    def rows_spec():
        return pl.BlockSpec((tq, DIFF_VW), lambda b, p, pt: (b, 0))

    def full_spec(a):
        nd = a.ndim
        return pl.BlockSpec(a.shape, lambda b, p, pt: (0,) * nd)

    grid_spec = pltpu.PrefetchScalarGridSpec(
        num_scalar_prefetch=1,
        grid=(nb, nsteps),
        in_specs=[pl.BlockSpec(memory_space=pltpu.SMEM), rows_spec(), rows_spec(), rows_spec(),
                  full_spec(d1), full_spec(d0), full_spec(dnw)]
                 + [page_spec(g) for g in range(npg)] + [page_spec(g) for g in range(npg)],
        out_specs=rows_spec(),
        scratch_shapes=[pltpu.VMEM((nh * tq, 1), F32), pltpu.VMEM((nh * tq, 1), F32),
                        pltpu.VMEM((nh * tq, LANES), F32)],
    )
    return pl.pallas_call(
        functools.partial(_sattn_body, npg=npg, nsteps=nsteps, out_scale=out_scale),
        grid_spec=grid_spec,
        out_shape=jax.ShapeDtypeStruct((nb * tq, DIFF_VW), F32),
        compiler_params=_params("parallel", "arbitrary"),
        name="diff_attn_sample",
    )(page_table.reshape(-1), lam, dq, dk, dv, d1, d0, dnw, *([cache_kt] * npg), *([cache_v2] * npg))


def _row(a):
    return a.reshape(1, -1)


def _token_mixers(x2, nb, t, lw, s0, buf, transposed_k):
    gq, gk, gv, gs, lg, u, dq, dv, *kk = _in_proj(x2, lw["w_in"], lw["wk_t"], lw["wg"], lw["bg"], nb, transposed_k,
                                                  lw["layer"])
    o_a, s_new = _gla(gq, gk, gv, lg, gs, lw["gla_nw"], s0.reshape(nb, GLA_QK, GLA_DV), nb, t)
    o_b, nbuf = _conv(u, buf, lw["conv_w"], lw["conv_b"], lw["conv_g"], lw["conv_beta"], nb, t)
    return o_a, o_b, dq, dv, kk, s_new.reshape(nb, GLA_HEADS, GLA_DK, GLA_DV), nbuf


def _prep_w_in(w):
    sizes = (GLA_QK, GLA_QK, GLA_VW, GLA_VW, GLA_GATE_RANK, CONV_CH, CONV_CH, DIFF_VW, DIFF_VW, DIFF_VW)
    parts, s = [], 0
    for n in sizes:
        parts.append(w[:, :, s:s + n])
        s += n
    wk_t = jnp.swapaxes(parts[8], 1, 2).astype(BF16)
    parts[4] = jnp.pad(parts[4], ((0, 0), (0, 0), (0, LANES - GLA_GATE_RANK)))
    return jnp.concatenate(parts, axis=2).astype(BF16), wk_t


def _stacked_weights(w_in, gla_w_gate_up, w_out, w_ffn_gate, w_ffn_up, w_ffn_down, w_ple_gate, w_ple_proj):
    w_in_p, wk_t = _prep_w_in(w_in)
    return dict(
        w_in=w_in_p, wk_t=wk_t,
        wg=jnp.pad(gla_w_gate_up, ((0, 0), (0, LANES - GLA_GATE_RANK), (0, 0))).astype(BF16),
        w_out=w_out.astype(BF16), w_ffn_gate=w_ffn_gate.astype(BF16), w_ffn_up=w_ffn_up.astype(BF16),
        w_ffn_down=w_ffn_down.astype(BF16), w_ple_gate=w_ple_gate.astype(BF16), w_ple_proj=w_ple_proj.astype(BF16),
    )


def _layer_vectors(i, gla_b_gate, gla_norm_w, conv_w, conv_b, conv_ln_g, conv_ln_b, diff_lq1, diff_lk1, diff_lq2,
                   diff_lk2, diff_norm_w, ln1_g, ln1_b, ln2_g, ln2_b):
    lam_init = 0.8 - 0.6 * math.exp(-0.3 * i)
    lam = (jnp.exp(jnp.sum(diff_lq1[i] * diff_lk1[i])) - jnp.exp(jnp.sum(diff_lq2[i] * diff_lk2[i]))
           + lam_init).reshape(1).astype(F32)
    return dict(
        layer=i,
        bg=_row(gla_b_gate[i]),
        gla_nw=_row(jnp.tile(gla_norm_w[i], GLA_HEADS)),
        conv_w=conv_w[i], conv_b=_row(conv_b[i]), conv_g=_row(conv_ln_g[i]), conv_beta=_row(conv_ln_b[i]),
        lam=lam, dnw=_row(diff_norm_w[i]),
        ln1_g=_row(ln1_g[i]), ln1_b=_row(ln1_b[i]), ln2_g=_row(ln2_g[i]), ln2_b=_row(ln2_b[i]),
    )


def kernel(x_prompt, x_sample, cache_k, cache_v, state_gla, state_conv, page_table, p_prompt, p_sample, w_in, gla_w_gate_up, gla_b_gate, gla_norm_w, conv_w, conv_b, conv_ln_g, conv_ln_b, diff_lq1, diff_lk1, diff_lq2, diff_lk2, diff_norm_w, rel_bias, w_out, ln1_g, ln1_b, w_ffn_gate, w_ffn_up, w_ffn_down, w_ple_gate, w_ple_proj, ln2_g, ln2_b):
    nbp, s, d = x_prompt.shape
    nbs, ts, _ = x_sample.shape
    depth, n_pool = cache_k.shape[:2]
    assert depth == DEPTH and cache_k.shape[2] == PAGE_SIZE
    cache_kt = jnp.transpose(cache_k, (0, 1, 3, 4, 2)).reshape(depth, n_pool, DIFF_VW, PAGE_SIZE)
    cache_v2 = cache_v.reshape(depth, n_pool, PAGE_SIZE * DIFF_HEADS, 2 * DIFF_DH)
    bias = _bias_tiles(rel_bias, ATTN_TILE)

    pe_prompt = p_prompt.reshape(depth * nbp * s, -1)
    pe_sample = p_sample.reshape(depth * nbs * ts, -1)
    yp = x_prompt.reshape(nbp * s, d)
    ys = x_sample.reshape(nbs * ts, d)
    zero_state = jnp.zeros((nbp, GLA_HEADS, GLA_DK, GLA_DV), F32)
    zero_buf = jnp.zeros((nbp, CONV_WIDTH - 1, CONV_CH), F32)
    outs = [[] for _ in range(6)]
    kts, vrs = [], []
    stacked = _stacked_weights(w_in, gla_w_gate_up, w_out, w_ffn_gate, w_ffn_up, w_ffn_down, w_ple_gate, w_ple_proj)
    for i in range(depth):
        lw = dict(stacked, **_layer_vectors(i, gla_b_gate, gla_norm_w, conv_w, conv_b, conv_ln_g, conv_ln_b, diff_lq1,
                                            diff_lk1, diff_lq2, diff_lk2, diff_norm_w, ln1_g, ln1_b, ln2_g, ln2_b))

        out_scale = 1.0 - (0.8 - 0.6 * math.exp(-0.3 * i))
        pa, pb, q16, v_rows, (kt_i, kt16, v16), s_i, c_i = _token_mixers(yp, nbp, s, lw, zero_state, zero_buf, True)
        kts.append(kt_i)
        vrs.append(v_rows)
        if i < depth - 1:
            pc = _attn_prompt(q16, kt16, v16, bias, lw["lam"], lw["dnw"], nbp, s, out_scale)
        else:
            pc, k_stack, v_stack = _attn_prompt(q16, kt16, v16, bias, lw["lam"], lw["dnw"], nbp, s, out_scale,
                                                kts, vrs)
        outs[0].append(s_i)
        outs[1].append(c_i)
        sa, sb, dq, dv, (dk,), s_i, c_i = _token_mixers(ys, nbs, ts, lw, state_gla[i], state_conv[i], False)
        outs[2].append(dk.reshape(nbs, ts, 2 * DIFF_HEADS, DIFF_DH))
        outs[3].append(dv.reshape(nbs, ts, DIFF_HEADS, 2 * DIFF_DH))
        outs[4].append(s_i)
        outs[5].append(c_i)
        sc = _attn_sample(dq, dk, dv, cache_kt, cache_v2, page_table, i, bias, lw["lam"], lw["dnw"], out_scale)
        yp = _mix_ffn(pa, pb, pc, yp, pe_prompt, i, lw)
        ys = _mix_ffn(sa, sb, sc, ys, pe_sample, i, lw)
    k_prompt = k_stack.reshape(depth, nbp, 2 * DIFF_HEADS, DIFF_DH, s).transpose(0, 1, 4, 2, 3)
    v_prompt = v_stack.reshape(depth, nbp, s, DIFF_HEADS, 2 * DIFF_DH)
    return (yp.reshape(nbp, s, d), ys.reshape(nbs, ts, d), k_prompt, v_prompt) + tuple(jnp.stack(o) for o in outs)
```

```python
import functools
import math

import jax
import jax.numpy as jnp
from jax import lax
from jax.experimental import pallas as pl
from jax.experimental.pallas import tpu as pltpu

F32 = jnp.float32
BF16 = jnp.bfloat16

GLA_HEADS = 4
GLA_DK = 32
GLA_DV = 64
GLA_QK = GLA_HEADS * GLA_DK
GLA_VW = GLA_HEADS * GLA_DV
GLA_GATE_RANK = 16
GLA_TAU = 16.0
CONV_CH = 256
CONV_WIDTH = 31
DIFF_HEADS = 4
DIFF_DH = 64
DIFF_VW = 2 * DIFF_HEADS * DIFF_DH
REL_BUCKETS = 32
REL_MAX_DIST = 128
PAGE_SIZE = 128
DEPTH = 2
ALPHA = (2 * DEPTH) ** 0.25
EPS = 1e-5
LOG2E = math.log2(math.e)

LANES = 128
SUBLANES = 8
PROJ_ROWS = 512
GLA_TILE = 2048
GLA_SUBTILE = 128
GLA_BLOCK = 16
CONV_TILE = 1024
CONV_HALO = 32
SEQS_PER_STEP = 8
SHORT_ROWS = 512
ATTN_TILE = 256
PAGES_PER_STEP = 16
MIX_ROWS = 512
FFN_CHUNK = 256
VMEM_LIMIT = 56 * 1024 * 1024

_IN_OFF = {}
_o = 0
for _name, _n in (("gq", GLA_QK), ("gk", GLA_QK), ("gv", GLA_VW), ("gg", GLA_VW), ("glr", LANES), ("ca", CONV_CH),
                  ("cg", CONV_CH), ("dq", DIFF_VW), ("dk", DIFF_VW), ("dv", DIFF_VW)):
    _IN_OFF[_name] = (_o, _n)
    _o += _n
N_IN_PAD = _o


def _params(*sem):
    return pltpu.CompilerParams(dimension_semantics=sem, vmem_limit_bytes=VMEM_LIMIT)


def _const_spec(shape, layer=None):
    if layer is None:
        nd = len(shape)
        return pl.BlockSpec(shape, lambda *_: (0,) * nd, pipeline_mode=pl.Buffered(1))
    nd = len(shape) - 1
    return pl.BlockSpec((None,) + tuple(shape[1:]), lambda *_: (layer,) + (0,) * nd, pipeline_mode=pl.Buffered(1))


def _nt(a, b):
    return lax.dot_general(a, b, (((1,), (1,)), ((), ())), preferred_element_type=F32)


def _mm(a, b):
    return jnp.dot(a, b, preferred_element_type=F32)


def _mm_split(a, b):
    hi = a.astype(BF16)
    lo = (a - hi.astype(F32)).astype(BF16)
    return _mm(hi, b) + _mm(lo, b)


def _sigmoid(x):
    return 1.0 / (1.0 + jnp.exp(-x))


def _layer_norm(x, g, b):
    mu = jnp.mean(x, axis=-1, keepdims=True)
    xc = x - mu
    var = jnp.mean(xc * xc, axis=-1, keepdims=True)
    return xc * lax.rsqrt(var + EPS) * g + b


def _seqs_per_step(nb, nj, r):
    if nj > 1:
        return 1
    n = math.gcd(nb, SEQS_PER_STEP)
    while n > 1 and n * r > SHORT_ROWS:
        n //= 2
    return n


def _fold_lanes(x, op):
    acc = x[:, 0:LANES]
    for c in range(1, x.shape[1] // LANES):
        acc = op(acc, x[:, c * LANES:(c + 1) * LANES])
    return acc


def _inproj_body(x_ref, w_ref, wkt_ref, wg_ref, bg_ref, gq_ref, gk_ref, gv_ref, gs_ref, lg_ref, u_ref, dq_ref,
                 dv_ref, *k_refs, transposed_k):
    x = x_ref[...].astype(BF16)

    def proj(name):
        lo, n = _IN_OFF[name]
        return _mm(x, w_ref[:, lo:lo + n])

    gq_ref[...] = proj("gq") * GLA_DK ** -0.5
    gk_ref[...] = proj("gk")
    gv_ref[...] = proj("gv")
    gg = proj("gg")
    gs_ref[...] = gg * _sigmoid(gg)
    z = _mm(proj("glr").astype(BF16), wg_ref[...]) + bg_ref[...]
    lg_ref[...] = (jnp.minimum(z, 0.0) - jnp.log(1.0 + jnp.exp(-jnp.abs(z)))) * (1.0 / GLA_TAU)
    u_ref[...] = proj("ca") * _sigmoid(proj("cg"))
    dq_ref[...] = (proj("dq") * (DIFF_DH ** -0.5 * LOG2E)).astype(dq_ref.dtype)
    dv = proj("dv")
    if transposed_k:
        kt_ref, kt16_ref, v16_ref = k_refs
        tm = x.shape[0]
        for h in range(DIFF_HEADS):
            dv_ref[pl.ds(h, tm, stride=DIFF_HEADS), :] = dv[:, h * LANES:(h + 1) * LANES]
        kt = _nt(wkt_ref[...], x)
        kt_ref[0] = kt
        kt16_ref[0] = kt.astype(BF16)
        v16_ref[...] = dv.astype(BF16)
    else:
        dv_ref[...] = dv
        k_refs[0][...] = proj("dk")


def _in_proj(x2, w_in_p, wk_t, wg_p, bg, nb, transposed_k, layer):
    m, d = x2.shape
    t = m // nb
    tm = min(PROJ_ROWS, t if transposed_k else m)
    nj = t // tm
    widths = (GLA_QK, GLA_QK, GLA_VW, GLA_VW, GLA_QK, CONV_CH, DIFF_VW)
    dtypes = [F32] * 6 + [BF16 if transposed_k else F32]
    out_specs = [pl.BlockSpec((tm, n), lambda i: (i, 0)) for n in widths]
    out_shape = [jax.ShapeDtypeStruct((m, n), dt) for n, dt in zip(widths, dtypes)]
    if transposed_k:
        out_specs.append(pl.BlockSpec((tm * DIFF_HEADS, LANES), lambda i: (i, 0)))
        out_shape.append(jax.ShapeDtypeStruct((m * DIFF_HEADS, LANES), F32))
        kt_spec = pl.BlockSpec((1, DIFF_VW, tm), lambda i: (i // nj, 0, i % nj))
        out_specs += [kt_spec, kt_spec, pl.BlockSpec((tm, DIFF_VW), lambda i: (i, 0))]
        out_shape += [jax.ShapeDtypeStruct((nb, DIFF_VW, t), F32), jax.ShapeDtypeStruct((nb, DIFF_VW, t), BF16),
                      jax.ShapeDtypeStruct((m, DIFF_VW), BF16)]
    else:
        out_specs += [pl.BlockSpec((tm, DIFF_VW), lambda i: (i, 0))] * 2
        out_shape += [jax.ShapeDtypeStruct((m, DIFF_VW), F32)] * 2
    return pl.pallas_call(
        functools.partial(_inproj_body, transposed_k=transposed_k),
        grid=(m // tm,),
        in_specs=[pl.BlockSpec((tm, d), lambda i: (i, 0)),
                  _const_spec(w_in_p.shape, layer), _const_spec(wk_t.shape, layer), _const_spec(wg_p.shape, layer),
                  _const_spec(bg.shape)],
        out_specs=out_specs,
        out_shape=out_shape,
        compiler_params=_params("parallel"),
        name="in_proj",
    )(x2, w_in_p, wk_t, wg_p, bg)


def _gla_body(q_ref, k_ref, v_ref, lg_ref, gs_ref, nw_ref, s0_ref, o_ref, sout_ref, s_scr, *, cb, nj):
    nseq = s0_ref.shape[0]
    r = q_ref.shape[0] // nseq
    rs = min(r, GLA_SUBTILE)
    row = lax.broadcasted_iota(jnp.int32, (rs, 1), 0) % cb

    same_head = (lax.broadcasted_iota(jnp.int32, (GLA_QK, GLA_VW), 0) // GLA_DK
                 == lax.broadcasted_iota(jnp.int32, (GLA_QK, GLA_VW), 1) // GLA_DV)
    expand = same_head.astype(BF16)
    mask_t = (lax.broadcasted_iota(jnp.int32, (GLA_VW, GLA_QK), 0) // GLA_DV
              == lax.broadcasted_iota(jnp.int32, (GLA_VW, GLA_QK), 1) // GLA_DK).astype(F32)
    grp = (lax.broadcasted_iota(jnp.int32, (GLA_VW, GLA_VW), 0) // GLA_DV
           == lax.broadcasted_iota(jnp.int32, (GLA_VW, GLA_VW), 1) // GLA_DV)
    head_mean = jnp.where(grp, 1.0 / GLA_DV, 0.0).astype(BF16)

    def run_sequence(first_row, st):
        for sub in range(r // rs):
            rows = pl.ds(first_row + sub * rs, rs)
            q = q_ref[rows, :]
            k = k_ref[rows, :]
            v = v_ref[rows, :]

            b = lg_ref[rows, :]
            s = 1
            while s < cb:
                b = b + jnp.where(row >= s, pltpu.roll(b, s, 0), 0.0)
                s *= 2

            o = _mm((q * k).astype(BF16), expand) * v
            for delta in range(1, cb):
                ks = pltpu.roll(k, delta, 0)
                bs = pltpu.roll(b, delta, 0)
                vs = pltpu.roll(v, delta, 0)
                p = jnp.where(row >= delta, q * ks * jnp.exp(b - bs), 0.0)
                o = o + _mm(p.astype(BF16), expand) * vs

            inter = []
            for t in range(rs // cb):
                sl = slice(t * cb, (t + 1) * cb)
                bt = b[sl]
                bend = bt[cb - 1:cb]
                qe = (q[sl] * jnp.exp(bt)).astype(BF16)
                inter.append(_nt(qe, st.astype(BF16)))
                ke = k[sl] * jnp.exp(bend - bt)
                vt = v[sl]
                if cb < 16:
                    ke = jnp.concatenate([ke, jnp.zeros((16 - cb, GLA_QK), F32)], axis=0)
                    vt = jnp.concatenate([vt, jnp.zeros((16 - cb, GLA_VW), F32)], axis=0)
                kv = lax.dot_general(vt.astype(BF16), ke.astype(BF16), (((0,), (0,)), ((), ())),
                                     preferred_element_type=F32)
                st = st * jnp.exp(bend) + kv * mask_t
            o = o + (inter[0] if len(inter) == 1 else jnp.concatenate(inter, axis=0))

            mean_sq = _mm_split(o * o, head_mean)
            o_ref[rows, :] = o * lax.rsqrt(mean_sq + EPS) * nw_ref[...] * gs_ref[rows, :]
        return st

    def load_state(seq):
        z = s0_ref[seq]
        z = jnp.concatenate([z] * GLA_HEADS, axis=1)
        return jnp.where(same_head, z, 0.0).T

    def store_state(seq, st):
        z = st.T
        out = z[:, 0:GLA_DV]
        for h in range(1, GLA_HEADS):
            out = out + z[:, h * GLA_DV:(h + 1) * GLA_DV]
        sout_ref[seq] = out

    if nj == 1:
        for seq in range(nseq):
            store_state(seq, run_sequence(seq * r, load_state(seq)))
    else:
        j = pl.program_id(1)

        @pl.when(j == 0)
        def _():
            s_scr[...] = load_state(0)

        st = run_sequence(0, s_scr[...])
        s_scr[...] = st

        @pl.when(j == nj - 1)
        def _():
            store_state(0, st)


def _gla(gq, gk, gv, lg, gs, nw, s0, nb, t):
    cb = GLA_BLOCK if t % GLA_BLOCK == 0 else t
    r = min(t, GLA_TILE)
    nj = t // r
    m = nb * t
    nseq = _seqs_per_step(nb, nj, r)

    def rows(n):
        return pl.BlockSpec((nseq * r, n), lambda b, j: (b * nj + j, 0))

    return pl.pallas_call(
        functools.partial(_gla_body, cb=cb, nj=nj),
        grid=(nb // nseq, nj),
        in_specs=[rows(GLA_QK), rows(GLA_QK), rows(GLA_VW), rows(GLA_QK), rows(GLA_VW),
                  _const_spec(nw.shape),
                  pl.BlockSpec((nseq, GLA_QK, GLA_DV), lambda b, j: (b, 0, 0))],
        out_specs=[rows(GLA_VW), pl.BlockSpec((nseq, GLA_QK, GLA_DV), lambda b, j: (b, 0, 0))],
        out_shape=[jax.ShapeDtypeStruct((m, GLA_VW), F32), jax.ShapeDtypeStruct((nb, GLA_QK, GLA_DV), F32)],
        scratch_shapes=[pltpu.VMEM((GLA_VW, GLA_QK), F32)],
        compiler_params=_params("parallel", "arbitrary"),
        name="gla",
    )(gq, gk, gv, lg, gs, nw, s0)


def _conv_body(u_ref, buf_ref, cw_ref, cb_ref, g_ref, beta_ref, y_ref, nbuf_ref, seq_scr, *, nj):
    nseq = buf_ref.shape[0]
    r = u_ref.shape[0] // nseq
    hist = CONV_WIDTH - 1

    def start_window(scr, seq):
        scr[0:CONV_HALO - hist, :] = jnp.zeros((CONV_HALO - hist, CONV_CH), F32)
        scr[CONV_HALO - hist:CONV_HALO, :] = buf_ref[seq]

    def run_tile(scr, first_row):
        scr[CONV_HALO:CONV_HALO + r, :] = u_ref[pl.ds(first_row, r), :]
        window = scr[...]
        rows = window.shape[0]
        acc = jnp.zeros((r, CONV_CH), F32) + cb_ref[...]
        for rho in range(SUBLANES):
            lo = CONV_HALO - hist + rho
            shifted = pltpu.roll(window, rows - lo, 0)
            for k in range(-(-CONV_WIDTH // SUBLANES)):
                w = SUBLANES * k + rho
                if w < CONV_WIDTH:
                    acc = acc + shifted[SUBLANES * k:SUBLANES * k + r, :] * cw_ref[w:w + 1, :]
        y = _layer_norm(acc, g_ref[...], beta_ref[...])
        y_ref[pl.ds(first_row, r), :] = y * _sigmoid(y)

    def last_rows(scr):
        return scr[r + CONV_HALO - hist:r + CONV_HALO, :]

    if nj == 1:
        for seq in range(nseq):
            scr = seq_scr.at[seq]
            start_window(scr, seq)
            run_tile(scr, seq * r)
            nbuf_ref[seq] = last_rows(scr)
    else:
        j = pl.program_id(1)
        scr = seq_scr.at[0]

        @pl.when(j == 0)
        def _():
            start_window(scr, 0)

        @pl.when(j > 0)
        def _():
            scr[0:CONV_HALO, :] = scr[r:r + CONV_HALO, :]

        run_tile(scr, 0)

        @pl.when(j == nj - 1)
        def _():
            nbuf_ref[0] = last_rows(scr)


def _conv(u, buf, cw, cb, g, beta, nb, t):
    r = min(t, CONV_TILE)
    nj = t // r
    assert nj == 1 or r >= CONV_HALO
    hist = CONV_WIDTH - 1
    nseq = _seqs_per_step(nb, nj, r)
    return pl.pallas_call(
        functools.partial(_conv_body, nj=nj),
        grid=(nb // nseq, nj),
        in_specs=[pl.BlockSpec((nseq * r, CONV_CH), lambda b, j: (b * nj + j, 0)),
                  pl.BlockSpec((nseq, hist, CONV_CH), lambda b, j: (b, 0, 0)),
                  _const_spec(cw.shape), _const_spec(cb.shape), _const_spec(g.shape), _const_spec(beta.shape)],
        out_specs=[pl.BlockSpec((nseq * r, CONV_CH), lambda b, j: (b * nj + j, 0)),
                   pl.BlockSpec((nseq, hist, CONV_CH), lambda b, j: (b, 0, 0))],
        out_shape=[jax.ShapeDtypeStruct((nb * t, CONV_CH), F32), jax.ShapeDtypeStruct((nb, hist, CONV_CH), F32)],
        scratch_shapes=[pltpu.VMEM((nseq, r + CONV_HALO, CONV_CH), F32)],
        compiler_params=_params("parallel", "arbitrary"),
        name="conv",
    )(u, buf, cw, cb, g, beta)


def _rel_bucket(dist):
    n = jnp.maximum(dist, 0)
    max_exact = REL_BUCKETS // 2
    nf = jnp.maximum(n, 1).astype(F32)
    large = max_exact + (jnp.log(nf / max_exact) / math.log(REL_MAX_DIST / max_exact)
                         * (REL_BUCKETS - max_exact)).astype(jnp.int32)
    large = jnp.minimum(large, REL_BUCKETS - 1)
    return jnp.where(n < max_exact, n, large)


def _bias_body(rb_ref, idx_ref, o_ref):
    h = pl.program_id(1)
    idx = idx_ref[0]
    acc = jnp.full(idx.shape, -jnp.inf, F32)
    for bucket in range(REL_BUCKETS):
        acc = jnp.where(idx == bucket, rb_ref[bucket, h], acc)
    o_ref[0, 0] = (acc - rb_ref[REL_BUCKETS - 1, h]) * LOG2E


def _bias_tiles(rel_bias, t):
    assert t >= REL_MAX_DIST
    ii = jnp.arange(t, dtype=jnp.int32)[:, None]
    jj = jnp.arange(t, dtype=jnp.int32)[None, :]
    idx = jnp.stack([jnp.where(ii >= jj, _rel_bucket(ii - jj), -1), _rel_bucket(t + ii - jj)])
    nh = rel_bias.shape[1]
    return pl.pallas_call(
        _bias_body,
        grid=(2, nh),
        in_specs=[pl.BlockSpec(memory_space=pltpu.SMEM),
                  pl.BlockSpec((1, t, t), lambda r, h: (r, 0, 0))],
        out_specs=pl.BlockSpec((1, 1, t, t), lambda r, h: (r, h, 0, 0)),
        out_shape=jax.ShapeDtypeStruct((2, nh, t, t), F32),
        compiler_params=_params("arbitrary", "arbitrary"),
        name="rel_bias_tiles",
    )(rel_bias, idx)


def _attn_body(lam_ref, q_ref, kt_ref, v_ref, bias_ref, dnw_ref, *rest, out_scale, nstack):
    stack_in = rest[:2 * nstack]
    o_ref = rest[2 * nstack]
    stack_out = rest[2 * nstack + 1:2 * nstack + 1 + (2 if nstack else 0)]
    q_scr, s_scr, m_scr, l_scr, acc_scr = rest[len(rest) - 5:]
    for layer in range(nstack):
        stack_out[0][layer, 0] = stack_in[layer][0]
        stack_out[1][layer] = stack_in[nstack + layer][...]
    i = pl.program_id(1)
    t = q_ref.shape[0]
    nmaps = 2 * DIFF_HEADS
    lane = lax.broadcasted_iota(jnp.int32, (t, LANES), 1)
    for n in range(nmaps):
        qh = q_ref[:, (n // 2) * LANES:(n // 2 + 1) * LANES]
        keep = (lane < DIFF_DH) if n % 2 == 0 else (lane >= DIFF_DH)
        q_scr[n] = jnp.where(keep, qh, jnp.zeros_like(qh))
    m_scr[...] = jnp.full(m_scr.shape, -jnp.inf, F32)
    l_scr[...] = jnp.zeros(l_scr.shape, F32)
    acc_scr[...] = jnp.zeros(acc_scr.shape, F32)

    def tile(j, which, ntile=1):
        width = ntile * t
        keys = pl.ds(pl.multiple_of(j * t, t), width)
        for n in range(nmaps):
            s = _mm(q_scr[n], kt_ref[0, (n // 2) * LANES:(n // 2 + 1) * LANES, keys])
            if which == "near":
                s = s + jnp.concatenate([bias_ref[1, n], bias_ref[0, n]], axis=1)
            elif which is not None:
                s = s + bias_ref[which, n]
            s_scr[n, :, 0:width] = s
        for n in range(nmaps):
            s = s_scr[n, :, 0:width]
            m_old = m_scr[n]
            row_max = jnp.max(_fold_lanes(s, jnp.maximum), axis=-1, keepdims=True)
            m_new = jnp.maximum(m_old, jnp.broadcast_to(row_max, (t, LANES)))
            alpha = jnp.exp2(m_old - m_new)
            ps = [jnp.exp2(s[:, c * LANES:(c + 1) * LANES] - m_new) for c in range(width // LANES)]
            part = ps[0]
            for pc in ps[1:]:
                part = part + pc
            l_scr[n] = alpha * l_scr[n] + part
            pv = _mm(jnp.concatenate(ps, axis=1).astype(BF16), v_ref[keys, (n // 2) * LANES:(n // 2 + 1) * LANES])
            acc_scr[n] = alpha * acc_scr[n] + pv
            m_scr[n] = m_new

    nfar = jnp.maximum(i - 1, 0)

    def far_pair(c, carry):
        tile(2 * c, None, 2)
        return carry

    lax.fori_loop(0, nfar // 2, far_pair, 0)

    @pl.when(nfar % 2 == 1)
    def _():
        tile(nfar - 1, None)

    @pl.when(i >= 1)
    def _():
        tile(i - 1, "near", 2)

    @pl.when(i == 0)
    def _():
        tile(0, 0)

    lam = lam_ref[0]
    outs = []
    for hp in range(DIFF_HEADS):
        l0 = jnp.sum(l_scr[2 * hp], axis=-1, keepdims=True)
        l1 = jnp.sum(l_scr[2 * hp + 1], axis=-1, keepdims=True)
        o = acc_scr[2 * hp] / l0 - lam * (acc_scr[2 * hp + 1] / l1)
        ms_o = jnp.mean(o * o, axis=-1, keepdims=True)
        outs.append(o * lax.rsqrt(ms_o + EPS) * dnw_ref[...] * out_scale)
    o_ref[...] = jnp.concatenate(outs, axis=1)


def _attn_prompt(q16, kt16, v16, bias, lam, dnw, nb, s, out_scale, stack_kt=(), stack_v=()):
    t = bias.shape[-1]
    nq = s // t
    nmaps = 2 * DIFF_HEADS
    nstack = len(stack_kt)
    assert len(stack_v) == nstack
    in_specs = [pl.BlockSpec(memory_space=pltpu.SMEM),
                pl.BlockSpec((t, DIFF_VW), lambda b, i: (b * nq + i, 0)),
                pl.BlockSpec((1, DIFF_VW, s), lambda b, i: (b, 0, 0)),
                pl.BlockSpec((s, DIFF_VW), lambda b, i: (b, 0)),
                _const_spec(bias.shape), _const_spec(dnw.shape)]
    in_specs += [pl.BlockSpec((1, DIFF_VW, t), lambda b, i: (b, 0, i))] * nstack
    in_specs += [pl.BlockSpec((t * DIFF_HEADS, LANES), lambda b, i: (b * nq + i, 0))] * nstack
    out_specs = [pl.BlockSpec((t, DIFF_VW), lambda b, i: (b * nq + i, 0))]
    out_shape = [jax.ShapeDtypeStruct((nb * s, DIFF_VW), F32)]
    if nstack:
        out_specs += [pl.BlockSpec((nstack, 1, DIFF_VW, t), lambda b, i: (0, b, 0, i)),
                      pl.BlockSpec((nstack, t * DIFF_HEADS, LANES), lambda b, i: (0, b * nq + i, 0))]
        out_shape += [jax.ShapeDtypeStruct((nstack, nb, DIFF_VW, s), F32),
                      jax.ShapeDtypeStruct((nstack, nb * s * DIFF_HEADS, LANES), F32)]
    out = pl.pallas_call(
        functools.partial(_attn_body, out_scale=out_scale, nstack=nstack),
        grid=(nb, nq),
        in_specs=in_specs,
        out_specs=out_specs,
        out_shape=out_shape,
        scratch_shapes=[pltpu.VMEM((nmaps, t, LANES), BF16), pltpu.VMEM((nmaps, t, 2 * t), F32),
                        pltpu.VMEM((nmaps, t, LANES), F32), pltpu.VMEM((nmaps, t, LANES), F32),
                        pltpu.VMEM((nmaps, t, LANES), F32)],
        compiler_params=_params("parallel", "arbitrary"),
        name="diff_attn_prompt",
    )(lam, q16, kt16, v16, bias, dnw, *stack_kt, *stack_v)
    return out if nstack else out[0]


def _sattn_phases(step, nsteps, lam_ref, q_ref, kn_ref, vn_ref, d1_ref, d0_ref, dnw_ref, kp, vp, o_ref, m_scr, l_scr,
                  acc_scr, out_scale):
    npg = len(kp)
    nh = 2 * DIFF_HEADS
    tq = q_ref.shape[0]
    rows_h = 2 * tq

    q = q_ref[...]
    lane_head = lax.broadcasted_iota(jnp.int32, q.shape, 1) // DIFF_DH
    qs = jnp.concatenate([jnp.where(lane_head == hh, q, 0.0) for hh in range(nh)], axis=0).astype(BF16)

    if step == 0:
        m_scr[...] = jnp.full(m_scr.shape, -jnp.inf, F32)
        l_scr[...] = jnp.zeros(l_scr.shape, F32)
        acc_scr[...] = jnp.zeros(acc_scr.shape, F32)

    def update(s, values_of):
        m_old = m_scr[...]
        m_new = jnp.maximum(m_old, jnp.max(s, axis=-1, keepdims=True))
        alpha = jnp.exp2(m_old - m_new)
        p = jnp.exp2(s - m_new)
        l_scr[...] = alpha * l_scr[...] + jnp.sum(p, axis=-1, keepdims=True)
        pvs = [_mm(p[h * rows_h:(h + 1) * rows_h].astype(BF16), values_of(h)) for h in range(DIFF_HEADS)]
        acc_scr[...] = alpha * acc_scr[...] + jnp.concatenate(pvs, axis=0)
        m_scr[...] = m_new

    is_last = step == nsteps - 1

    def logits():
        kt = jnp.concatenate([kp[g][...].astype(BF16) for g in range(npg)], axis=1)
        s = _mm(qs, kt)
        if is_last:
            zeros = jnp.zeros((s.shape[0], s.shape[1] - PAGE_SIZE), F32)
            s = s + jnp.concatenate([zeros, d1_ref[...]], axis=1)
        return s

    def finish(s):
        update(s, lambda h: jnp.concatenate(
            [vp[g][pl.ds(h, PAGE_SIZE, stride=DIFF_HEADS), :].astype(BF16) for g in range(npg)], axis=0))
        if not is_last:
            return
        pad = jnp.zeros((PAGE_SIZE - tq, DIFF_VW), F32)
        kn = jnp.concatenate([kn_ref[...], pad], axis=0).astype(BF16)
        vn = jnp.concatenate([vn_ref[...], pad], axis=0).astype(BF16)
        update(_nt(qs, kn) + d0_ref[...], lambda h: vn[:, h * LANES:(h + 1) * LANES])
        o = acc_scr[...] / l_scr[...]
        lam = lam_ref[0]
        outs = []
        for h in range(DIFF_HEADS):
            oh = o[h * rows_h:h * rows_h + tq] - lam * o[h * rows_h + tq:(h + 1) * rows_h]
            ms = jnp.mean(oh * oh, axis=-1, keepdims=True)
            outs.append(oh * lax.rsqrt(ms + EPS) * dnw_ref[...] * out_scale)
        o_ref[...] = jnp.concatenate(outs, axis=1)

    return logits, finish


_MIX_CONSTS = ("w_out", "ln1_g", "ln1_b", "w_ffn_gate", "w_ffn_up", "w_ffn_down", "w_ple_gate", "w_ple_proj",
               "ln2_g", "ln2_b")


def _mix_head(oa_ref, ob_ref, oc_ref, x_ref, pe_ref, wo_ref, g1_ref, b1_ref, wpg_ref, wpp_ref):
    mix = _mm(oa_ref[...].astype(BF16), wo_ref[0:GLA_VW, :])
    mix = mix + _mm(ob_ref[...].astype(BF16), wo_ref[GLA_VW:GLA_VW + CONV_CH, :])
    mix = mix + _mm(oc_ref[...].astype(BF16), wo_ref[GLA_VW + CONV_CH:, :])
    x = _layer_norm(ALPHA * x_ref[...] + mix, g1_ref[...], b1_ref[...])
    xb = x.astype(BF16)
    ple = _sigmoid(_mm(xb, wpg_ref[...])) * _mm(pe_ref[...].astype(BF16), wpp_ref[...])
    return xb, ALPHA * x + ple


def _ffn_cols(xb, wg_ref, wu_ref, wd_ref, lo, hi):
    acc = None
    c = lo
    while c < hi:
        w = min(FFN_CHUNK, hi - c)
        gate = _mm(xb, wg_ref[:, c:c + w])
        hid = gate * _sigmoid(gate) * _mm(xb, wu_ref[:, c:c + w])
        part = _mm(hid.astype(BF16), wd_ref[c:c + w, :])
        acc = part if acc is None else acc + part
        c += w
    return acc


def _mix_ffn_body(oa_ref, ob_ref, oc_ref, x_ref, pe_ref, wo_ref, g1_ref, b1_ref, wg_ref, wu_ref, wd_ref, wpg_ref,
                  wpp_ref, g2_ref, b2_ref, y_ref):
    xb, base = _mix_head(oa_ref, ob_ref, oc_ref, x_ref, pe_ref, wo_ref, g1_ref, b1_ref, wpg_ref, wpp_ref)
    acc = base + _ffn_cols(xb, wg_ref, wu_ref, wd_ref, 0, wg_ref.shape[1])
    y_ref[...] = _layer_norm(acc, g2_ref[...], b2_ref[...])


def _mix_ffn(oa, ob, oc, x2, pe_all, layer, lw):
    m, d = x2.shape
    tm = min(MIX_ROWS, m)
    nt = m // tm
    consts = [lw[n] for n in _MIX_CONSTS]

    def rows(n):
        return pl.BlockSpec((tm, n), lambda i: (i, 0))

    return pl.pallas_call(
        _mix_ffn_body,
        grid=(nt,),
        in_specs=[rows(GLA_VW), rows(CONV_CH), rows(DIFF_VW), rows(d),
                  pl.BlockSpec((tm, pe_all.shape[1]), lambda i: (layer * nt + i, 0))]
                 + [_const_spec(a.shape, layer if a.ndim == 3 else None) for a in consts],
        out_specs=rows(d),
        out_shape=jax.ShapeDtypeStruct((m, d), F32),
        compiler_params=_params("parallel"),
        name="mix_ffn",
    )(oa, ob, oc, x2, pe_all, *consts)


def _sattn_body(pt_ref, lam_ref, q_ref, kn_ref, vn_ref, d1_ref, d0_ref, dnw_ref, *rest, npg, nsteps, out_scale):
    del pt_ref
    kp = rest[:npg]
    vp = rest[npg:2 * npg]
    o_ref = rest[2 * npg]
    m_scr, l_scr, acc_scr = rest[2 * npg + 1:]
    p_id = pl.program_id(1)
    for step in range(nsteps):
        @pl.when(p_id == step)
        def _(step=step):
            logits, finish = _sattn_phases(step, nsteps, lam_ref, q_ref, kn_ref, vn_ref, d1_ref, d0_ref, dnw_ref,
                                           kp, vp, o_ref, m_scr, l_scr, acc_scr, out_scale)
            finish(logits())


def _attn_sample(dq, dk, dv, cache_kt, cache_v2, page_table, layer, bias, lam, dnw, out_scale):
    nb, n_pages = page_table.shape
    tq = dq.shape[0] // nb
    npg = PAGES_PER_STEP
    nsteps = n_pages // npg
    assert n_pages % npg == 0 and cache_kt.shape[-1] == PAGE_SIZE and tq == SUBLANES
    nh = 2 * DIFF_HEADS
    t = bias.shape[-1]
    d1 = bias[1, :, 0:tq, t - PAGE_SIZE:t].reshape(nh * tq, PAGE_SIZE)
    d0 = jnp.concatenate([bias[0, :, 0:tq, 0:tq], jnp.full((nh, tq, PAGE_SIZE - tq), -jnp.inf, F32)],
                         axis=-1).reshape(nh * tq, PAGE_SIZE)

    def page_spec(g):
        return pl.BlockSpec((None, None, DIFF_VW, PAGE_SIZE),
                            lambda b, p, pt: (layer, pt[b * n_pages + p * npg + g], 0, 0))

    def rows_spec():
        return pl.BlockSpec((tq, DIFF_VW), lambda b, p, pt: (b, 0))

    def full_spec(a):
        nd = a.ndim
        return pl.BlockSpec(a.shape, lambda b, p, pt: (0,) * nd)

    grid_spec = pltpu.PrefetchScalarGridSpec(
        num_scalar_prefetch=1,
        grid=(nb, nsteps),
        in_specs=[pl.BlockSpec(memory_space=pltpu.SMEM), rows_spec(), rows_spec(), rows_spec(),
                  full_spec(d1), full_spec(d0), full_spec(dnw)]
                 + [page_spec(g) for g in range(npg)] + [page_spec(g) for g in range(npg)],
        out_specs=rows_spec(),
        scratch_shapes=[pltpu.VMEM((nh * tq, 1), F32), pltpu.VMEM((nh * tq, 1), F32),
                        pltpu.VMEM((nh * tq, LANES), F32)],
    )
    return pl.pallas_call(
        functools.partial(_sattn_body, npg=npg, nsteps=nsteps, out_scale=out_scale),
        grid_spec=grid_spec,
        out_shape=jax.ShapeDtypeStruct((nb * tq, DIFF_VW), F32),
        compiler_params=_params("parallel", "arbitrary"),
        name="diff_attn_sample",
    )(page_table.reshape(-1), lam, dq, dk, dv, d1, d0, dnw, *([cache_kt] * npg), *([cache_v2] * npg))


def _row(a):
    return a.reshape(1, -1)


def _token_mixers(x2, nb, t, lw, s0, buf, transposed_k):
    gq, gk, gv, gs, lg, u, dq, dv, *kk = _in_proj(x2, lw["w_in"], lw["wk_t"], lw["wg"], lw["bg"], nb, transposed_k,
                                                  lw["layer"])
    o_a, s_new = _gla(gq, gk, gv, lg, gs, lw["gla_nw"], s0.reshape(nb, GLA_QK, GLA_DV), nb, t)
    o_b, nbuf = _conv(u, buf, lw["conv_w"], lw["conv_b"], lw["conv_g"], lw["conv_beta"], nb, t)
    return o_a, o_b, dq, dv, kk, s_new.reshape(nb, GLA_HEADS, GLA_DK, GLA_DV), nbuf


def _prep_w_in(w):
    sizes = (GLA_QK, GLA_QK, GLA_VW, GLA_VW, GLA_GATE_RANK, CONV_CH, CONV_CH, DIFF_VW, DIFF_VW, DIFF_VW)
    parts, s = [], 0
    for n in sizes:
        parts.append(w[:, :, s:s + n])
        s += n
    wk_t = jnp.swapaxes(parts[8], 1, 2).astype(BF16)
    parts[4] = jnp.pad(parts[4], ((0, 0), (0, 0), (0, LANES - GLA_GATE_RANK)))
    return jnp.concatenate(parts, axis=2).astype(BF16), wk_t


def _stacked_weights(w_in, gla_w_gate_up, w_out, w_ffn_gate, w_ffn_up, w_ffn_down, w_ple_gate, w_ple_proj):
    w_in_p, wk_t = _prep_w_in(w_in)
    return dict(
        w_in=w_in_p, wk_t=wk_t,
        wg=jnp.pad(gla_w_gate_up, ((0, 0), (0, LANES - GLA_GATE_RANK), (0, 0))).astype(BF16),
        w_out=w_out.astype(BF16), w_ffn_gate=w_ffn_gate.astype(BF16), w_ffn_up=w_ffn_up.astype(BF16),
        w_ffn_down=w_ffn_down.astype(BF16), w_ple_gate=w_ple_gate.astype(BF16), w_ple_proj=w_ple_proj.astype(BF16),
    )


def _layer_vectors(i, gla_b_gate, gla_norm_w, conv_w, conv_b, conv_ln_g, conv_ln_b, diff_lq1, diff_lk1, diff_lq2,
                   diff_lk2, diff_norm_w, ln1_g, ln1_b, ln2_g, ln2_b):
    lam_init = 0.8 - 0.6 * math.exp(-0.3 * i)
    lam = (jnp.exp(jnp.sum(diff_lq1[i] * diff_lk1[i])) - jnp.exp(jnp.sum(diff_lq2[i] * diff_lk2[i]))
           + lam_init).reshape(1).astype(F32)
    return dict(
        layer=i,
        bg=_row(gla_b_gate[i]),
        gla_nw=_row(jnp.tile(gla_norm_w[i], GLA_HEADS)),
        conv_w=conv_w[i], conv_b=_row(conv_b[i]), conv_g=_row(conv_ln_g[i]), conv_beta=_row(conv_ln_b[i]),
        lam=lam, dnw=_row(diff_norm_w[i]),
        ln1_g=_row(ln1_g[i]), ln1_b=_row(ln1_b[i]), ln2_g=_row(ln2_g[i]), ln2_b=_row(ln2_b[i]),
    )


def kernel(x_prompt, x_sample, cache_k, cache_v, state_gla, state_conv, page_table, p_prompt, p_sample, w_in, gla_w_gate_up, gla_b_gate, gla_norm_w, conv_w, conv_b, conv_ln_g, conv_ln_b, diff_lq1, diff_lk1, diff_lq2, diff_lk2, diff_norm_w, rel_bias, w_out, ln1_g, ln1_b, w_ffn_gate, w_ffn_up, w_ffn_down, w_ple_gate, w_ple_proj, ln2_g, ln2_b):
    nbp, s, d = x_prompt.shape
    nbs, ts, _ = x_sample.shape
    depth, n_pool = cache_k.shape[:2]
    assert depth == DEPTH and cache_k.shape[2] == PAGE_SIZE
    cache_kt = jnp.transpose(cache_k, (0, 1, 3, 4, 2)).reshape(depth, n_pool, DIFF_VW, PAGE_SIZE)
    cache_v2 = cache_v.reshape(depth, n_pool, PAGE_SIZE * DIFF_HEADS, 2 * DIFF_DH)
    bias = _bias_tiles(rel_bias, ATTN_TILE)

    pe_prompt = p_prompt.reshape(depth * nbp * s, -1)
    pe_sample = p_sample.reshape(depth * nbs * ts, -1)
    yp = x_prompt.reshape(nbp * s, d)
    ys = x_sample.reshape(nbs * ts, d)
    zero_state = jnp.zeros((nbp, GLA_HEADS, GLA_DK, GLA_DV), F32)
    zero_buf = jnp.zeros((nbp, CONV_WIDTH - 1, CONV_CH), F32)
    outs = [[] for _ in range(6)]
    kts, vrs = [], []
    stacked = _stacked_weights(w_in, gla_w_gate_up, w_out, w_ffn_gate, w_ffn_up, w_ffn_down, w_ple_gate, w_ple_proj)
    for i in range(depth):
        lw = dict(stacked, **_layer_vectors(i, gla_b_gate, gla_norm_w, conv_w, conv_b, conv_ln_g, conv_ln_b, diff_lq1,
                                            diff_lk1, diff_lq2, diff_lk2, diff_norm_w, ln1_g, ln1_b, ln2_g, ln2_b))

        out_scale = 1.0 - (0.8 - 0.6 * math.exp(-0.3 * i))
        pa, pb, q16, v_rows, (kt_i, kt16, v16), s_i, c_i = _token_mixers(yp, nbp, s, lw, zero_state, zero_buf, True)
        kts.append(kt_i)
        vrs.append(v_rows)
        if i < depth - 1:
            pc = _attn_prompt(q16, kt16, v16, bias, lw["lam"], lw["dnw"], nbp, s, out_scale)
        else:
            pc, k_stack, v_stack = _attn_prompt(q16, kt16, v16, bias, lw["lam"], lw["dnw"], nbp, s, out_scale,
                                                kts, vrs)
        outs[0].append(s_i)
        outs[1].append(c_i)
        sa, sb, dq, dv, (dk,), s_i, c_i = _token_mixers(ys, nbs, ts, lw, state_gla[i], state_conv[i], False)
        outs[2].append(dk.reshape(nbs, ts, 2 * DIFF_HEADS, DIFF_DH))
        outs[3].append(dv.reshape(nbs, ts, DIFF_HEADS, 2 * DIFF_DH))
        outs[4].append(s_i)
        outs[5].append(c_i)
        sc = _attn_sample(dq, dk, dv, cache_kt, cache_v2, page_table, i, bias, lw["lam"], lw["dnw"], out_scale)
        yp = _mix_ffn(pa, pb, pc, yp, pe_prompt, i, lw)
        ys = _mix_ffn(sa, sb, sc, ys, pe_sample, i, lw)
    k_prompt = k_stack.reshape(depth, nbp, 2 * DIFF_HEADS, DIFF_DH, s).transpose(0, 1, 4, 2, 3)
    v_prompt = v_stack.reshape(depth, nbp, s, DIFF_HEADS, 2 * DIFF_DH)
    return (yp.reshape(nbp, s, d), ys.reshape(nbs, ts, d), k_prompt, v_prompt) + tuple(jnp.stack(o) for o in outs)
```
